```python
import math
import jax
import jax.numpy as jnp
from jax import lax
import numpy as np

D_MODEL = 1024
BATCH = 4
SEQ = 4096
DEPTH = 4
DEC_BATCH = 32
DEC_SEQ = 8
PAST_LEN = 8192
PAGE_SIZE = 128

MIX_W = D_MODEL // 2
N_BRANCH = 4
A_HEADS = 4
A_DK = MIX_W // A_HEADS
A_DV = MIX_W // A_HEADS
A_QKV = A_HEADS * (2 * A_DK + A_DV)
A_CONV = 4
A_CHUNK = 64
B_HEADS = 4
B_DV = MIX_W // B_HEADS
B_DK = B_DV // 2
B_CHUNK = 128
ROPE_BASE = 10000.0
C_GROUPS = 4
C_GW = MIX_W // C_GROUPS
POOL_WINDOWS = (2, 4, 8, 16)
POOL_MAX = 16
D_DILATIONS = (1, 4, 16)
N_DG = 3
D_SPAN = 128
D_HEADS = 4
D_HD = MIX_W // D_HEADS
REL_BUCKETS = 32
REL_MAX_DIST = D_SPAN * 16
D_FF = -(-8 * D_MODEL // (3 * 256)) * 256
PLE_DIM = 256
EPS = 1e-6
IN_SIZES = (A_QKV, MIX_W, A_HEADS, A_HEADS,
            B_HEADS * B_DK, B_HEADS * B_DK, MIX_W, MIX_W,
            MIX_W,
            3 * N_DG * MIX_W,
            N_BRANCH * D_MODEL)
IN_W = sum(IN_SIZES)

kernel_name = 'hybrid_gated_branch_decoder_step'


def _split_points():
    pts, acc = [], 0
    for s in IN_SIZES[:-1]:
        acc += s
        pts.append(acc)
    return pts


def rmsnorm(x, g):
    xf = x.astype(jnp.float32)
    y = xf * lax.rsqrt(jnp.mean(xf * xf, axis=-1, keepdims=True) + EPS)
    return (y * g).astype(x.dtype)


def rms_heads(x):
    return x * lax.rsqrt(jnp.mean(x * x, axis=-1, keepdims=True) + EPS)


def l2norm(x):
    return x * lax.rsqrt(jnp.sum(x * x, axis=-1, keepdims=True) + EPS)


def rotary(x, pos):
    half = x.shape[-1] // 2
    inv = ROPE_BASE ** (-jnp.arange(half, dtype=jnp.float32) / half)
    ang = pos.astype(jnp.float32)[:, None] * inv[None, :]
    cos = jnp.cos(ang)[None, :, None, :]
    sin = jnp.sin(ang)[None, :, None, :]
    x1, x2 = x[..., :half], x[..., half:]
    return jnp.concatenate([x1 * cos - x2 * sin, x1 * sin + x2 * cos], axis=-1)


def t5_bucket(dist):
    exact = REL_BUCKETS // 2
    n = jnp.maximum(dist, 0)
    nf = jnp.maximum(n, 1).astype(jnp.float32)
    large = exact + (jnp.log(nf / exact) / math.log(REL_MAX_DIST / exact)
                     * (REL_BUCKETS - exact)).astype(jnp.int32)
    large = jnp.minimum(large, REL_BUCKETS - 1)
    return jnp.where(n < exact, n, large)


def chunked(step_fn, state0, xs, chunk):
    T = xs[0].shape[2]
    if T % chunk:
        chunk = T
    nc = T // chunk
    split = lambda x: jnp.moveaxis(x.reshape(x.shape[:2] + (nc, chunk) + x.shape[3:]), 2, 0)
    state, o = lax.scan(lambda s, c: step_fn(s, *c), state0, tuple(split(x) for x in xs))
    o = jnp.moveaxis(o, 0, 2)
    return state, o.reshape(o.shape[:2] + (T,) + o.shape[4:])


def delta_chunk(s, q, k, v, beta, g):
    L = q.shape[2]
    gc = jnp.cumsum(g, axis=-1)
    causal = jnp.tril(jnp.ones((L, L), dtype=bool))
    strict = jnp.tril(jnp.ones((L, L), dtype=bool), -1)
    diff = gc[..., :, None] - gc[..., None, :]
    gamma = jnp.where(causal, jnp.exp(jnp.where(causal, diff, 0.0)), 0.0)
    kk = jnp.einsum('bhlk,bhmk->bhlm', k, k)
    m = jnp.where(strict, beta[..., None] * kk * gamma, 0.0) + jnp.eye(L, dtype=kk.dtype)
    rhs = jnp.concatenate([v * beta[..., None], k * (beta * jnp.exp(gc))[..., None]], axis=-1)
    sol = jax.lax.linalg.triangular_solve(m, rhs, left_side=True, lower=True, unit_diagonal=True)
    u, w = sol[..., :A_DV], sol[..., A_DV:]
    v_new = u - jnp.einsum('bhlk,bhkv->bhlv', w, s)
    qk = jnp.einsum('bhlk,bhmk->bhlm', q, k) * gamma
    o = (jnp.einsum('bhlk,bhkv->bhlv', q * jnp.exp(gc)[..., None], s)
         + jnp.einsum('bhlm,bhmv->bhlv', qk, v_new))
    g_last = gc[..., -1:]
    s_new = (s * jnp.exp(g_last)[..., None]
             + jnp.einsum('bhlk,bhlv->bhkv', k * jnp.exp(g_last - gc)[..., None], v_new))
    return s_new, o


def retention_chunk(r, q, k, v, log_gamma):
    L = q.shape[2]
    idx = jnp.arange(L, dtype=jnp.float32)
    causal = idx[:, None] >= idx[None, :]
    lg = log_gamma[:, None, None]
    decay = jnp.where(causal, jnp.exp(lg * jnp.where(causal, idx[:, None] - idx[None, :], 0.0)), 0.0)
    scores = jnp.einsum('bhlk,bhmk->bhlm', q, k) * decay
    o = (jnp.einsum('bhlm,bhmv->bhlv', scores, v)
         + jnp.einsum('bhlk,bhkv->bhlv', q, r) * jnp.exp(lg * (idx + 1.0)[:, None]))
    r_new = (r * jnp.exp(log_gamma * L)[:, None, None]
             + jnp.einsum('bhlk,bhlv->bhkv', k * jnp.exp(lg * (L - 1.0 - idx)[:, None]), v))
    return r_new, o


def mixer_delta(qkv_pre, z, b_raw, a_raw, conv_prev, s_prev, conv_w, a_log, dt_bias, a_gain):
    Bn, T, _ = qkv_pre.shape
    ext = jnp.concatenate([conv_prev.astype(jnp.float32), qkv_pre], axis=1)
    acc = ext[:, :T] * conv_w[0]
    for j in range(1, A_CONV):
        acc = acc + ext[:, j:j + T] * conv_w[j]
    qkv = jax.nn.silu(acc)
    q, k, v = jnp.split(qkv, [A_HEADS * A_DK, 2 * A_HEADS * A_DK], axis=-1)
    q = l2norm(q.reshape(Bn, T, A_HEADS, A_DK)) * (A_DK ** -0.5)
    k = l2norm(k.reshape(Bn, T, A_HEADS, A_DK))
    v = v.reshape(Bn, T, A_HEADS, A_DV)
    beta = jax.nn.sigmoid(b_raw)
    g = -jnp.exp(a_log) * jax.nn.softplus(a_raw + dt_bias)
    sw = lambda t: jnp.swapaxes(t, 1, 2)
    s_new, o = chunked(delta_chunk, s_prev.astype(jnp.float32),
                       (sw(q), sw(k), sw(v), sw(beta), sw(g)), A_CHUNK)
    o = rms_heads(jnp.swapaxes(o, 1, 2)) * a_gain * jax.nn.silu(z.reshape(Bn, T, A_HEADS, A_DV))
    return o.reshape(Bn, T, MIX_W), ext[:, T:], s_new


def mixer_ret(q, k, v, gate, pos, r_prev):
    Bn, T, _ = q.shape
    q = rotary(q.reshape(Bn, T, B_HEADS, B_DK), pos)
    k = rotary(k.reshape(Bn, T, B_HEADS, B_DK), pos) * (B_DK ** -0.5)
    v = v.reshape(Bn, T, B_HEADS, B_DV)
    log_gamma = jnp.log1p(-jnp.exp2(-5.0 - jnp.arange(B_HEADS, dtype=jnp.float32)))
    step = lambda r, qc, kc, vc: retention_chunk(r, qc, kc, vc, log_gamma)
    sw = lambda t: jnp.swapaxes(t, 1, 2)
    r_new, o = chunked(step, r_prev.astype(jnp.float32), (sw(q), sw(k), sw(v)), B_CHUNK)
    o = rms_heads(jnp.swapaxes(o, 1, 2)) * jax.nn.silu(gate.reshape(Bn, T, B_HEADS, B_DV))
    return o.reshape(Bn, T, MIX_W), r_new


def mixer_pool(c_in, pos, prev, c_w, c_scale):
    Bn, T, C = c_in.shape
    P = POOL_MAX - 1
    ext = jnp.concatenate([prev.astype(jnp.float32), c_in], axis=1)
    cs = jnp.concatenate([jnp.zeros((Bn, 1, C), jnp.float32), jnp.cumsum(ext, axis=1)], axis=1)
    outs = []
    for gi, w in enumerate(POOL_WINDOWS):
        sl = slice(gi * C_GW, (gi + 1) * C_GW)
        win_sum = cs[:, P + 1:P + 1 + T, sl] - cs[:, P + 1 - w:P + 1 - w + T, sl]
        cnt = jnp.minimum(pos + 1, w).astype(jnp.float32)[None, :, None]
        outs.append(win_sum / cnt - ext[:, P:, sl])
    y = jnp.stack(outs, axis=2)
    y = jnp.einsum('btgc,gcd->btgd', y, c_w).reshape(Bn, T, C) * c_scale
    return y, ext[:, T:]


def dilated_band(q, k, v, bias_g, dil):
    Bn, S, H, E = q.shape
    L = S // dil
    BLK = D_SPAN
    nb = -(-L // BLK)
    pad = nb * BLK - L
    ZB = Bn * dil
    sub = lambda x: x.reshape(Bn, L, dil, H, E).transpose(0, 2, 1, 3, 4).reshape(ZB, L, H, E)
    qs = jnp.pad(sub(q), ((0, 0), (0, pad), (0, 0), (0, 0))).reshape(ZB, nb, BLK, H, E)

    def band_keys(x):
        xb = jnp.pad(sub(x), ((0, 0), (BLK, pad), (0, 0), (0, 0))).reshape(ZB, nb + 1, BLK, H, E)
        return jnp.concatenate([xb[:, :-1], xb[:, 1:]], axis=2)

    kb, vb = band_keys(k), band_keys(v)
    i = jnp.arange(BLK)[:, None]
    j = jnp.arange(2 * BLK)[None, :]
    rel = i + BLK - j
    band = (rel >= 0) & (rel <= D_SPAN)
    key_sub = jnp.arange(nb)[:, None, None] * BLK - BLK + j[None]
    mask = band[None] & (key_sub >= 0)
    bias = bias_g[t5_bucket(rel * dil)].transpose(2, 0, 1)
    s = jnp.einsum('znqhe,znkhe->znhqk', qs, kb) * (E ** -0.5) + bias[None, None]
    s = jnp.where(mask[None, :, None], s, -jnp.inf)
    lse = jax.nn.logsumexp(s, axis=-1)
    o = jnp.einsum('znhqk,znkhe->znqhe', jnp.exp(s - lse[..., None]), vb)
    o = o.reshape(ZB, nb * BLK, H, E)[:, :L].reshape(Bn, dil, L, H, E)
    o = o.transpose(0, 2, 1, 3, 4).reshape(Bn, S, H, E)
    lse = lse.transpose(0, 1, 3, 2).reshape(ZB, nb * BLK, H)[:, :L].reshape(Bn, dil, L, H)
    lse = lse.transpose(0, 2, 1, 3).reshape(Bn, S, H)
    return o, lse


def dilated_gather(q, k, v, kv_buf, bias_g, dil):
    Bn, T, H, E = q.shape
    Lb = kv_buf.shape[1]
    kv = jnp.concatenate([kv_buf.astype(jnp.float32), jnp.stack([k, v], axis=2)], axis=1)
    m = jnp.arange(D_SPAN + 1)
    idx = Lb + jnp.arange(T)[:, None] - m[None, :] * dil
    valid = idx >= 0
    gk = kv[:, jnp.clip(idx, 0)]
    bias = bias_g[t5_bucket(m * dil)].T
    s = jnp.einsum('bthe,btmhe->bhtm', q, gk[:, :, :, 0]) * (E ** -0.5) + bias[None, :, None, :]
    s = jnp.where(valid[None, None], s, -jnp.inf)
    lse = jax.nn.logsumexp(s, axis=-1)
    o = jnp.einsum('bhtm,btmhe->bthe', jnp.exp(s - lse[..., None]), gk[:, :, :, 1])
    return o, jnp.swapaxes(lse, 1, 2), kv[:, -Lb:]


def mixer_dilated(d_qkv, kv_bufs, t5_bias, is_prompt):
    Bn, T, _ = d_qkv.shape
    qkv = d_qkv.reshape(Bn, T, 3, N_DG, D_HEADS, D_HD)
    outs, lses, new = [], [], []
    for gi, dil in enumerate(D_DILATIONS):
        q, k, v = qkv[:, :, 0, gi], qkv[:, :, 1, gi], qkv[:, :, 2, gi]
        bias_g = t5_bias[:, gi * D_HEADS:(gi + 1) * D_HEADS]
        if is_prompt:
            o, lse = dilated_band(q, k, v, bias_g, dil)
            keep = min(D_SPAN * dil, T)
            new.append(jnp.stack([k[:, T - keep:], v[:, T - keep:]], axis=2))
        else:
            o, lse, buf = dilated_gather(q, k, v, kv_bufs[gi], bias_g, dil)
            new.append(buf)
        outs.append(o)
        lses.append(lse)
    wts = jax.nn.softmax(jnp.stack(lses), axis=0)
    o = jnp.einsum('gbth,gbthe->bthe', wts, jnp.stack(outs))
    return o.reshape(Bn, T, MIX_W), new


def layer(h, p_i, pos, st, lw, t5_bias, is_prompt):
    (g_mix, w_in, conv_w, a_log, dt_bias, a_gain, c_w, c_scale, w_br, w_o,
     g_ffn, w_gate, w_up, w_down, g_ple, w_ple_gate, w_ple) = lw
    s_delta, s_conv, s_ret, s_pool, kv_bufs = st
    Bn, T, _ = h.shape
    u = rmsnorm(h, g_mix)
    proj = jnp.einsum('btd,de->bte', u, w_in).astype(jnp.float32)
    (a_qkv, a_z, a_b, a_a, b_q, b_k, b_v, b_g, c_in, d_qkv, gate_raw) = jnp.split(proj, _split_points(), axis=-1)
    y_a, conv_new, delta_new = mixer_delta(a_qkv, a_z, a_b, a_a, s_conv, s_delta, conv_w, a_log, dt_bias, a_gain)
    y_b, ret_new = mixer_ret(b_q, b_k, b_v, b_g, pos, s_ret)
    y_c, pool_new = mixer_pool(c_in, pos, s_pool, c_w, c_scale)
    y_d, win_new = mixer_dilated(d_qkv, kv_bufs, t5_bias, is_prompt)
    branches = jnp.stack([y_a, y_b, y_c, y_d], axis=2)
    up = jnp.einsum('btnc,ncd->btnd', branches, w_br)
    gates = jax.nn.sigmoid(gate_raw.reshape(Bn, T, N_BRANCH, D_MODEL))
    merged = jnp.einsum('btnd,btnd->btd', gates, up)
    h = h + jnp.einsum('btd,de->bte', merged, w_o).astype(h.dtype)
    f = rmsnorm(h, g_ffn)
    ff = jax.nn.silu(f @ w_gate) * (f @ w_up)
    h = h + (ff @ w_down).astype(h.dtype)
    e = rmsnorm(h, g_ple)
    h = h + (jax.nn.sigmoid(e @ w_ple_gate) * (p_i @ w_ple)).astype(h.dtype)
    return h, (delta_new, conv_new, ret_new, pool_new, win_new[0], win_new[1], win_new[2])


def run_trunk(x, p, pos, states, is_prompt, layer_weights, t5_bias, g_final):
    Bn = x.shape[0]
    h = x
    per_layer = []
    for i in range(DEPTH):
        lw = tuple(w[i] for w in layer_weights)
        if is_prompt:
            st = (jnp.zeros((Bn, A_HEADS, A_DK, A_DV), jnp.float32),
                  jnp.zeros((Bn, A_CONV - 1, A_QKV), jnp.float32),
                  jnp.zeros((Bn, B_HEADS, B_DK, B_DV), jnp.float32),
                  jnp.zeros((Bn, POOL_MAX - 1, MIX_W), jnp.float32),
                  None)
        else:
            st = (states[0][i], states[1][i], states[2][i], states[3][i],
                  (states[4][i], states[5][i], states[6][i]))
        h, ns = layer(h, p[i], pos, st, lw, t5_bias, is_prompt)
        per_layer.append(ns)
    stacked = tuple(jnp.stack([ns[j] for ns in per_layer]) for j in range(len(per_layer[0])))
    return rmsnorm(h, g_final), stacked


def setup_inputs(seed: int = 0) -> dict:
    key = jax.random.key(seed)
    ks = iter(jax.random.split(key, 48))
    f32 = jnp.float32
    nrm = lambda shape, scale: scale * jax.random.normal(next(ks), shape, f32)
    gain = lambda shape: 1.0 + 0.01 * jax.random.normal(next(ks), shape, f32)
    lb = [min(D_SPAN * d, PAST_LEN) for d in D_DILATIONS]
    a_log = jnp.log(jax.random.uniform(next(ks), (DEPTH, A_HEADS), f32, 1.0, 16.0))
    dt = jnp.exp(jax.random.uniform(next(ks), (DEPTH, A_HEADS), f32, math.log(1e-3), math.log(1e-1)))
    dt_bias = dt + jnp.log(-jnp.expm1(-dt))
    return {
        'x_prompt': nrm((BATCH, SEQ, D_MODEL), 1.0),
        'x_sample': nrm((DEC_BATCH, DEC_SEQ, D_MODEL), 1.0),
        'state_delta': nrm((DEPTH, DEC_BATCH, A_HEADS, A_DK, A_DV), 0.1),
        'state_delta_conv': nrm((DEPTH, DEC_BATCH, A_CONV - 1, A_QKV), 1.0),
        'state_ret': nrm((DEPTH, DEC_BATCH, B_HEADS, B_DK, B_DV), 1.0),
        'state_pool': nrm((DEPTH, DEC_BATCH, POOL_MAX - 1, MIX_W), 1.0),
        'cache_win0': nrm((DEPTH, DEC_BATCH, lb[0], 2, D_HEADS, D_HD), 1.0),
        'cache_win1': nrm((DEPTH, DEC_BATCH, lb[1], 2, D_HEADS, D_HD), 1.0),
        'cache_win2': nrm((DEPTH, DEC_BATCH, lb[2], 2, D_HEADS, D_HD), 1.0),
        'p_prompt': nrm((DEPTH, BATCH, SEQ, PLE_DIM), 1.0),
        'p_sample': nrm((DEPTH, DEC_BATCH, DEC_SEQ, PLE_DIM), 1.0),
        'g_mix': gain((DEPTH, D_MODEL)),
        'w_in': nrm((DEPTH, D_MODEL, IN_W), D_MODEL ** -0.5),
        'conv_w': nrm((DEPTH, A_CONV, A_QKV), 0.5),
        'a_log': a_log,
        'dt_bias': dt_bias,
        'a_gain': gain((DEPTH, A_DV)),
        'c_w': nrm((DEPTH, C_GROUPS, C_GW, C_GW), C_GW ** -0.5),
        'c_scale': gain((DEPTH, MIX_W)),
        't5_bias': nrm((REL_BUCKETS, N_DG * D_HEADS), 0.5),
        'w_br': nrm((DEPTH, N_BRANCH, MIX_W, D_MODEL), MIX_W ** -0.5),
        'w_o': nrm((DEPTH, D_MODEL, D_MODEL), D_MODEL ** -0.5),
        'g_ffn': gain((DEPTH, D_MODEL)),
        'w_gate': nrm((DEPTH, D_MODEL, D_FF), D_MODEL ** -0.5),
        'w_up': nrm((DEPTH, D_MODEL, D_FF), D_MODEL ** -0.5),
        'w_down': nrm((DEPTH, D_FF, D_MODEL), D_FF ** -0.5),
        'g_ple': gain((DEPTH, D_MODEL)),
        'w_ple_gate': nrm((DEPTH, D_MODEL, D_MODEL), D_MODEL ** -0.5),
        'w_ple': nrm((DEPTH, PLE_DIM, D_MODEL), PLE_DIM ** -0.5),
        'g_final': gain((D_MODEL,)),
    }


def reference(x_prompt, x_sample, state_delta, state_delta_conv, state_ret, state_pool,
              cache_win0, cache_win1, cache_win2, p_prompt, p_sample,
              g_mix, w_in, conv_w, a_log, dt_bias, a_gain, c_w, c_scale, t5_bias,
              w_br, w_o, g_ffn, w_gate, w_up, w_down, g_ple, w_ple_gate, w_ple, g_final):
    layer_weights = (g_mix, w_in, conv_w, a_log, dt_bias, a_gain, c_w, c_scale, w_br, w_o,
                     g_ffn, w_gate, w_up, w_down, g_ple, w_ple_gate, w_ple)
    pos_p = jnp.arange(x_prompt.shape[1], dtype=jnp.int32)
    pos_s = PAST_LEN + jnp.arange(x_sample.shape[1], dtype=jnp.int32)
    y_prompt, (delta_p, conv_p, ret_p, pool_p, win0_p, win1_p, win2_p) = run_trunk(
        x_prompt, p_prompt, pos_p, None, True, layer_weights, t5_bias, g_final)
    sample_states = (state_delta, state_delta_conv, state_ret, state_pool, cache_win0, cache_win1, cache_win2)
    y_sample, (delta_s, conv_s, ret_s, pool_s, win0_s, win1_s, win2_s) = run_trunk(
        x_sample, p_sample, pos_s, sample_states, False, layer_weights, t5_bias, g_final)
    return (y_prompt, y_sample, delta_p, delta_s, conv_p, conv_s, ret_p, ret_s,
            pool_p, pool_s, win0_p, win0_s, win1_p, win1_s, win2_p, win2_s)
```

```python
import functools
import math

import numpy as np
import jax
import jax.numpy as jnp
from jax import lax
from jax.experimental import pallas as pl
from jax.experimental.pallas import tpu as pltpu

F32 = jnp.float32
BF16 = jnp.bfloat16
HIGHEST = lax.Precision.HIGHEST

D_MODEL = 1024
DEPTH = 4
PAST_LEN = 8192
MIX_W = D_MODEL // 2
N_BRANCH = 4
A_HEADS = 4
A_DK = MIX_W // A_HEADS
A_DV = MIX_W // A_HEADS
A_QKV = A_HEADS * (2 * A_DK + A_DV)
A_CONV = 4
A_CHUNK = 64
B_HEADS = 4
B_DV = MIX_W // B_HEADS
B_DK = B_DV // 2
B_CHUNK = 128
ROPE_BASE = 10000.0
C_GROUPS = 4
C_GW = MIX_W // C_GROUPS
POOL_WINDOWS = (2, 4, 8, 16)
POOL_MAX = 16
D_DILATIONS = (1, 4, 16)
N_DG = 3
D_SPAN = 128
D_HEADS = 4
D_HD = MIX_W // D_HEADS
REL_BUCKETS = 32
REL_MAX_DIST = D_SPAN * 16
D_FF = -(-8 * D_MODEL // (3 * 256)) * 256
PLE_DIM = 256
EPS = 1e-6

LANES = 128
SUBLANES = 8
VMEM_LIMIT_BYTES = 56 * 1024 * 1024

PROJ_W = 12800
COL_AQKV = 0
COL_AZ = 1536
COL_BQ = 2048
COL_BK = 2304
COL_BV = 2560
COL_BG = 3072
COL_CIN = 3584
COL_GATE = 4096
COL_DQKV = 8192
DBLK = D_HEADS * D_HD


def _cparams(sem):
    return pltpu.CompilerParams(dimension_semantics=sem, vmem_limit_bytes=VMEM_LIMIT_BYTES)


def _nt(a, b, precision=None):
    return lax.dot_general(a, b, (((1,), (1,)), ((), ())), precision=precision,
                           preferred_element_type=F32)


def _tn(a, b):
    return lax.dot_general(a, b, (((0,), (0,)), ((), ())), preferred_element_type=F32)


def _mm(a, b, precision=None):
    return jnp.dot(a, b, precision=precision, preferred_element_type=F32)


def _sigmoid(x):
    return 1.0 / (1.0 + jnp.exp(-x))


def _silu(x):
    return x * _sigmoid(x)


def _rms_rows(x, g):
    return x * lax.rsqrt(jnp.mean(x * x, axis=-1, keepdims=True) + EPS) * g


def _norm_matmul_kernel(x_ref, g_ref, w_ref, o_ref, u_ref):
    @pl.when(pl.program_id(1) == 0)
    def _():
        u_ref[...] = _rms_rows(x_ref[...], g_ref[...]).astype(BF16)

    o_ref[...] = _mm(u_ref[...], w_ref[...])


def _norm_matmul(x, g, w, tm, tn):
    n, d = x.shape
    wcols = w.shape[1]
    return pl.pallas_call(
        _norm_matmul_kernel,
        grid=(n // tm, wcols // tn),
        in_specs=[pl.BlockSpec((tm, d), lambda i, j: (i, 0)),
                  pl.BlockSpec((1, d), lambda i, j: (0, 0)),
                  pl.BlockSpec((d, tn), lambda i, j: (0, j))],
        out_specs=pl.BlockSpec((tm, tn), lambda i, j: (i, j)),
        out_shape=jax.ShapeDtypeStruct((n, wcols), F32),
        scratch_shapes=[pltpu.VMEM((tm, d), BF16)],
        compiler_params=_cparams(("parallel", "arbitrary")),
        name="norm_matmul",
    )(x, g.reshape(1, d), w)


def _delta_kernel(qkv_ref, z_ref, ab_ref, cprev_ref, sprev_ref, convw_ref, par_ref, gain_ref,
                  o_ref, snew_ref, ext_ref, s_ref, vn_ref, qs_ref, *, chunk, carry):
    j = pl.program_id(1)
    n_s, l_r, _ = qkv_ref.shape
    t_b = n_s * l_r
    n_c = t_b // chunk
    hdr = SUBLANES
    pre = A_CONV - 1

    @pl.when(j == 0)
    def _():
        ext_ref[:, 0:hdr, :] = jnp.zeros((n_s, hdr, A_QKV), F32)
        ext_ref[:, hdr - pre:hdr, :] = cprev_ref[...]
        if carry:
            s_ref[...] = sprev_ref[0]

    if carry:
        @pl.when(j > 0)
        def _():
            ext_ref[:, 0:hdr, :] = ext_ref[:, l_r:l_r + hdr, :]

    ext_ref[:, hdr:hdr + l_r, :] = qkv_ref[...]
    cw = convw_ref[...]

    def conv_cols(c0):
        acc = None
        for t in range(A_CONV):
            term = (ext_ref[:, hdr - pre + t:hdr - pre + t + l_r, c0:c0 + LANES]
                    * cw[t:t + 1, c0:c0 + LANES])
            acc = term if acc is None else acc + term
        return _silu(acc).reshape(t_b, LANES)

    ab = ab_ref[...].reshape(t_b, LANES)
    par = par_ref[...]
    beta_all = _sigmoid(ab)
    xs = ab + par[1:2, :]
    softplus = jnp.maximum(xs, 0.0) + jnp.log1p(jnp.exp(-jnp.abs(xs)))
    g_all = -jnp.exp(par[0:1, :]) * softplus

    sh = int(math.log2(chunk))
    ri = lax.broadcasted_iota(jnp.int32, (t_b, t_b), 0)
    ci = lax.broadcasted_iota(jnp.int32, (t_b, t_b), 1)
    same = lax.shift_right_logical(ri, sh) == lax.shift_right_logical(ci, sh)
    causal = jnp.logical_and(same, ci <= ri)
    strict = jnp.logical_and(same, ci < ri)
    gc_all = _mm(causal.astype(F32), g_all, precision=HIGHEST)
    sel = (lax.broadcasted_iota(jnp.int32, (SUBLANES, LANES), 0)
           == lax.broadcasted_iota(jnp.int32, (SUBLANES, LANES), 1)).astype(F32)
    gc_rows = _nt(sel, gc_all, precision=HIGHEST)

    gain = gain_ref[...]
    z = z_ref[...].reshape(t_b, MIX_W)
    row_id = lax.broadcasted_iota(jnp.int32, (t_b, 1), 0)

    outs = []
    for h in range(A_HEADS):
        gcc = gc_all[:, A_HEADS + h:A_HEADS + h + 1]
        gcr = gc_rows[A_HEADS + h:A_HEADS + h + 1, :]
        bcol = beta_all[:, h:h + 1]
        gam = jnp.where(causal, jnp.exp(jnp.where(causal, gcc - gcr, 0.0)), 0.0)
        q = conv_cols(h * A_DK)
        k = conv_cols(A_HEADS * A_DK + h * A_DK)
        v = conv_cols(2 * A_HEADS * A_DK + h * A_DV)
        q = q * lax.rsqrt(jnp.sum(q * q, axis=-1, keepdims=True) + EPS) * (A_DK ** -0.5)
        k = k * lax.rsqrt(jnp.sum(k * k, axis=-1, keepdims=True) + EPS)
        kb = k.astype(BF16)
        kk = _nt(kb, kb)
        qk = (_nt(q.astype(BF16), kb) * gam).astype(BF16)
        a = jnp.where(strict, bcol * kk * gam, 0.0)
        r = -a
        p = a
        for _ in range(sh - 1):
            pb = p.astype(BF16)
            p = _mm(pb, pb)
            r = r + p + _mm(p.astype(BF16), r.astype(BF16))
        ecol = jnp.exp(gcc)
        rhs = jnp.concatenate([v * bcol, k * (bcol * ecol)], axis=1)
        sol = rhs + _mm(r.astype(BF16), rhs.astype(BF16))
        u = sol[:, :A_DV]
        w = sol[:, A_DV:]
        qe = q * ecol
        for c in range(n_c):
            r0 = c * chunk
            if carry:
                s = s_ref[h]
            else:
                s = sprev_ref[c, h]
            wq = jnp.concatenate([w[r0:r0 + chunk], qe[r0:r0 + chunk]], axis=0).astype(BF16)
            t = _mm(wq, s.astype(BF16))
            vnew = u[r0:r0 + chunk] - t[:chunk]
            vn_ref[r0:r0 + chunk, :] = vnew
            qs_ref[r0:r0 + chunk, :] = t[chunk:]
            g_last = gcc[r0 + chunk - 1:r0 + chunk, :]
            kd = k * jnp.exp(g_last - gcc)
            if chunk >= 2 * SUBLANES:
                upd = _tn(kd[r0:r0 + chunk].astype(BF16), vnew.astype(BF16))
            else:
                in_c = jnp.logical_and(row_id >= r0, row_id < r0 + chunk)
                upd = _tn(jnp.where(in_c, kd, 0.0).astype(BF16),
                          jnp.where(in_c, vn_ref[...], 0.0).astype(BF16))
            s_new = s * jnp.exp(g_last) + upd
            if carry:
                s_ref[h] = s_new
            else:
                snew_ref[c, h] = s_new
        o = qs_ref[...] + _mm(qk, vn_ref[...].astype(BF16))
        o = o * lax.rsqrt(jnp.mean(o * o, axis=-1, keepdims=True) + EPS) * gain
        outs.append(o * _silu(z[:, h * A_DV:(h + 1) * A_DV]))
    o_ref[...] = jnp.concatenate(outs, axis=1).reshape(n_s, l_r, MIX_W)

    if carry:
        @pl.when(j == pl.num_programs(1) - 1)
        def _():
            snew_ref[0] = s_ref[...]


def _mixer_delta(proj3, ab3, conv_prev, s_prev, conv_w, par, gain, *, n_s, l_r, chunk, carry):
    bsz, t_len, _ = proj3.shape
    t_b = n_s * l_r
    grid = (bsz // n_s, t_len // l_r)
    kern = functools.partial(_delta_kernel, chunk=chunk, carry=carry)
    return pl.pallas_call(
        kern,
        grid=grid,
        in_specs=[
            pl.BlockSpec((n_s, l_r, A_QKV), lambda b, j: (b, j, COL_AQKV // A_QKV)),
            pl.BlockSpec((n_s, l_r, MIX_W), lambda b, j: (b, j, COL_AZ // MIX_W)),
            pl.BlockSpec((n_s, l_r, LANES), lambda b, j: (b, j, 0)),
            pl.BlockSpec((n_s, A_CONV - 1, A_QKV), lambda b, j: (b, 0, 0)),
            pl.BlockSpec((n_s, A_HEADS, A_DK, A_DV), lambda b, j: (b, 0, 0, 0)),
            pl.BlockSpec((A_CONV, A_QKV), lambda b, j: (0, 0)),
            pl.BlockSpec((SUBLANES, LANES), lambda b, j: (0, 0)),
            pl.BlockSpec((1, A_DV), lambda b, j: (0, 0)),
        ],
        out_specs=[
            pl.BlockSpec((n_s, l_r, MIX_W), lambda b, j: (b, j, 0)),
            pl.BlockSpec((n_s, A_HEADS, A_DK, A_DV), lambda b, j: (b, 0, 0, 0)),
        ],
        out_shape=[jax.ShapeDtypeStruct((bsz, t_len, MIX_W), F32),
                   jax.ShapeDtypeStruct((bsz, A_HEADS, A_DK, A_DV), F32)],
        scratch_shapes=[pltpu.VMEM((n_s, SUBLANES + l_r + SUBLANES, A_QKV), F32),
                        pltpu.VMEM((A_HEADS, A_DK, A_DV), F32),
                        pltpu.VMEM((t_b, A_DV), F32),
                        pltpu.VMEM((t_b, A_DV), F32)],
        compiler_params=_cparams(("parallel", "arbitrary")),
        name="mixer_delta",
    )(proj3, proj3, ab3, conv_prev, s_prev, conv_w, par, gain)


def _ret_log_gamma(h):
    return math.log1p(-(2.0 ** (-5.0 - h)))


def _ret_kernel(q_ref, k_ref, v_ref, g_ref, cos_ref, sin_ref, rprev_ref,
                o_ref, rnew_ref, r_ref, *, chunk, carry):
    j = pl.program_id(1)
    n_s, l_r, _ = q_ref.shape
    t_b = n_s * l_r
    n_c = t_b // chunk
    half = B_DK // 2

    if carry:
        @pl.when(j == 0)
        def _():
            r_ref[...] = rprev_ref[0]

    cos = cos_ref[...]
    sin = sin_ref[...]
    lane = lax.broadcasted_iota(jnp.int32, (1, LANES), 1)
    first_half = (lane % B_DK) < half
    lane_head = lane // B_DK

    def rope(x):
        swapped = jnp.where(first_half, pltpu.roll(x, LANES - half, axis=1),
                            pltpu.roll(x, half, axis=1))
        return x * cos + swapped * sin

    idx_i = lax.broadcasted_iota(jnp.int32, (chunk, chunk), 0)
    idx_j = lax.broadcasted_iota(jnp.int32, (chunk, chunk), 1)
    pos_col = lax.broadcasted_iota(jnp.int32, (chunk, 1), 0).astype(F32)
    row_head = lax.broadcasted_iota(jnp.int32, (LANES, 1), 0) // B_DK
    gate = g_ref[...].reshape(t_b, MIX_W)
    vv = v_ref[...].reshape(t_b, MIX_W)

    small = chunk < 2 * SUBLANES
    if small:
        sh = int(math.log2(chunk))
        bi = lax.broadcasted_iota(jnp.int32, (t_b, t_b), 0)
        bj = lax.broadcasted_iota(jnp.int32, (t_b, t_b), 1)
        bsame = lax.shift_right_logical(bi, sh) == lax.shift_right_logical(bj, sh)
        bcausal = jnp.logical_and(bsame, bj <= bi)
        bdiff = (bi - bj).astype(F32)
        row_id = lax.broadcasted_iota(jnp.int32, (t_b, 1), 0)
        pos_in = (row_id & (chunk - 1)).astype(F32)

    outs = [None] * B_HEADS
    for p in range(B_HEADS // 2):
        q2 = rope(q_ref[...].reshape(t_b, B_HEADS * B_DK)[:, p * LANES:(p + 1) * LANES])
        k2 = rope(k_ref[...].reshape(t_b, B_HEADS * B_DK)[:, p * LANES:(p + 1) * LANES]) * (B_DK ** -0.5)
        lgs = [_ret_log_gamma(2 * p + hh) for hh in range(2)]
        qh = [jnp.where(lane_head == hh, q2, 0.0) for hh in range(2)]
        kh = [jnp.where(lane_head == hh, k2, 0.0) for hh in range(2)]
        vh = [vv[:, (2 * p + hh) * B_DV:(2 * p + hh + 1) * B_DV] for hh in range(2)]
        row_scale = jnp.where(row_head == 0, math.exp(lgs[0] * chunk), math.exp(lgs[1] * chunk))
        if small:
            intra = []
            for hh in range(2):
                dec = jnp.where(bcausal, jnp.exp(lgs[hh] * jnp.where(bcausal, bdiff, 0.0)), 0.0)
                sc = _nt(qh[hh].astype(BF16), kh[hh].astype(BF16)) * dec
                intra.append(_mm(sc.astype(BF16), vh[hh].astype(BF16)))
            cross = [[], []]
            for c in range(n_c):
                r0 = c * chunk
                rp = r_ref[p] if carry else rprev_ref[c, p]
                qq = jnp.concatenate([qh[0][r0:r0 + chunk], qh[1][r0:r0 + chunk]], axis=0)
                t = _mm(qq.astype(BF16), rp.astype(BF16))
                cross[0].append(t[:chunk])
                cross[1].append(t[chunk:])
                in_c = jnp.logical_and(row_id >= r0, row_id < r0 + chunk)
                upd = None
                for hh in range(2):
                    kd = jnp.where(in_c, kh[hh] * jnp.exp(lgs[hh] * (chunk - 1.0 - pos_in)), 0.0)
                    term = _tn(kd.astype(BF16), vh[hh].astype(BF16))
                    upd = term if upd is None else upd + term
                rn = rp * row_scale + upd
                if carry:
                    r_ref[p] = rn
                else:
                    rnew_ref[c, p] = rn
            for hh in range(2):
                cr = jnp.concatenate(cross[hh], axis=0) * jnp.exp(lgs[hh] * (pos_in + 1.0))
                outs[2 * p + hh] = intra[hh] + cr
        else:
            pieces = [[], []]
            for c in range(n_c):
                r0 = c * chunk
                rp = r_ref[p] if carry else rprev_ref[c, p]
                rpb = rp.astype(BF16)
                upd = None
                for hh in range(2):
                    qc = qh[hh][r0:r0 + chunk].astype(BF16)
                    kc = kh[hh][r0:r0 + chunk]
                    vc = vh[hh][r0:r0 + chunk].astype(BF16)
                    causal = idx_j <= idx_i
                    dec = jnp.where(causal, jnp.exp(lgs[hh] * jnp.where(causal, (idx_i - idx_j).astype(F32), 0.0)), 0.0)
                    sc = _nt(qc, kc.astype(BF16)) * dec
                    oc = _mm(sc.astype(BF16), vc) + _mm(qc, rpb) * jnp.exp(lgs[hh] * (pos_col + 1.0))
                    pieces[hh].append(oc)
                    kd = kc * jnp.exp(lgs[hh] * (chunk - 1.0 - pos_col))
                    term = _tn(kd.astype(BF16), vc)
                    upd = term if upd is None else upd + term
                rn = rp * row_scale + upd
                if carry:
                    r_ref[p] = rn
                else:
                    rnew_ref[c, p] = rn
            for hh in range(2):
                outs[2 * p + hh] = jnp.concatenate(pieces[hh], axis=0) if n_c > 1 else pieces[hh][0]

    res = []
    for h in range(B_HEADS):
        o = outs[h]
        o = o * lax.rsqrt(jnp.mean(o * o, axis=-1, keepdims=True) + EPS)
        res.append(o * _silu(gate[:, h * B_DV:(h + 1) * B_DV]))
    o_ref[...] = jnp.concatenate(res, axis=1).reshape(n_s, l_r, MIX_W)

    if carry:
        @pl.when(j == pl.num_programs(1) - 1)
        def _():
            rnew_ref[0] = r_ref[...]


def _mixer_ret(proj3, cos_t, sin_t, r_prev, *, n_s, l_r, chunk, carry):
    bsz, t_len, _ = proj3.shape
    t_b = n_s * l_r
    grid = (bsz // n_s, t_len // l_r)
    n_tab = cos_t.shape[0] // t_b
    kern = functools.partial(_ret_kernel, chunk=chunk, carry=carry)
    qw = B_HEADS * B_DK
    return pl.pallas_call(
        kern,
        grid=grid,
        in_specs=[
            pl.BlockSpec((n_s, l_r, qw), lambda b, j: (b, j, COL_BQ // qw)),
            pl.BlockSpec((n_s, l_r, qw), lambda b, j: (b, j, COL_BK // qw)),
            pl.BlockSpec((n_s, l_r, MIX_W), lambda b, j: (b, j, COL_BV // MIX_W)),
            pl.BlockSpec((n_s, l_r, MIX_W), lambda b, j: (b, j, COL_BG // MIX_W)),
            pl.BlockSpec((t_b, LANES), lambda b, j: (j % n_tab, 0)),
            pl.BlockSpec((t_b, LANES), lambda b, j: (j % n_tab, 0)),
            pl.BlockSpec((n_s, B_HEADS // 2, LANES, B_DV), lambda b, j: (b, 0, 0, 0)),
        ],
        out_specs=[
            pl.BlockSpec((n_s, l_r, MIX_W), lambda b, j: (b, j, 0)),
            pl.BlockSpec((n_s, B_HEADS // 2, LANES, B_DV), lambda b, j: (b, 0, 0, 0)),
        ],
        out_shape=[jax.ShapeDtypeStruct((bsz, t_len, MIX_W), F32),
                   jax.ShapeDtypeStruct((bsz, B_HEADS // 2, LANES, B_DV), F32)],
        scratch_shapes=[pltpu.VMEM((B_HEADS // 2, LANES, B_DV), F32)],
        compiler_params=_cparams(("parallel", "arbitrary")),
        name="mixer_ret",
    )(proj3, proj3, proj3, proj3, cos_t, sin_t, r_prev)


def _rope_tables(pos, reps):
    half = B_DK // 2
    inv = ROPE_BASE ** (-jnp.arange(half, dtype=F32) / half)
    ang = pos.astype(F32)[:, None] * inv[None, :]
    cos = jnp.cos(ang)
    sin = jnp.sin(ang)
    cos_t = jnp.concatenate([cos, cos, cos, cos], axis=1)
    sin_t = jnp.concatenate([-sin, sin, -sin, sin], axis=1)
    return jnp.tile(cos_t, (reps, 1)), jnp.tile(sin_t, (reps, 1))


def _pool_kernel(x_ref, prev_ref, cw_ref, cs_ref, o_ref, ext_ref, *, pos0):
    j = pl.program_id(1)
    n_s, l_r, _ = x_ref.shape
    t_b = n_s * l_r
    hdr = POOL_MAX

    @pl.when(j == 0)
    def _():
        ext_ref[:, 0:hdr, :] = jnp.zeros((n_s, hdr, MIX_W), F32)
        ext_ref[:, 1:hdr, :] = prev_ref[...]

    @pl.when(j > 0)
    def _():
        ext_ref[:, 0:hdr, :] = ext_ref[:, l_r:l_r + hdr, :]

    ext_ref[:, hdr:hdr + l_r, :] = x_ref[...]
    pos = pos0 + j * l_r + lax.broadcasted_iota(jnp.int32, (1, l_r, 1), 1)
    outs = []
    for gi, w in enumerate(POOL_WINDOWS):
        c0 = gi * C_GW
        acc = None
        for t in range(w):
            term = ext_ref[:, hdr - t:hdr - t + l_r, c0:c0 + C_GW]
            acc = term if acc is None else acc + term
        cnt = jnp.minimum(pos + 1, w).astype(F32)
        y = acc / cnt - ext_ref[:, hdr:hdr + l_r, c0:c0 + C_GW]
        y = _mm(y.reshape(t_b, C_GW).astype(BF16), cw_ref[gi])
        outs.append(y)
    o_ref[...] = (jnp.concatenate(outs, axis=1) * cs_ref[...]).reshape(n_s, l_r, MIX_W)


def _mixer_pool(proj3, prev, c_w, c_scale, *, n_s, l_r, pos0):
    bsz, t_len, _ = proj3.shape
    grid = (bsz // n_s, t_len // l_r)
    kern = functools.partial(_pool_kernel, pos0=pos0)
    return pl.pallas_call(
        kern,
        grid=grid,
        in_specs=[
            pl.BlockSpec((n_s, l_r, MIX_W), lambda b, j: (b, j, COL_CIN // MIX_W)),
            pl.BlockSpec((n_s, POOL_MAX - 1, MIX_W), lambda b, j: (b, 0, 0)),
            pl.BlockSpec((C_GROUPS, C_GW, C_GW), lambda b, j: (0, 0, 0)),
            pl.BlockSpec((1, MIX_W), lambda b, j: (0, 0)),
        ],
        out_specs=pl.BlockSpec((n_s, l_r, MIX_W), lambda b, j: (b, j, 0)),
        out_shape=jax.ShapeDtypeStruct((bsz, t_len, MIX_W), F32),
        scratch_shapes=[pltpu.VMEM((n_s, POOL_MAX + l_r + SUBLANES, MIX_W), F32)],
        compiler_params=_cparams(("parallel", "arbitrary")),
        name="mixer_pool",
    )(proj3, prev, c_w, c_scale)


def _lookup_kernel(oh_ref, tab_ref, o_ref):
    o_ref[...] = _mm(oh_ref[...], tab_ref[...], precision=HIGHEST)


def _bias_lookup(idx, table):
    n_valid = idx.shape[0]
    tr = 4096
    n_rows = -(-n_valid // tr) * tr
    idx = np.concatenate([idx, np.zeros((n_rows - n_valid,), idx.dtype)])
    onehot = jnp.asarray(idx[:, None] == np.arange(REL_BUCKETS)[None, :], dtype=F32)
    tab = jnp.pad(table, ((0, 0), (0, LANES - table.shape[1])))
    out = pl.pallas_call(
        _lookup_kernel,
        grid=(n_rows // tr,),
        in_specs=[pl.BlockSpec((tr, REL_BUCKETS), lambda i: (i, 0)),
                  pl.BlockSpec((REL_BUCKETS, LANES), lambda i: (0, 0))],
        out_specs=pl.BlockSpec((tr, LANES), lambda i: (i, 0)),
        out_shape=jax.ShapeDtypeStruct((n_rows, LANES), F32),
        compiler_params=_cparams(("parallel",)),
        name="bias_lookup",
    )(onehot, tab)
    return out[:n_valid, :table.shape[1]]


def _t5_bucket_np(dist):
    exact = REL_BUCKETS // 2
    n = np.maximum(dist, 0)
    nf = np.maximum(n, 1).astype(np.float32)
    large = exact + (np.log(nf / np.float32(exact)) / np.float32(math.log(REL_MAX_DIST / exact))
                     * np.float32(REL_BUCKETS - exact)).astype(np.int32)
    large = np.minimum(large, REL_BUCKETS - 1)
    return np.where(n < exact, n, large).astype(np.int32)


def _band_kernel(q_ref, kc_ref, vc_ref, kp_ref, vp_ref, bias_ref, o_ref, lse_ref):
    n = pl.program_id(2)
    q_rows = q_ref.shape[0]
    n_q = q_rows // D_SPAN
    ii = lax.broadcasted_iota(jnp.int32, (D_SPAN, D_SPAN), 0)
    jj = lax.broadcasted_iota(jnp.int32, (D_SPAN, D_SPAN), 1)
    mask_cur = jj <= ii
    mask_prev_tri = jj >= ii
    has_prev = n > 0
    lane4 = lax.broadcasted_iota(jnp.int32, (1, D_HEADS), 1)
    scale = D_HD ** -0.5
    for qi in range(n_q):
        r0 = qi * D_SPAN
        lse_rows = jnp.zeros((D_SPAN, D_HEADS), F32)
        o_heads = []
        for h in range(D_HEADS):
            cs = slice(h * D_HD, (h + 1) * D_HD)
            qh = q_ref[r0:r0 + D_SPAN, cs].astype(BF16)
            k_cur = kc_ref[r0:r0 + D_SPAN, cs].astype(BF16)
            v_cur = vc_ref[r0:r0 + D_SPAN, cs].astype(BF16)
            if qi == 0:
                k_prev = kp_ref[:, cs].astype(BF16)
                v_prev = vp_ref[:, cs].astype(BF16)
                mask_prev = jnp.logical_and(mask_prev_tri, has_prev)
            else:
                k_prev = kc_ref[r0 - D_SPAN:r0, cs].astype(BF16)
                v_prev = vc_ref[r0 - D_SPAN:r0, cs].astype(BF16)
                mask_prev = mask_prev_tri
            s_p = _nt(qh, k_prev) * scale + bias_ref[h, :, 0:D_SPAN]
            s_c = _nt(qh, k_cur) * scale + bias_ref[h, :, D_SPAN:2 * D_SPAN]
            s_p = jnp.where(mask_prev, s_p, -jnp.inf)
            s_c = jnp.where(mask_cur, s_c, -jnp.inf)
            m = jnp.maximum(jnp.max(s_p, axis=-1, keepdims=True), jnp.max(s_c, axis=-1, keepdims=True))
            p_p = jnp.exp(s_p - m)
            p_c = jnp.exp(s_c - m)
            l = jnp.sum(p_p, axis=-1, keepdims=True) + jnp.sum(p_c, axis=-1, keepdims=True)
            acc = _mm(p_p.astype(BF16), v_prev) + _mm(p_c.astype(BF16), v_cur)
            o_heads.append(acc / l)
            lse_rows = jnp.where(lane4 == h, m + jnp.log(l), lse_rows)
        o_ref[r0:r0 + D_SPAN, :] = jnp.concatenate(o_heads, axis=1)
        lse_ref[r0:r0 + D_SPAN, :] = lse_rows


def _dilated_band(proj3, bias_band, gi, dil):
    bsz, s_len, _ = proj3.shape
    l_sub = s_len // dil
    q_rows = min(4 * D_SPAN, l_sub)
    n_blk = l_sub // q_rows
    sub = q_rows // D_SPAN
    xv = proj3.reshape(bsz, l_sub, dil * PROJ_W)
    ncol = PROJ_W // DBLK
    base = COL_DQKV // DBLK
    cq, ck, cv = base + gi, base + N_DG + gi, base + 2 * N_DG + gi
    o, lse = pl.pallas_call(
        _band_kernel,
        grid=(bsz, dil, n_blk),
        in_specs=[
            pl.BlockSpec((None, q_rows, DBLK), lambda b, r, n: (b, n, r * ncol + cq)),
            pl.BlockSpec((None, q_rows, DBLK), lambda b, r, n: (b, n, r * ncol + ck)),
            pl.BlockSpec((None, q_rows, DBLK), lambda b, r, n: (b, n, r * ncol + cv)),
            pl.BlockSpec((None, D_SPAN, DBLK), lambda b, r, n: (b, jnp.maximum(n * sub - 1, 0), r * ncol + ck)),
            pl.BlockSpec((None, D_SPAN, DBLK), lambda b, r, n: (b, jnp.maximum(n * sub - 1, 0), r * ncol + cv)),
            pl.BlockSpec((D_HEADS, D_SPAN, 2 * D_SPAN), lambda b, r, n: (0, 0, 0)),
        ],
        out_specs=[
            pl.BlockSpec((None, q_rows, DBLK), lambda b, r, n: (b, n, r)),
            pl.BlockSpec((None, None, q_rows, D_HEADS), lambda b, r, n: (b, r, n, 0)),
        ],
        out_shape=[jax.ShapeDtypeStruct((bsz, l_sub, dil * DBLK), F32),
                   jax.ShapeDtypeStruct((bsz, dil, l_sub, D_HEADS), F32)],
        compiler_params=_cparams(("parallel", "parallel", "arbitrary")),
        name="dilated_band",
    )(xv, xv, xv, xv, xv, bias_band)
    o = o.reshape(bsz, s_len, DBLK)
    lse = lse.transpose(0, 2, 1, 3).reshape(bsz, s_len, D_HEADS)
    return o, lse


def _step_attn_kernel(q_ref, kn_ref, vn_ref, cache_ref, bias_c_ref, mask_c_ref, bias_n_ref,
                      mask_n_ref, buf_ref, o_ref, lse_ref, newrows_ref, pad_ref, m_ref, l_ref, acc_ref):
    del buf_ref
    c = pl.program_id(1)
    t_new = q_ref.shape[0]
    n_hq = D_HEADS * t_new
    scale = D_HD ** -0.5
    rowh = lax.broadcasted_iota(jnp.int32, (n_hq, 1), 0) // t_new
    laneh = lax.broadcasted_iota(jnp.int32, (1, DBLK), 1) // D_HD
    q = q_ref[...]
    qrows = jnp.where(rowh == laneh, jnp.concatenate([q] * D_HEADS, axis=0), 0.0).astype(BF16)

    @pl.when(c == 0)
    def _():
        pad_ref[...] = jnp.zeros(pad_ref.shape, F32)
        pad_ref[0:t_new, 0:DBLK] = kn_ref[...]
        pad_ref[0:t_new, DBLK:2 * DBLK] = vn_ref[...]
        newrows_ref[:, 0:DBLK] = kn_ref[...]
        newrows_ref[:, DBLK:2 * DBLK] = vn_ref[...]
        s = _nt(qrows, pad_ref[:, 0:DBLK].astype(BF16)) * scale + bias_n_ref[...] + mask_n_ref[...]
        m = jnp.max(s, axis=-1, keepdims=True)
        p = jnp.exp(s - m)
        m_ref[...] = m
        l_ref[...] = jnp.sum(p, axis=-1, keepdims=True)
        acc_ref[...] = _mm(p.astype(BF16), pad_ref[:, DBLK:2 * DBLK].astype(BF16))

    kc = cache_ref[:, 0:DBLK].astype(BF16)
    vc = cache_ref[:, DBLK:2 * DBLK].astype(BF16)
    s = _nt(qrows, kc) * scale + bias_c_ref[...] + mask_c_ref[...]
    m_old = m_ref[...]
    m_new = jnp.maximum(m_old, jnp.max(s, axis=-1, keepdims=True))
    alpha = jnp.exp(m_old - m_new)
    p = jnp.exp(s - m_new)
    l_ref[...] = alpha * l_ref[...] + jnp.sum(p, axis=-1, keepdims=True)
    acc_ref[...] = alpha * acc_ref[...] + _mm(p.astype(BF16), vc)
    m_ref[...] = m_new

    @pl.when(c == pl.num_programs(1) - 1)
    def _():
        l = l_ref[...]
        o_all = acc_ref[...] / l
        lse_all = m_ref[...] + jnp.log(l)
        lane4 = lax.broadcasted_iota(jnp.int32, (1, D_HEADS), 1)
        o = jnp.zeros((t_new, DBLK), F32)
        lse = jnp.zeros((t_new, D_HEADS), F32)
        for h in range(D_HEADS):
            o = jnp.where(laneh == h, o_all[h * t_new:(h + 1) * t_new, :], o)
            lse = jnp.where(lane4 == h, lse_all[h * t_new:(h + 1) * t_new, :], lse)
        o_ref[...] = o
        lse_ref[...] = lse


def _dilated_step(proj3, cache4, out_buf, layer, gi, bias_c, mask_c, bias_n, mask_n):
    bsz, t_new, _ = proj3.shape
    l_buf = cache4.shape[2]
    rows = min(l_buf, 1024)
    n_ch = l_buf // rows
    n_hq = D_HEADS * t_new
    base = COL_DQKV // DBLK
    cq, ck, cv = base + gi, base + N_DG + gi, base + 2 * N_DG + gi
    last_blk = l_buf // t_new - 1
    return pl.pallas_call(
        _step_attn_kernel,
        grid=(bsz, n_ch),
        in_specs=[
            pl.BlockSpec((None, t_new, DBLK), lambda b, c: (b, 0, cq)),
            pl.BlockSpec((None, t_new, DBLK), lambda b, c: (b, 0, ck)),
            pl.BlockSpec((None, t_new, DBLK), lambda b, c: (b, 0, cv)),
            pl.BlockSpec((None, None, rows, 2 * DBLK), lambda b, c: (layer, b, c, 0)),
            pl.BlockSpec((n_hq, rows), lambda b, c: (0, c)),
            pl.BlockSpec((n_hq, rows), lambda b, c: (0, c)),
            pl.BlockSpec((n_hq, LANES), lambda b, c: (0, 0)),
            pl.BlockSpec((n_hq, LANES), lambda b, c: (0, 0)),
            pl.BlockSpec(memory_space=pl.ANY),
        ],
        out_specs=[
            pl.BlockSpec((None, t_new, DBLK), lambda b, c: (b, 0, 0)),
            pl.BlockSpec((None, t_new, D_HEADS), lambda b, c: (b, 0, 0)),
            pl.BlockSpec((None, None, t_new, 2 * DBLK), lambda b, c: (layer, b, last_blk, 0)),
        ],
        out_shape=[jax.ShapeDtypeStruct((bsz, t_new, DBLK), F32),
                   jax.ShapeDtypeStruct((bsz, t_new, D_HEADS), F32),
                   jax.ShapeDtypeStruct(out_buf.shape, F32)],
        scratch_shapes=[pltpu.VMEM((LANES, 2 * DBLK), F32),
                        pltpu.VMEM((n_hq, 1), F32),
                        pltpu.VMEM((n_hq, 1), F32),
                        pltpu.VMEM((n_hq, DBLK), F32)],
        input_output_aliases={8: 2},
        compiler_params=_cparams(("parallel", "arbitrary")),
        name="dilated_step",
    )(proj3, proj3, proj3, cache4, bias_c, mask_c, bias_n, mask_n, out_buf)


def _shift_kernel(x_ref, nxt_ref, o_ref, *, t_new):
    c = pl.program_id(1)
    rows = x_ref.shape[1]
    o_ref[:, 0:rows - t_new, :] = x_ref[:, t_new:rows, :]
    tail = jnp.where(c == pl.num_programs(1) - 1, 0.0, nxt_ref[...])
    o_ref[:, rows - t_new:rows, :] = tail


def _shift_cache(cache3, t_new):
    nb_total, l_buf, width = cache3.shape
    rows = min(l_buf, 1024)
    n_ch = l_buf // rows
    nb = max(1, 1024 // l_buf)
    per = rows // t_new
    last = l_buf // t_new - 1
    return pl.pallas_call(
        functools.partial(_shift_kernel, t_new=t_new),
        grid=(nb_total // nb, n_ch),
        in_specs=[pl.BlockSpec((nb, rows, width), lambda i, c: (i, c, 0)),
                  pl.BlockSpec((nb, t_new, width), lambda i, c: (i, jnp.minimum((c + 1) * per, last), 0))],
        out_specs=pl.BlockSpec((nb, rows, width), lambda i, c: (i, c, 0)),
        out_shape=jax.ShapeDtypeStruct(cache3.shape, F32),
        compiler_params=_cparams(("parallel", "arbitrary")),
        name="shift_cache",
    )(cache3, cache3)


def _merge_kernel(gate_ref, ya_ref, yb_ref, yc_ref, o0_ref, o1_ref, o2_ref, l0_ref, l1_ref, l2_ref,
                  wbr_ref, wo_ref, h_ref, out_ref):
    l0, l1, l2 = l0_ref[...], l1_ref[...], l2_ref[...]
    m = jnp.maximum(jnp.maximum(l0, l1), l2)
    e0, e1, e2 = jnp.exp(l0 - m), jnp.exp(l1 - m), jnp.exp(l2 - m)
    inv = 1.0 / (e0 + e1 + e2)
    w0, w1, w2 = e0 * inv, e1 * inv, e2 * inv
    yd = []
    for h in range(D_HEADS):
        cs = slice(h * D_HD, (h + 1) * D_HD)
        yd.append(w0[:, h:h + 1] * o0_ref[:, cs] + w1[:, h:h + 1] * o1_ref[:, cs]
                  + w2[:, h:h + 1] * o2_ref[:, cs])
    branches = [ya_ref[...], yb_ref[...], yc_ref[...], jnp.concatenate(yd, axis=1)]
    merged = None
    for nbr in range(N_BRANCH):
        up = _mm(branches[nbr].astype(BF16), wbr_ref[nbr])
        term = _sigmoid(gate_ref[:, nbr * D_MODEL:(nbr + 1) * D_MODEL]) * up
        merged = term if merged is None else merged + term
    out_ref[...] = h_ref[...] + _mm(merged.astype(BF16), wo_ref[...])


def _merge(proj2, ya, yb, yc, od, lse, w_br, w_o, h2, tm):
    n = h2.shape[0]
    row = lambda width: pl.BlockSpec((tm, width), lambda i: (i, 0))
    return pl.pallas_call(
        _merge_kernel,
        grid=(n // tm,),
        in_specs=[pl.BlockSpec((tm, N_BRANCH * D_MODEL), lambda i: (i, COL_GATE // (N_BRANCH * D_MODEL))),
                  row(MIX_W), row(MIX_W), row(MIX_W), row(MIX_W), row(MIX_W), row(MIX_W),
                  row(D_HEADS), row(D_HEADS), row(D_HEADS),
                  pl.BlockSpec((N_BRANCH, MIX_W, D_MODEL), lambda i: (0, 0, 0)),
                  pl.BlockSpec((D_MODEL, D_MODEL), lambda i: (0, 0)),
                  row(D_MODEL)],
        out_specs=row(D_MODEL),
        out_shape=jax.ShapeDtypeStruct((n, D_MODEL), F32),
        compiler_params=_cparams(("parallel",)),
        name="branch_merge",
    )(proj2, ya, yb, yc, od[0], od[1], od[2], lse[0], lse[1], lse[2], w_br, w_o, h2)


def _ffn_kernel(h_ref, gffn_ref, wg_ref, wu_ref, wd_ref, p_ref, gple_ref, wpg_ref, wple_ref, gfin_ref,
                out_ref, f_ref, acc_ref, *, final):
    k = pl.program_id(1)

    @pl.when(k == 0)
    def _():
        f_ref[...] = _rms_rows(h_ref[...], gffn_ref[...]).astype(BF16)
        acc_ref[...] = jnp.zeros(acc_ref.shape, F32)

    f = f_ref[...]
    a = _mm(f, wg_ref[...])
    b = _mm(f, wu_ref[...])
    acc_ref[...] += _mm((_silu(a) * b).astype(BF16), wd_ref[...])

    @pl.when(k == pl.num_programs(1) - 1)
    def _():
        h2 = h_ref[...] + acc_ref[...]
        e = _rms_rows(h2, gple_ref[...]).astype(BF16)
        gate = _sigmoid(_mm(e, wpg_ref[...]))
        h3 = h2 + gate * _mm(p_ref[...].astype(BF16), wple_ref[...])
        if final:
            out_ref[...] = _rms_rows(h3, gfin_ref[...])
        else:
            out_ref[...] = h3


def _ffn(h2, g_ffn, w_gate, w_up, w_down, p2, g_ple, w_pg, w_ple, g_fin, tm, tf, final):
    n = h2.shape[0]
    vec = lambda: pl.BlockSpec((1, D_MODEL), lambda i, k: (0, 0))
    return pl.pallas_call(
        functools.partial(_ffn_kernel, final=final),
        grid=(n // tm, D_FF // tf),
        in_specs=[pl.BlockSpec((tm, D_MODEL), lambda i, k: (i, 0)),
                  vec(),
                  pl.BlockSpec((D_MODEL, tf), lambda i, k: (0, k)),
                  pl.BlockSpec((D_MODEL, tf), lambda i, k: (0, k)),
                  pl.BlockSpec((tf, D_MODEL), lambda i, k: (k, 0)),
                  pl.BlockSpec((tm, PLE_DIM), lambda i, k: (i, 0)),
                  vec(),
                  pl.BlockSpec((D_MODEL, D_MODEL), lambda i, k: (0, 0)),
                  pl.BlockSpec((PLE_DIM, D_MODEL), lambda i, k: (0, 0)),
                  vec()],
        out_specs=pl.BlockSpec((tm, D_MODEL), lambda i, k: (i, 0)),
        out_shape=jax.ShapeDtypeStruct((n, D_MODEL), F32),
        scratch_shapes=[pltpu.VMEM((tm, D_MODEL), BF16), pltpu.VMEM((tm, D_MODEL), F32)],
        compiler_params=_cparams(("parallel", "arbitrary")),
        name="ffn_ple",
    )(h2, g_ffn.reshape(1, -1), w_gate, w_up, w_down, p2, g_ple.reshape(1, -1), w_pg, w_ple,
      g_fin.reshape(1, -1))


def _prep_weights(w_in, w_br, w_o, w_gate, w_up, w_down, w_ple_gate, w_ple, a_log, dt_bias):
    o_ab = A_QKV + MIX_W
    o_bq = o_ab + 2 * A_HEADS
    o_d = o_bq + 2 * B_HEADS * B_DK + 3 * MIX_W
    o_g = o_d + 3 * N_DG * MIX_W
    w_main = jnp.concatenate([w_in[:, :, :o_ab], w_in[:, :, o_bq:o_d], w_in[:, :, o_g:],
                              w_in[:, :, o_d:o_g]], axis=-1).astype(BF16)
    w_ab = jnp.pad(w_in[:, :, o_ab:o_bq], ((0, 0), (0, 0), (0, LANES - 2 * A_HEADS))).astype(BF16)
    par = jnp.zeros((a_log.shape[0], SUBLANES, LANES), F32)
    par = par.at[:, 0, A_HEADS:2 * A_HEADS].set(a_log)
    par = par.at[:, 1, A_HEADS:2 * A_HEADS].set(dt_bias)
    return dict(w_main=w_main, w_ab=w_ab, par=par, w_br=w_br.astype(BF16), w_o=w_o.astype(BF16),
                w_gate=w_gate.astype(BF16), w_up=w_up.astype(BF16), w_down=w_down.astype(BF16),
                w_pg=w_ple_gate.astype(BF16), w_ple=w_ple.astype(BF16))


def _band_bias(t5_bias):
    i = np.arange(D_SPAN)[:, None]
    j = np.arange(2 * D_SPAN)[None, :]
    rel = i + D_SPAN - j
    out = []
    for gi, dil in enumerate(D_DILATIONS):
        idx = _t5_bucket_np(rel * dil).reshape(-1)
        tab = _bias_lookup(idx, t5_bias[:, gi * D_HEADS:(gi + 1) * D_HEADS])
        out.append(tab.reshape(D_SPAN, 2 * D_SPAN, D_HEADS).transpose(2, 0, 1))
    return out


def _step_bias(t5_bias, l_buf, t_new, gi, dil):
    t = np.arange(t_new)[:, None]
    j = np.arange(l_buf + LANES)[None, :]
    dist = l_buf + t - j
    valid = (dist >= 0) & (dist % dil == 0) & (dist <= D_SPAN * dil) & (j < l_buf + t_new)
    idx = _t5_bucket_np(np.where(valid, dist, 0)).reshape(-1)
    tab = _bias_lookup(idx, t5_bias[:, gi * D_HEADS:(gi + 1) * D_HEADS])
    tab = tab.reshape(t_new, l_buf + LANES, D_HEADS).transpose(2, 0, 1).reshape(D_HEADS * t_new, l_buf + LANES)
    vmask = np.tile(valid[None], (D_HEADS, 1, 1)).reshape(D_HEADS * t_new, l_buf + LANES)
    tab = jnp.where(vmask, tab, 0.0)
    mask = jnp.asarray(np.where(vmask, 0.0, -np.inf), dtype=F32)
    return tab[:, :l_buf], mask[:, :l_buf], tab[:, l_buf:], mask[:, l_buf:]


def _layer_common(h2, p2, bsz, t_len, wts, lw, i, mix_fn, tm, final, g_final):
    n = bsz * t_len
    proj2 = _norm_matmul(h2, lw['g_mix'][i], wts['w_main'][i], tm, 1280)
    ab2 = _norm_matmul(h2, lw['g_mix'][i], wts['w_ab'][i], tm, LANES)
    proj3 = proj2.reshape(bsz, t_len, PROJ_W)
    ab3 = ab2.reshape(bsz, t_len, LANES)
    ya, yb, yc, od, lse, states = mix_fn(proj3, ab3)
    flat = lambda x: x.reshape(n, x.shape[-1])
    tm2 = min(tm, 256)
    h2 = _merge(proj2, flat(ya), flat(yb), flat(yc), [flat(x) for x in od], [flat(x) for x in lse],
                wts['w_br'][i], wts['w_o'][i], h2, tm2)
    tm3 = min(tm, 512)
    h2 = _ffn(h2, lw['g_ffn'][i], wts['w_gate'][i], wts['w_up'][i], wts['w_down'][i], p2,
              lw['g_ple'][i], wts['w_pg'][i], wts['w_ple'][i], g_final, tm3, D_FF // 2, final)
    return h2, proj3, states


def _run_prompt(x, p, wts, lw, t5_bias, g_final, depth):
    bsz, t_len, _ = x.shape
    n = bsz * t_len
    h2 = x.reshape(n, D_MODEL)
    pos = jnp.arange(t_len, dtype=jnp.int32)
    cos_t, sin_t = _rope_tables(pos, 1)
    band_bias = _band_bias(t5_bias)
    zero_s = jnp.zeros((bsz, A_HEADS, A_DK, A_DV), F32)
    zero_c = jnp.zeros((bsz, A_CONV - 1, A_QKV), F32)
    zero_r = jnp.zeros((bsz, B_HEADS // 2, LANES, B_DV), F32)
    zero_p = jnp.zeros((bsz, POOL_MAX - 1, MIX_W), F32)
    per_layer = []
    for i in range(depth):
        def mix_fn(proj3, ab3, i=i):
            ya, s_new = _mixer_delta(proj3, ab3, zero_c, zero_s, lw['conv_w'][i], wts['par'][i],
                                     lw['a_gain'][i].reshape(1, A_DV), n_s=1, l_r=256, chunk=A_CHUNK, carry=True)
            yb, r_new = _mixer_ret(proj3, cos_t, sin_t, zero_r, n_s=1, l_r=256, chunk=B_CHUNK, carry=True)
            yc = _mixer_pool(proj3, zero_p, lw['c_w'][i].astype(BF16), lw['c_scale'][i].reshape(1, MIX_W),
                             n_s=1, l_r=256, pos0=0)
            od, lse = [], []
            for gi, dil in enumerate(D_DILATIONS):
                o, l = _dilated_band(proj3, band_bias[gi], gi, dil)
                od.append(o)
                lse.append(l)
            return ya, yb, yc, od, lse, (s_new, r_new)

        h2, proj3, (s_new, r_new) = _layer_common(h2, p[i].reshape(n, PLE_DIM), bsz, t_len, wts, lw, i,
                                                  mix_fn, 1024, i == depth - 1, g_final)
        wins = []
        for gi, dil in enumerate(D_DILATIONS):
            keep = min(D_SPAN * dil, t_len)
            kcol = COL_DQKV + (N_DG + gi) * DBLK
            vcol = COL_DQKV + (2 * N_DG + gi) * DBLK
            kv = jnp.stack([proj3[:, t_len - keep:, kcol:kcol + DBLK],
                            proj3[:, t_len - keep:, vcol:vcol + DBLK]], axis=2)
            wins.append(kv.reshape(bsz, keep, 2, D_HEADS, D_HD))
        per_layer.append((s_new,
                          proj3[:, t_len - (A_CONV - 1):, COL_AQKV:COL_AQKV + A_QKV],
                          r_new.reshape(bsz, B_HEADS, B_DK, B_DV),
                          proj3[:, t_len - (POOL_MAX - 1):, COL_CIN:COL_CIN + MIX_W],
                          wins[0], wins[1], wins[2]))
    stacked = tuple(jnp.stack([ns[j] for ns in per_layer]) for j in range(7))
    return h2.reshape(bsz, t_len, D_MODEL), stacked


def _run_sample(x, p, states, wts, lw, t5_bias, g_final, depth):
    s_delta, s_conv, s_ret, s_pool, caches = states
    bsz, t_len, _ = x.shape
    n = bsz * t_len
    n_s = 16
    h2 = x.reshape(n, D_MODEL)
    pos = PAST_LEN + jnp.arange(t_len, dtype=jnp.int32)
    cos_t, sin_t = _rope_tables(pos, n_s)
    cache4, out_bufs, step_bias = [], [], []
    for gi, dil in enumerate(D_DILATIONS):
        c = caches[gi]
        l_buf = c.shape[2]
        c4 = c.reshape(c.shape[0], bsz, l_buf, 2 * DBLK)
        cache4.append(c4)
        shifted = _shift_cache(c4.reshape(c.shape[0] * bsz, l_buf, 2 * DBLK), t_len)
        out_bufs.append(shifted.reshape(c4.shape))
        step_bias.append(_step_bias(t5_bias, l_buf, t_len, gi, dil))
    s_ret2 = s_ret.reshape(s_ret.shape[0], bsz, B_HEADS // 2, LANES, B_DV)
    per_layer = []
    for i in range(depth):
        def mix_fn(proj3, ab3, i=i):
            ya, s_new = _mixer_delta(proj3, ab3, s_conv[i], s_delta[i], lw['conv_w'][i], wts['par'][i],
                                     lw['a_gain'][i].reshape(1, A_DV), n_s=n_s, l_r=t_len, chunk=t_len, carry=False)
            yb, r_new = _mixer_ret(proj3, cos_t, sin_t, s_ret2[i], n_s=n_s, l_r=t_len, chunk=t_len, carry=False)
            yc = _mixer_pool(proj3, s_pool[i], lw['c_w'][i].astype(BF16), lw['c_scale'][i].reshape(1, MIX_W),
                             n_s=1, l_r=t_len, pos0=PAST_LEN)
            od, lse = [], []
            for gi, dil in enumerate(D_DILATIONS):
                o, l, out_bufs[gi] = _dilated_step(proj3, cache4[gi], out_bufs[gi], i, gi, *step_bias[gi])
                od.append(o)
                lse.append(l)
            return ya, yb, yc, od, lse, (s_new, r_new)

        h2, proj3, (s_new, r_new) = _layer_common(h2, p[i].reshape(n, PLE_DIM), bsz, t_len, wts, lw, i,
                                                  mix_fn, n, i == depth - 1, g_final)
        pool_new = jnp.concatenate([s_pool[i][:, t_len:], proj3[:, :, COL_CIN:COL_CIN + MIX_W]], axis=1)
        per_layer.append((s_new,
                          proj3[:, t_len - (A_CONV - 1):, COL_AQKV:COL_AQKV + A_QKV],
                          r_new.reshape(bsz, B_HEADS, B_DK, B_DV),
                          pool_new))
    stacked = tuple(jnp.stack([ns[j] for ns in per_layer]) for j in range(4))
    wins = tuple(out_bufs[gi].reshape(caches[gi].shape) for gi in range(N_DG))
    return h2.reshape(bsz, t_len, D_MODEL), stacked + wins


def kernel(x_prompt, x_sample, state_delta, state_delta_conv, state_ret, state_pool, cache_win0, cache_win1, cache_win2, p_prompt, p_sample, g_mix, w_in, conv_w, a_log, dt_bias, a_gain, c_w, c_scale, t5_bias, w_br, w_o, g_ffn, w_gate, w_up, w_down, g_ple, w_ple_gate, w_ple, g_final):
    depth = w_in.shape[0]
    wts = _prep_weights(w_in, w_br, w_o, w_gate, w_up, w_down, w_ple_gate, w_ple, a_log, dt_bias)
    lw = dict(g_mix=g_mix, conv_w=conv_w, a_gain=a_gain, c_w=c_w, c_scale=c_scale, g_ffn=g_ffn, g_ple=g_ple)
    y_p, (delta_p, conv_p, ret_p, pool_p, win0_p, win1_p, win2_p) = _run_prompt(
        x_prompt, p_prompt, wts, lw, t5_bias, g_final, depth)
    y_s, (delta_s, conv_s, ret_s, pool_s, win0_s, win1_s, win2_s) = _run_sample(
        x_sample, p_sample, (state_delta, state_delta_conv, state_ret, state_pool,
                             (cache_win0, cache_win1, cache_win2)), wts, lw, t5_bias, g_final, depth)
    return (y_p, y_s, delta_p, delta_s, conv_p, conv_s, ret_p, ret_s,
            pool_p, pool_s, win0_p, win0_s, win1_p, win1_s, win2_p, win2_s)
```

```python
import functools
import math

import numpy as np
import jax
import jax.numpy as jnp
from jax import lax
from jax.experimental import pallas as pl
from jax.experimental.pallas import tpu as pltpu

F32 = jnp.float32
BF16 = jnp.bfloat16
HIGHEST = lax.Precision.HIGHEST

D_MODEL = 1024
DEPTH = 4
PAST_LEN = 8192
MIX_W = D_MODEL // 2
N_BRANCH = 4
A_HEADS = 4
A_DK = MIX_W // A_HEADS
A_DV = MIX_W // A_HEADS
A_QKV = A_HEADS * (2 * A_DK + A_DV)
A_CONV = 4
A_CHUNK = 64
B_HEADS = 4
B_DV = MIX_W // B_HEADS
B_DK = B_DV // 2
B_CHUNK = 128
ROPE_BASE = 10000.0
C_GROUPS = 4
C_GW = MIX_W // C_GROUPS
POOL_WINDOWS = (2, 4, 8, 16)
POOL_MAX = 16
D_DILATIONS = (1, 4, 16)
N_DG = 3
D_SPAN = 128
D_HEADS = 4
D_HD = MIX_W // D_HEADS
REL_BUCKETS = 32
REL_MAX_DIST = D_SPAN * 16
D_FF = -(-8 * D_MODEL // (3 * 256)) * 256
PLE_DIM = 256
EPS = 1e-6

LANES = 128
SUBLANES = 8
VMEM_LIMIT_BYTES = 56 * 1024 * 1024

PROJ_W = 12800
COL_AQKV = 0
COL_AZ = 1536
COL_BQ = 2048
COL_BK = 2304
COL_BV = 2560
COL_BG = 3072
COL_CIN = 3584
COL_GATE = 4096
COL_DQKV = 8192
DBLK = D_HEADS * D_HD


def _cparams(sem):
    return pltpu.CompilerParams(dimension_semantics=sem, vmem_limit_bytes=VMEM_LIMIT_BYTES)


def _nt(a, b, precision=None):
    return lax.dot_general(a, b, (((1,), (1,)), ((), ())), precision=precision,
                           preferred_element_type=F32)


def _tn(a, b):
    return lax.dot_general(a, b, (((0,), (0,)), ((), ())), preferred_element_type=F32)


def _mm(a, b, precision=None):
    return jnp.dot(a, b, precision=precision, preferred_element_type=F32)


def _sigmoid(x):
    return 1.0 / (1.0 + jnp.exp(-x))


def _silu(x):
    return x * _sigmoid(x)


def _rms_rows(x, g):
    return x * lax.rsqrt(jnp.mean(x * x, axis=-1, keepdims=True) + EPS) * g


def _norm_matmul_kernel(x_ref, g_ref, w_ref, o_ref, u_ref):
    @pl.when(pl.program_id(1) == 0)
    def _():
        u_ref[...] = _rms_rows(x_ref[...], g_ref[...]).astype(BF16)

    o_ref[...] = _mm(u_ref[...], w_ref[...])


def _norm_matmul(x, g, w, tm, tn):
    n, d = x.shape
    wcols = w.shape[1]
    return pl.pallas_call(
        _norm_matmul_kernel,
        grid=(n // tm, wcols // tn),
        in_specs=[pl.BlockSpec((tm, d), lambda i, j: (i, 0)),
                  pl.BlockSpec((1, d), lambda i, j: (0, 0)),
                  pl.BlockSpec((d, tn), lambda i, j: (0, j))],
        out_specs=pl.BlockSpec((tm, tn), lambda i, j: (i, j)),
        out_shape=jax.ShapeDtypeStruct((n, wcols), F32),
        scratch_shapes=[pltpu.VMEM((tm, d), BF16)],
        compiler_params=_cparams(("parallel", "arbitrary")),
        name="norm_matmul",
    )(x, g.reshape(1, d), w)


def _delta_kernel(qkv_ref, z_ref, ab_ref, cprev_ref, sprev_ref, convw_ref, par_ref, gain_ref,
                  o_ref, snew_ref, ext_ref, s_ref, vn_ref, qs_ref, *, chunk, carry):
    j = pl.program_id(1)
    n_s, l_r, _ = qkv_ref.shape
    t_b = n_s * l_r
    n_c = t_b // chunk
    hdr = SUBLANES
    pre = A_CONV - 1

    @pl.when(j == 0)
    def _():
        ext_ref[:, 0:hdr, :] = jnp.zeros((n_s, hdr, A_QKV), F32)
        ext_ref[:, hdr - pre:hdr, :] = cprev_ref[...]
        if carry:
            s_ref[...] = sprev_ref[0]

    if carry:
        @pl.when(j > 0)
        def _():
            ext_ref[:, 0:hdr, :] = ext_ref[:, l_r:l_r + hdr, :]

    ext_ref[:, hdr:hdr + l_r, :] = qkv_ref[...]
    cw = convw_ref[...]

    def conv_cols(c0):
        acc = None
        for t in range(A_CONV):
            term = (ext_ref[:, hdr - pre + t:hdr - pre + t + l_r, c0:c0 + LANES]
                    * cw[t:t + 1, c0:c0 + LANES])
            acc = term if acc is None else acc + term
        return _silu(acc).reshape(t_b, LANES)

    ab = ab_ref[...].reshape(t_b, LANES)
    par = par_ref[...]
    beta_all = _sigmoid(ab)
    xs = ab + par[1:2, :]
    softplus = jnp.maximum(xs, 0.0) + jnp.log1p(jnp.exp(-jnp.abs(xs)))
    g_all = -jnp.exp(par[0:1, :]) * softplus

    sh = int(math.log2(chunk))
    ri = lax.broadcasted_iota(jnp.int32, (t_b, t_b), 0)
    ci = lax.broadcasted_iota(jnp.int32, (t_b, t_b), 1)
    same = lax.shift_right_logical(ri, sh) == lax.shift_right_logical(ci, sh)
    causal = jnp.logical_and(same, ci <= ri)
    strict = jnp.logical_and(same, ci < ri)
    gc_all = _mm(causal.astype(F32), g_all, precision=HIGHEST)
    sel = (lax.broadcasted_iota(jnp.int32, (SUBLANES, LANES), 0)
           == lax.broadcasted_iota(jnp.int32, (SUBLANES, LANES), 1)).astype(F32)
    gc_rows = _nt(sel, gc_all, precision=HIGHEST)

    gain = gain_ref[...]
    z = z_ref[...].reshape(t_b, MIX_W)
    row_id = lax.broadcasted_iota(jnp.int32, (t_b, 1), 0)

    outs = []
    for h in range(A_HEADS):
        gcc = gc_all[:, A_HEADS + h:A_HEADS + h + 1]
        gcr = gc_rows[A_HEADS + h:A_HEADS + h + 1, :]
        bcol = beta_all[:, h:h + 1]
        gam = jnp.where(causal, jnp.exp(jnp.where(causal, gcc - gcr, 0.0)), 0.0)
        q = conv_cols(h * A_DK)
        k = conv_cols(A_HEADS * A_DK + h * A_DK)
        v = conv_cols(2 * A_HEADS * A_DK + h * A_DV)
        q = q * lax.rsqrt(jnp.sum(q * q, axis=-1, keepdims=True) + EPS) * (A_DK ** -0.5)
        k = k * lax.rsqrt(jnp.sum(k * k, axis=-1, keepdims=True) + EPS)
        kb = k.astype(BF16)
        kk = _nt(kb, kb)
        qk = (_nt(q.astype(BF16), kb) * gam).astype(BF16)
        a = jnp.where(strict, bcol * kk * gam, 0.0)
        r = -a
        p = a
        for _ in range(sh - 1):
            pb = p.astype(BF16)
            p = _mm(pb, pb)
            r = r + p + _mm(p.astype(BF16), r.astype(BF16))
        ecol = jnp.exp(gcc)
        rhs = jnp.concatenate([v * bcol, k * (bcol * ecol)], axis=1)
        sol = rhs + _mm(r.astype(BF16), rhs.astype(BF16))
        u = sol[:, :A_DV]
        w = sol[:, A_DV:]
        qe = q * ecol
        for c in range(n_c):
            r0 = c * chunk
            if carry:
                s = s_ref[h]
            else:
                s = sprev_ref[c, h]
            wq = jnp.concatenate([w[r0:r0 + chunk], qe[r0:r0 + chunk]], axis=0).astype(BF16)
            t = _mm(wq, s.astype(BF16))
            vnew = u[r0:r0 + chunk] - t[:chunk]
            vn_ref[r0:r0 + chunk, :] = vnew
            qs_ref[r0:r0 + chunk, :] = t[chunk:]
            g_last = gcc[r0 + chunk - 1:r0 + chunk, :]
            kd = k * jnp.exp(g_last - gcc)
            if chunk >= 2 * SUBLANES:
                upd = _tn(kd[r0:r0 + chunk].astype(BF16), vnew.astype(BF16))
            else:
                in_c = jnp.logical_and(row_id >= r0, row_id < r0 + chunk)
                upd = _tn(jnp.where(in_c, kd, 0.0).astype(BF16),
                          jnp.where(in_c, vn_ref[...], 0.0).astype(BF16))
            s_new = s * jnp.exp(g_last) + upd
            if carry:
                s_ref[h] = s_new
            else:
                snew_ref[c, h] = s_new
        o = qs_ref[...] + _mm(qk, vn_ref[...].astype(BF16))
        o = o * lax.rsqrt(jnp.mean(o * o, axis=-1, keepdims=True) + EPS) * gain
        outs.append(o * _silu(z[:, h * A_DV:(h + 1) * A_DV]))
    o_ref[...] = jnp.concatenate(outs, axis=1).reshape(n_s, l_r, MIX_W)

    if carry:
        @pl.when(j == pl.num_programs(1) - 1)
        def _():
            snew_ref[0] = s_ref[...]


def _mixer_delta(proj3, ab3, conv_prev, s_prev, conv_w, par, gain, *, n_s, l_r, chunk, carry):
    bsz, t_len, _ = proj3.shape
    t_b = n_s * l_r
    grid = (bsz // n_s, t_len // l_r)
    kern = functools.partial(_delta_kernel, chunk=chunk, carry=carry)
    return pl.pallas_call(
        kern,
        grid=grid,
        in_specs=[
            pl.BlockSpec((n_s, l_r, A_QKV), lambda b, j: (b, j, COL_AQKV // A_QKV)),
            pl.BlockSpec((n_s, l_r, MIX_W), lambda b, j: (b, j, COL_AZ // MIX_W)),
            pl.BlockSpec((n_s, l_r, LANES), lambda b, j: (b, j, 0)),
            pl.BlockSpec((n_s, A_CONV - 1, A_QKV), lambda b, j: (b, 0, 0)),
            pl.BlockSpec((n_s, A_HEADS, A_DK, A_DV), lambda b, j: (b, 0, 0, 0)),
            pl.BlockSpec((A_CONV, A_QKV), lambda b, j: (0, 0)),
            pl.BlockSpec((SUBLANES, LANES), lambda b, j: (0, 0)),
            pl.BlockSpec((1, A_DV), lambda b, j: (0, 0)),
        ],
        out_specs=[
            pl.BlockSpec((n_s, l_r, MIX_W), lambda b, j: (b, j, 0)),
            pl.BlockSpec((n_s, A_HEADS, A_DK, A_DV), lambda b, j: (b, 0, 0, 0)),
        ],
        out_shape=[jax.ShapeDtypeStruct((bsz, t_len, MIX_W), F32),
                   jax.ShapeDtypeStruct((bsz, A_HEADS, A_DK, A_DV), F32)],
        scratch_shapes=[pltpu.VMEM((n_s, SUBLANES + l_r + SUBLANES, A_QKV), F32),
                        pltpu.VMEM((A_HEADS, A_DK, A_DV), F32),
                        pltpu.VMEM((t_b, A_DV), F32),
                        pltpu.VMEM((t_b, A_DV), F32)],
        compiler_params=_cparams(("parallel", "arbitrary")),
        name="mixer_delta",
    )(proj3, proj3, ab3, conv_prev, s_prev, conv_w, par, gain)


def _ret_log_gamma(h):
    return math.log1p(-(2.0 ** (-5.0 - h)))


def _ret_kernel(q_ref, k_ref, v_ref, g_ref, cos_ref, sin_ref, rprev_ref,
                o_ref, rnew_ref, r_ref, *, chunk, carry):
    j = pl.program_id(1)
    n_s, l_r, _ = q_ref.shape
    t_b = n_s * l_r
    n_c = t_b // chunk
    half = B_DK // 2

    if carry:
        @pl.when(j == 0)
        def _():
            r_ref[...] = rprev_ref[0]

    cos = cos_ref[...]
    sin = sin_ref[...]
    lane = lax.broadcasted_iota(jnp.int32, (1, LANES), 1)
    first_half = (lane % B_DK) < half
    lane_head = lane // B_DK

    def rope(x):
        swapped = jnp.where(first_half, pltpu.roll(x, LANES - half, axis=1),
                            pltpu.roll(x, half, axis=1))
        return x * cos + swapped * sin

    idx_i = lax.broadcasted_iota(jnp.int32, (chunk, chunk), 0)
    idx_j = lax.broadcasted_iota(jnp.int32, (chunk, chunk), 1)
    pos_col = lax.broadcasted_iota(jnp.int32, (chunk, 1), 0).astype(F32)
    row_head = lax.broadcasted_iota(jnp.int32, (LANES, 1), 0) // B_DK
    gate = g_ref[...].reshape(t_b, MIX_W)
    vv = v_ref[...].reshape(t_b, MIX_W)

    small = chunk < 2 * SUBLANES
    if small:
        sh = int(math.log2(chunk))
        bi = lax.broadcasted_iota(jnp.int32, (t_b, t_b), 0)
        bj = lax.broadcasted_iota(jnp.int32, (t_b, t_b), 1)
        bsame = lax.shift_right_logical(bi, sh) == lax.shift_right_logical(bj, sh)
        bcausal = jnp.logical_and(bsame, bj <= bi)
        bdiff = (bi - bj).astype(F32)
        row_id = lax.broadcasted_iota(jnp.int32, (t_b, 1), 0)
        pos_in = (row_id & (chunk - 1)).astype(F32)

    outs = [None] * B_HEADS
    for p in range(B_HEADS // 2):
        q2 = rope(q_ref[...].reshape(t_b, B_HEADS * B_DK)[:, p * LANES:(p + 1) * LANES])
        k2 = rope(k_ref[...].reshape(t_b, B_HEADS * B_DK)[:, p * LANES:(p + 1) * LANES]) * (B_DK ** -0.5)
        lgs = [_ret_log_gamma(2 * p + hh) for hh in range(2)]
        qh = [jnp.where(lane_head == hh, q2, 0.0) for hh in range(2)]
        kh = [jnp.where(lane_head == hh, k2, 0.0) for hh in range(2)]
        vh = [vv[:, (2 * p + hh) * B_DV:(2 * p + hh + 1) * B_DV] for hh in range(2)]
        row_scale = jnp.where(row_head == 0, math.exp(lgs[0] * chunk), math.exp(lgs[1] * chunk))
        if small:
            intra = []
            for hh in range(2):
                dec = jnp.where(bcausal, jnp.exp(lgs[hh] * jnp.where(bcausal, bdiff, 0.0)), 0.0)
                sc = _nt(qh[hh].astype(BF16), kh[hh].astype(BF16)) * dec
                intra.append(_mm(sc.astype(BF16), vh[hh].astype(BF16)))
            cross = [[], []]
            for c in range(n_c):
                r0 = c * chunk
                rp = r_ref[p] if carry else rprev_ref[c, p]
                qq = jnp.concatenate([qh[0][r0:r0 + chunk], qh[1][r0:r0 + chunk]], axis=0)
                t = _mm(qq.astype(BF16), rp.astype(BF16))
                cross[0].append(t[:chunk])
                cross[1].append(t[chunk:])
                in_c = jnp.logical_and(row_id >= r0, row_id < r0 + chunk)
                upd = None
                for hh in range(2):
                    kd = jnp.where(in_c, kh[hh] * jnp.exp(lgs[hh] * (chunk - 1.0 - pos_in)), 0.0)
                    term = _tn(kd.astype(BF16), vh[hh].astype(BF16))
                    upd = term if upd is None else upd + term
                rn = rp * row_scale + upd
                if carry:
                    r_ref[p] = rn
                else:
                    rnew_ref[c, p] = rn
            for hh in range(2):
                cr = jnp.concatenate(cross[hh], axis=0) * jnp.exp(lgs[hh] * (pos_in + 1.0))
                outs[2 * p + hh] = intra[hh] + cr
        else:
            pieces = [[], []]
            for c in range(n_c):
                r0 = c * chunk
                rp = r_ref[p] if carry else rprev_ref[c, p]
                rpb = rp.astype(BF16)
                upd = None
                for hh in range(2):
                    qc = qh[hh][r0:r0 + chunk].astype(BF16)
                    kc = kh[hh][r0:r0 + chunk]
                    vc = vh[hh][r0:r0 + chunk].astype(BF16)
                    causal = idx_j <= idx_i
                    dec = jnp.where(causal, jnp.exp(lgs[hh] * jnp.where(causal, (idx_i - idx_j).astype(F32), 0.0)), 0.0)
                    sc = _nt(qc, kc.astype(BF16)) * dec
                    oc = _mm(sc.astype(BF16), vc) + _mm(qc, rpb) * jnp.exp(lgs[hh] * (pos_col + 1.0))
                    pieces[hh].append(oc)
                    kd = kc * jnp.exp(lgs[hh] * (chunk - 1.0 - pos_col))
                    term = _tn(kd.astype(BF16), vc)
                    upd = term if upd is None else upd + term
                rn = rp * row_scale + upd
                if carry:
                    r_ref[p] = rn
                else:
                    rnew_ref[c, p] = rn
            for hh in range(2):
                outs[2 * p + hh] = jnp.concatenate(pieces[hh], axis=0) if n_c > 1 else pieces[hh][0]

    res = []
    for h in range(B_HEADS):
        o = outs[h]
        o = o * lax.rsqrt(jnp.mean(o * o, axis=-1, keepdims=True) + EPS)
        res.append(o * _silu(gate[:, h * B_DV:(h + 1) * B_DV]))
    o_ref[...] = jnp.concatenate(res, axis=1).reshape(n_s, l_r, MIX_W)

    if carry:
        @pl.when(j == pl.num_programs(1) - 1)
        def _():
            rnew_ref[0] = r_ref[...]


def _mixer_ret(proj3, cos_t, sin_t, r_prev, *, n_s, l_r, chunk, carry):
    bsz, t_len, _ = proj3.shape
    t_b = n_s * l_r
    grid = (bsz // n_s, t_len // l_r)
    n_tab = cos_t.shape[0] // t_b
    kern = functools.partial(_ret_kernel, chunk=chunk, carry=carry)
    qw = B_HEADS * B_DK
    return pl.pallas_call(
        kern,
        grid=grid,
        in_specs=[
            pl.BlockSpec((n_s, l_r, qw), lambda b, j: (b, j, COL_BQ // qw)),
            pl.BlockSpec((n_s, l_r, qw), lambda b, j: (b, j, COL_BK // qw)),
            pl.BlockSpec((n_s, l_r, MIX_W), lambda b, j: (b, j, COL_BV // MIX_W)),
            pl.BlockSpec((n_s, l_r, MIX_W), lambda b, j: (b, j, COL_BG // MIX_W)),
            pl.BlockSpec((t_b, LANES), lambda b, j: (j % n_tab, 0)),
            pl.BlockSpec((t_b, LANES), lambda b, j: (j % n_tab, 0)),
            pl.BlockSpec((n_s, B_HEADS // 2, LANES, B_DV), lambda b, j: (b, 0, 0, 0)),
        ],
        out_specs=[
            pl.BlockSpec((n_s, l_r, MIX_W), lambda b, j: (b, j, 0)),
            pl.BlockSpec((n_s, B_HEADS // 2, LANES, B_DV), lambda b, j: (b, 0, 0, 0)),
        ],
        out_shape=[jax.ShapeDtypeStruct((bsz, t_len, MIX_W), F32),
                   jax.ShapeDtypeStruct((bsz, B_HEADS // 2, LANES, B_DV), F32)],
        scratch_shapes=[pltpu.VMEM((B_HEADS // 2, LANES, B_DV), F32)],
        compiler_params=_cparams(("parallel", "arbitrary")),
        name="mixer_ret",
    )(proj3, proj3, proj3, proj3, cos_t, sin_t, r_prev)


def _rope_tables(pos, reps):
    half = B_DK // 2
    inv = ROPE_BASE ** (-jnp.arange(half, dtype=F32) / half)
    ang = pos.astype(F32)[:, None] * inv[None, :]
    cos = jnp.cos(ang)
    sin = jnp.sin(ang)
    cos_t = jnp.concatenate([cos, cos, cos, cos], axis=1)
    sin_t = jnp.concatenate([-sin, sin, -sin, sin], axis=1)
    return jnp.tile(cos_t, (reps, 1)), jnp.tile(sin_t, (reps, 1))


def _pool_kernel(x_ref, prev_ref, cw_ref, cs_ref, o_ref, ext_ref, *, pos0):
    j = pl.program_id(1)
    n_s, l_r, _ = x_ref.shape
    t_b = n_s * l_r
    hdr = POOL_MAX

    @pl.when(j == 0)
    def _():
        ext_ref[:, 0:hdr, :] = jnp.zeros((n_s, hdr, MIX_W), F32)
        ext_ref[:, 1:hdr, :] = prev_ref[...]

    @pl.when(j > 0)
    def _():
        ext_ref[:, 0:hdr, :] = ext_ref[:, l_r:l_r + hdr, :]

    ext_ref[:, hdr:hdr + l_r, :] = x_ref[...]
    pos = pos0 + j * l_r + lax.broadcasted_iota(jnp.int32, (1, l_r, 1), 1)
    outs = []
    for gi, w in enumerate(POOL_WINDOWS):
        c0 = gi * C_GW
        acc = None
        for t in range(w):
            term = ext_ref[:, hdr - t:hdr - t + l_r, c0:c0 + C_GW]
            acc = term if acc is None else acc + term
        cnt = jnp.minimum(pos + 1, w).astype(F32)
        y = acc / cnt - ext_ref[:, hdr:hdr + l_r, c0:c0 + C_GW]
        y = _mm(y.reshape(t_b, C_GW).astype(BF16), cw_ref[gi])
        outs.append(y)
    o_ref[...] = (jnp.concatenate(outs, axis=1) * cs_ref[...]).reshape(n_s, l_r, MIX_W)


def _mixer_pool(proj3, prev, c_w, c_scale, *, n_s, l_r, pos0):
    bsz, t_len, _ = proj3.shape
    grid = (bsz // n_s, t_len // l_r)
    kern = functools.partial(_pool_kernel, pos0=pos0)
    return pl.pallas_call(
        kern,
        grid=grid,
        in_specs=[
            pl.BlockSpec((n_s, l_r, MIX_W), lambda b, j: (b, j, COL_CIN // MIX_W)),
            pl.BlockSpec((n_s, POOL_MAX - 1, MIX_W), lambda b, j: (b, 0, 0)),
            pl.BlockSpec((C_GROUPS, C_GW, C_GW), lambda b, j: (0, 0, 0)),
            pl.BlockSpec((1, MIX_W), lambda b, j: (0, 0)),
        ],
        out_specs=pl.BlockSpec((n_s, l_r, MIX_W), lambda b, j: (b, j, 0)),
        out_shape=jax.ShapeDtypeStruct((bsz, t_len, MIX_W), F32),
        scratch_shapes=[pltpu.VMEM((n_s, POOL_MAX + l_r + SUBLANES, MIX_W), F32)],
        compiler_params=_cparams(("parallel", "arbitrary")),
        name="mixer_pool",
    )(proj3, prev, c_w, c_scale)


def _lookup_kernel(oh_ref, tab_ref, o_ref):
    o_ref[...] = _mm(oh_ref[...], tab_ref[...], precision=HIGHEST)


def _bias_lookup(idx, table):
    n_valid = idx.shape[0]
    tr = 4096
    n_rows = -(-n_valid // tr) * tr
    idx = np.concatenate([idx, np.zeros((n_rows - n_valid,), idx.dtype)])
    onehot = jnp.asarray(idx[:, None] == np.arange(REL_BUCKETS)[None, :], dtype=F32)
    tab = jnp.pad(table, ((0, 0), (0, LANES - table.shape[1])))
    out = pl.pallas_call(
        _lookup_kernel,
        grid=(n_rows // tr,),
        in_specs=[pl.BlockSpec((tr, REL_BUCKETS), lambda i: (i, 0)),
                  pl.BlockSpec((REL_BUCKETS, LANES), lambda i: (0, 0))],
        out_specs=pl.BlockSpec((tr, LANES), lambda i: (i, 0)),
        out_shape=jax.ShapeDtypeStruct((n_rows, LANES), F32),
        compiler_params=_cparams(("parallel",)),
        name="bias_lookup",
    )(onehot, tab)
    return out[:n_valid, :table.shape[1]]


def _t5_bucket_np(dist):
    exact = REL_BUCKETS // 2
    n = np.maximum(dist, 0)
    nf = np.maximum(n, 1).astype(np.float32)
    large = exact + (np.log(nf / np.float32(exact)) / np.float32(math.log(REL_MAX_DIST / exact))
                     * np.float32(REL_BUCKETS - exact)).astype(np.int32)
    large = np.minimum(large, REL_BUCKETS - 1)
    return np.where(n < exact, n, large).astype(np.int32)


SUPER = D_SPAN * D_DILATIONS[-1]
TILES = SUPER // D_SPAN


def _band_fused_kernel(*refs):
    ins = refs[:5 * N_DG]
    bias_ref = refs[5 * N_DG]
    o_ref = refs[5 * N_DG + 1]
    scr = refs[5 * N_DG + 2:]
    kf = scr[0:2 * N_DG:2]
    vf = scr[1:2 * N_DG:2]
    og = scr[2 * N_DG:3 * N_DG]
    lg = scr[3 * N_DG:4 * N_DG]
    has_prev = pl.program_id(2) > 0
    ii = lax.broadcasted_iota(jnp.int32, (D_SPAN, 2 * D_SPAN), 0)
    jj = lax.broadcasted_iota(jnp.int32, (D_SPAN, 2 * D_SPAN), 1)
    band = jnp.logical_and(jj >= ii, jj <= ii + D_SPAN)
    in_cur = jj >= D_SPAN
    scale = D_HD ** -0.5
    for g, dil in enumerate(D_DILATIONS):
        q_ref, k_ref, v_ref, kp_ref, vp_ref = ins[5 * g:5 * g + 5]
        span = D_SPAN * dil
        kf[g][0:span, :] = kp_ref[...]
        kf[g][span:span + SUPER, :] = k_ref[...]
        vf[g][0:span, :] = vp_ref[...]
        vf[g][span:span + SUPER, :] = v_ref[...]
        bias = bias_ref[g]
        sh = int(math.log2(dil))

        def tile(idx, carry, g=g, dil=dil, span=span, sh=sh, q_ref=q_ref, bias=bias):
            qi = lax.shift_right_logical(idx, sh)
            res = idx & (dil - 1)
            start = qi * span + res
            if dil == 1:
                start = pl.multiple_of(start, D_SPAN)
                rows = pl.ds(start, D_SPAN)
                krows = pl.ds(start, 2 * D_SPAN)
            else:
                rows = pl.ds(start, D_SPAN, stride=dil)
                krows = pl.ds(start, 2 * D_SPAN, stride=dil)
            q = q_ref[rows, :].astype(BF16)
            kk = kf[g][krows, :].astype(BF16)
            vv = vf[g][krows, :].astype(BF16)
            s = _nt(q, kk) * scale + bias
            valid = jnp.logical_and(band, jnp.logical_or(in_cur, jnp.logical_or(has_prev, qi > 0)))
            s = jnp.where(valid, s, -jnp.inf)
            m = jnp.max(s, axis=-1, keepdims=True)
            p = jnp.exp(s - m)
            l = jnp.sum(p, axis=-1, keepdims=True)
            og[g][rows, :] = _mm(p.astype(BF16), vv) / l
            lg[g][rows, :] = jnp.broadcast_to(m + jnp.log(l), (D_SPAN, D_HD))
            return carry

        lax.fori_loop(0, TILES, tile, 0, unroll=2)

    def merge(c, carry):
        rows = pl.ds(pl.multiple_of(c * D_SPAN, D_SPAN), D_SPAN)
        l0, l1, l2 = lg[0][rows, :], lg[1][rows, :], lg[2][rows, :]
        m = jnp.maximum(jnp.maximum(l0, l1), l2)
        e0, e1, e2 = jnp.exp(l0 - m), jnp.exp(l1 - m), jnp.exp(l2 - m)
        inv = 1.0 / (e0 + e1 + e2)
        o_ref[rows, :] = (e0 * og[0][rows, :] + e1 * og[1][rows, :] + e2 * og[2][rows, :]) * inv
        return carry

    lax.fori_loop(0, TILES, merge, 0)


def _dilated_band(proj3, band_bias):
    bsz, s_len, _ = proj3.shape
    n_blk = s_len // SUPER
    base = COL_DQKV // D_HD
    in_specs = []
    args = []
    for g, dil in enumerate(D_DILATIONS):
        span = D_SPAN * dil
        per = SUPER // span

        def col(c, g=g):
            return lambda b, h, n: (b, n, base + (c * N_DG + g) * D_HEADS + h)

        def col_prev(c, g=g, per=per):
            return lambda b, h, n: (b, jnp.maximum(n * per - 1, 0), base + (c * N_DG + g) * D_HEADS + h)

        in_specs += [pl.BlockSpec((None, SUPER, D_HD), col(0)),
                     pl.BlockSpec((None, SUPER, D_HD), col(1)),
                     pl.BlockSpec((None, SUPER, D_HD), col(2)),
                     pl.BlockSpec((None, span, D_HD), col_prev(1)),
                     pl.BlockSpec((None, span, D_HD), col_prev(2))]
        args += [proj3] * 5
    in_specs.append(pl.BlockSpec((N_DG, None, D_SPAN, 2 * D_SPAN), lambda b, h, n: (0, h, 0, 0)))
    scratch = []
    for dil in D_DILATIONS:
        scratch += [pltpu.VMEM((D_SPAN * dil + SUPER, D_HD), F32)] * 2
    scratch += [pltpu.VMEM((SUPER, D_HD), F32)] * (2 * N_DG)
    return pl.pallas_call(
        _band_fused_kernel,
        grid=(bsz, D_HEADS, n_blk),
        in_specs=in_specs,
        out_specs=pl.BlockSpec((None, SUPER, D_HD), lambda b, h, n: (b, n, h)),
        out_shape=jax.ShapeDtypeStruct((bsz, s_len, MIX_W), F32),
        scratch_shapes=scratch,
        compiler_params=_cparams(("parallel", "parallel", "arbitrary")),
        name="dilated_band",
    )(*args, band_bias)


KV_ROWS = 2 * D_HEADS


def _step_fused_kernel(q_ref, kn_ref, vn_ref, cache_ref, nxt_ref, bias_c_ref, mask_c_ref, bias_n_ref,
                       mask_n_ref, buf_ref, o_ref, lse_ref, out_ref, padk_ref, padv_ref, m_ref, l_ref, acc_ref):
    del buf_ref
    c = pl.program_id(1)
    n_ch = pl.num_programs(1)
    t_new = q_ref.shape[0]
    flat = cache_ref.shape[0]
    n_pos = flat // KV_ROWS
    shift = t_new * KV_ROWS
    scale = D_HD ** -0.5
    zeros_q = jnp.zeros((SUBLANES, D_HD), F32)

    out_ref[0:flat - shift, :] = cache_ref[shift:flat, :]

    @pl.when(c < n_ch - 1)
    def _():
        out_ref[flat - shift:flat, :] = nxt_ref[...]

    @pl.when(c == n_ch - 1)
    def _():
        for h in range(D_HEADS):
            cs = slice(h * D_HD, (h + 1) * D_HD)
            out_ref[pl.ds(flat - shift + h, t_new, stride=KV_ROWS), :] = kn_ref[:, cs]
            out_ref[pl.ds(flat - shift + D_HEADS + h, t_new, stride=KV_ROWS), :] = vn_ref[:, cs]

    @pl.when(c == 0)
    def _():
        padk_ref[...] = jnp.zeros(padk_ref.shape, F32)
        padv_ref[...] = jnp.zeros(padv_ref.shape, F32)
        for h in range(D_HEADS):
            cs = slice(h * D_HD, (h + 1) * D_HD)
            rs = slice(h * t_new, (h + 1) * t_new)
            padk_ref[h, 0:t_new, :] = kn_ref[:, cs]
            padv_ref[h, 0:t_new, :] = vn_ref[:, cs]
            lhs = jnp.concatenate([q_ref[:, cs], zeros_q], axis=0).astype(BF16)
            s = _nt(lhs, padk_ref[h].astype(BF16))[0:t_new] * scale + bias_n_ref[h] + mask_n_ref[...]
            m = jnp.max(s, axis=-1, keepdims=True)
            p = jnp.exp(s - m)
            m_ref[rs, :] = m
            l_ref[rs, :] = jnp.sum(p, axis=-1, keepdims=True)
            p16 = jnp.concatenate([p, jnp.zeros_like(p)], axis=0).astype(BF16)
            acc_ref[rs, :] = _mm(p16, padv_ref[h].astype(BF16))[0:t_new]

    for h in range(D_HEADS):
        cs = slice(h * D_HD, (h + 1) * D_HD)
        rs = slice(h * t_new, (h + 1) * t_new)
        kh = cache_ref[pl.ds(h, n_pos, stride=KV_ROWS), :].astype(BF16)
        vh = cache_ref[pl.ds(D_HEADS + h, n_pos, stride=KV_ROWS), :].astype(BF16)
        lhs = jnp.concatenate([q_ref[:, cs], zeros_q], axis=0).astype(BF16)
        s = _nt(lhs, kh)[0:t_new] * scale + bias_c_ref[h] + mask_c_ref[...]
        m_old = m_ref[rs, :]
        m_new = jnp.maximum(m_old, jnp.max(s, axis=-1, keepdims=True))
        alpha = jnp.exp(m_old - m_new)
        p = jnp.exp(s - m_new)
        l_ref[rs, :] = alpha * l_ref[rs, :] + jnp.sum(p, axis=-1, keepdims=True)
        p16 = jnp.concatenate([p, jnp.zeros_like(p)], axis=0).astype(BF16)
        acc_ref[rs, :] = alpha * acc_ref[rs, :] + _mm(p16, vh)[0:t_new]
        m_ref[rs, :] = m_new

    @pl.when(c == n_ch - 1)
    def _():
        for h in range(D_HEADS):
            cs = slice(h * D_HD, (h + 1) * D_HD)
            rs = slice(h * t_new, (h + 1) * t_new)
            l = l_ref[rs, :]
            o_ref[:, cs] = acc_ref[rs, :] / l
            lse_ref[:, h:h + 1] = m_ref[rs, :] + jnp.log(l)


def _dilated_step(proj3, cache_flat, out_buf, layer, gi, bias_c, mask_c, bias_n, mask_n):
    bsz, t_new, _ = proj3.shape
    l_buf = cache_flat.shape[1] // KV_ROWS
    n_pos = min(l_buf, 1024)
    n_ch = l_buf // n_pos
    flat = n_pos * KV_ROWS
    shift = t_new * KV_ROWS
    per = flat // shift
    last = l_buf * KV_ROWS // shift - 1
    base = COL_DQKV // DBLK
    cq, ck, cv = base + gi, base + N_DG + gi, base + 2 * N_DG + gi
    n_hq = D_HEADS * t_new
    return pl.pallas_call(
        _step_fused_kernel,
        grid=(bsz, n_ch),
        in_specs=[
            pl.BlockSpec((None, t_new, DBLK), lambda b, c: (b, 0, cq)),
            pl.BlockSpec((None, t_new, DBLK), lambda b, c: (b, 0, ck)),
            pl.BlockSpec((None, t_new, DBLK), lambda b, c: (b, 0, cv)),
            pl.BlockSpec((None, flat, D_HD), lambda b, c: (layer * bsz + b, c, 0)),
            pl.BlockSpec((None, shift, D_HD), lambda b, c: (layer * bsz + b, jnp.minimum((c + 1) * per, last), 0)),
            pl.BlockSpec((D_HEADS, t_new, n_pos), lambda b, c: (0, 0, c)),
            pl.BlockSpec((t_new, n_pos), lambda b, c: (0, c)),
            pl.BlockSpec((D_HEADS, t_new, LANES), lambda b, c: (0, 0, 0)),
            pl.BlockSpec((t_new, LANES), lambda b, c: (0, 0)),
            pl.BlockSpec(memory_space=pl.ANY),
        ],
        out_specs=[
            pl.BlockSpec((None, t_new, DBLK), lambda b, c: (b, 0, 0)),
            pl.BlockSpec((None, t_new, D_HEADS), lambda b, c: (b, 0, 0)),
            pl.BlockSpec((None, flat, D_HD), lambda b, c: (layer * bsz + b, c, 0)),
        ],
        out_shape=[jax.ShapeDtypeStruct((bsz, t_new, DBLK), F32),
                   jax.ShapeDtypeStruct((bsz, t_new, D_HEADS), F32),
                   jax.ShapeDtypeStruct(out_buf.shape, F32)],
        scratch_shapes=[pltpu.VMEM((D_HEADS, LANES, D_HD), F32),
                        pltpu.VMEM((D_HEADS, LANES, D_HD), F32),
                        pltpu.VMEM((n_hq, 1), F32),
                        pltpu.VMEM((n_hq, 1), F32),
                        pltpu.VMEM((n_hq, D_HD), F32)],
        input_output_aliases={9: 2},
        compiler_params=_cparams(("parallel", "arbitrary")),
        name="dilated_step",
    )(proj3, proj3, proj3, cache_flat, cache_flat, bias_c, mask_c, bias_n, mask_n, out_buf)


def _merge_kernel(*refs, split_d):
    if split_d:
        (gate_ref, ya_ref, yb_ref, yc_ref, o0_ref, o1_ref, o2_ref, l0_ref, l1_ref, l2_ref,
         wbr_ref, wo_ref, h_ref, out_ref) = refs
        l0, l1, l2 = l0_ref[...], l1_ref[...], l2_ref[...]
        m = jnp.maximum(jnp.maximum(l0, l1), l2)
        e0, e1, e2 = jnp.exp(l0 - m), jnp.exp(l1 - m), jnp.exp(l2 - m)
        inv = 1.0 / (e0 + e1 + e2)
        w0, w1, w2 = e0 * inv, e1 * inv, e2 * inv
        yd = []
        for h in range(D_HEADS):
            cs = slice(h * D_HD, (h + 1) * D_HD)
            yd.append(w0[:, h:h + 1] * o0_ref[:, cs] + w1[:, h:h + 1] * o1_ref[:, cs]
                      + w2[:, h:h + 1] * o2_ref[:, cs])
        y_d = jnp.concatenate(yd, axis=1)
    else:
        gate_ref, ya_ref, yb_ref, yc_ref, yd_ref, wbr_ref, wo_ref, h_ref, out_ref = refs
        y_d = yd_ref[...]
    branches = [ya_ref[...], yb_ref[...], yc_ref[...], y_d]
    merged = None
    for nbr in range(N_BRANCH):
        up = _mm(branches[nbr].astype(BF16), wbr_ref[nbr])
        term = _sigmoid(gate_ref[:, nbr * D_MODEL:(nbr + 1) * D_MODEL]) * up
        merged = term if merged is None else merged + term
    out_ref[...] = h_ref[...] + _mm(merged.astype(BF16), wo_ref[...])


def _merge(proj2, ya, yb, yc, yd, w_br, w_o, h2, tm):
    n = h2.shape[0]
    row = lambda width: pl.BlockSpec((tm, width), lambda i: (i, 0))
    split_d = isinstance(yd, tuple)
    if split_d:
        d_args = list(yd[0]) + list(yd[1])
        d_specs = [row(MIX_W)] * N_DG + [row(D_HEADS)] * N_DG
    else:
        d_args = [yd]
        d_specs = [row(MIX_W)]
    return pl.pallas_call(
        functools.partial(_merge_kernel, split_d=split_d),
        grid=(n // tm,),
        in_specs=[pl.BlockSpec((tm, N_BRANCH * D_MODEL), lambda i: (i, COL_GATE // (N_BRANCH * D_MODEL))),
                  row(MIX_W), row(MIX_W), row(MIX_W)] + d_specs + [
                  pl.BlockSpec((N_BRANCH, MIX_W, D_MODEL), lambda i: (0, 0, 0)),
                  pl.BlockSpec((D_MODEL, D_MODEL), lambda i: (0, 0)),
                  row(D_MODEL)],
        out_specs=row(D_MODEL),
        out_shape=jax.ShapeDtypeStruct((n, D_MODEL), F32),
        compiler_params=_cparams(("parallel",)),
        name="branch_merge",
    )(proj2, ya, yb, yc, *d_args, w_br, w_o, h2)


def _ffn_kernel(h_ref, gffn_ref, wg_ref, wu_ref, wd_ref, p_ref, gple_ref, wpg_ref, wple_ref, gfin_ref,
                out_ref, f_ref, acc_ref, *, final):
    k = pl.program_id(1)

    @pl.when(k == 0)
    def _():
        f_ref[...] = _rms_rows(h_ref[...], gffn_ref[...]).astype(BF16)
        acc_ref[...] = jnp.zeros(acc_ref.shape, F32)

    f = f_ref[...]
    a = _mm(f, wg_ref[...])
    b = _mm(f, wu_ref[...])
    acc_ref[...] += _mm((_silu(a) * b).astype(BF16), wd_ref[...])

    @pl.when(k == pl.num_programs(1) - 1)
    def _():
        h2 = h_ref[...] + acc_ref[...]
        e = _rms_rows(h2, gple_ref[...]).astype(BF16)
        gate = _sigmoid(_mm(e, wpg_ref[...]))
        h3 = h2 + gate * _mm(p_ref[...].astype(BF16), wple_ref[...])
        if final:
            out_ref[...] = _rms_rows(h3, gfin_ref[...])
        else:
            out_ref[...] = h3


def _ffn(h2, g_ffn, w_gate, w_up, w_down, p2, g_ple, w_pg, w_ple, g_fin, tm, tf, final):
    n = h2.shape[0]
    vec = lambda: pl.BlockSpec((1, D_MODEL), lambda i, k: (0, 0))
    return pl.pallas_call(
        functools.partial(_ffn_kernel, final=final),
        grid=(n // tm, D_FF // tf),
        in_specs=[pl.BlockSpec((tm, D_MODEL), lambda i, k: (i, 0)),
                  vec(),
                  pl.BlockSpec((D_MODEL, tf), lambda i, k: (0, k)),
                  pl.BlockSpec((D_MODEL, tf), lambda i, k: (0, k)),
                  pl.BlockSpec((tf, D_MODEL), lambda i, k: (k, 0)),
                  pl.BlockSpec((tm, PLE_DIM), lambda i, k: (i, 0)),
                  vec(),
                  pl.BlockSpec((D_MODEL, D_MODEL), lambda i, k: (0, 0)),
                  pl.BlockSpec((PLE_DIM, D_MODEL), lambda i, k: (0, 0)),
                  vec()],
        out_specs=pl.BlockSpec((tm, D_MODEL), lambda i, k: (i, 0)),
        out_shape=jax.ShapeDtypeStruct((n, D_MODEL), F32),
        scratch_shapes=[pltpu.VMEM((tm, D_MODEL), BF16), pltpu.VMEM((tm, D_MODEL), F32)],
        compiler_params=_cparams(("parallel", "arbitrary")),
        name="ffn_ple",
    )(h2, g_ffn.reshape(1, -1), w_gate, w_up, w_down, p2, g_ple.reshape(1, -1), w_pg, w_ple,
      g_fin.reshape(1, -1))


def _prep_weights(w_in, w_br, w_o, w_gate, w_up, w_down, w_ple_gate, w_ple, a_log, dt_bias):
    o_ab = A_QKV + MIX_W
    o_bq = o_ab + 2 * A_HEADS
    o_d = o_bq + 2 * B_HEADS * B_DK + 3 * MIX_W
    o_g = o_d + 3 * N_DG * MIX_W
    w_main = jnp.concatenate([w_in[:, :, :o_ab], w_in[:, :, o_bq:o_d], w_in[:, :, o_g:],
                              w_in[:, :, o_d:o_g]], axis=-1).astype(BF16)
    w_ab = jnp.pad(w_in[:, :, o_ab:o_bq], ((0, 0), (0, 0), (0, LANES - 2 * A_HEADS))).astype(BF16)
    par = jnp.zeros((a_log.shape[0], SUBLANES, LANES), F32)
    par = par.at[:, 0, A_HEADS:2 * A_HEADS].set(a_log)
    par = par.at[:, 1, A_HEADS:2 * A_HEADS].set(dt_bias)
    return dict(w_main=w_main, w_ab=w_ab, par=par, w_br=w_br.astype(BF16), w_o=w_o.astype(BF16),
                w_gate=w_gate.astype(BF16), w_up=w_up.astype(BF16), w_down=w_down.astype(BF16),
                w_pg=w_ple_gate.astype(BF16), w_ple=w_ple.astype(BF16))


def _band_bias(t5_bias):
    i = np.arange(D_SPAN)[:, None]
    j = np.arange(2 * D_SPAN)[None, :]
    rel = i + D_SPAN - j
    out = []
    for gi, dil in enumerate(D_DILATIONS):
        idx = _t5_bucket_np(rel * dil).reshape(-1)
        tab = _bias_lookup(idx, t5_bias[:, gi * D_HEADS:(gi + 1) * D_HEADS])
        out.append(tab.reshape(D_SPAN, 2 * D_SPAN, D_HEADS).transpose(2, 0, 1))
    return jnp.stack(out)


def _step_bias(t5_bias, l_buf, t_new, gi, dil):
    t = np.arange(t_new)[:, None]
    j = np.arange(l_buf + LANES)[None, :]
    dist = l_buf + t - j
    valid = (dist >= 0) & (dist % dil == 0) & (dist <= D_SPAN * dil) & (j < l_buf + t_new)
    idx = _t5_bucket_np(np.where(valid, dist, 0)).reshape(-1)
    tab = _bias_lookup(idx, t5_bias[:, gi * D_HEADS:(gi + 1) * D_HEADS])
    tab = tab.reshape(t_new, l_buf + LANES, D_HEADS).transpose(2, 0, 1)
    tab = jnp.where(valid[None], tab, 0.0)
    mask = jnp.asarray(np.where(valid, 0.0, -np.inf), dtype=F32)
    return tab[:, :, :l_buf], mask[:, :l_buf], tab[:, :, l_buf:], mask[:, l_buf:]


def _layer_common(h2, p2, bsz, t_len, wts, lw, i, mix_fn, tm, final, g_final):
    n = bsz * t_len
    proj2 = _norm_matmul(h2, lw['g_mix'][i], wts['w_main'][i], tm, 1280)
    ab2 = _norm_matmul(h2, lw['g_mix'][i], wts['w_ab'][i], tm, LANES)
    proj3 = proj2.reshape(bsz, t_len, PROJ_W)
    ab3 = ab2.reshape(bsz, t_len, LANES)
    ya, yb, yc, yd, states = mix_fn(proj3, ab3)
    flat = lambda x: x.reshape(n, x.shape[-1])
    if isinstance(yd, tuple):
        yd = ([flat(x) for x in yd[0]], [flat(x) for x in yd[1]])
    else:
        yd = flat(yd)
    tm2 = min(tm, 256)
    h2 = _merge(proj2, flat(ya), flat(yb), flat(yc), yd, wts['w_br'][i], wts['w_o'][i], h2, tm2)
    tm3 = min(tm, 512)
    h2 = _ffn(h2, lw['g_ffn'][i], wts['w_gate'][i], wts['w_up'][i], wts['w_down'][i], p2,
              lw['g_ple'][i], wts['w_pg'][i], wts['w_ple'][i], g_final, tm3, D_FF // 2, final)
    return h2, proj3, states


def _run_prompt(x, p, wts, lw, t5_bias, g_final, depth):
    bsz, t_len, _ = x.shape
    n = bsz * t_len
    h2 = x.reshape(n, D_MODEL)
    pos = jnp.arange(t_len, dtype=jnp.int32)
    cos_t, sin_t = _rope_tables(pos, 1)
    band_bias = _band_bias(t5_bias)
    zero_s = jnp.zeros((bsz, A_HEADS, A_DK, A_DV), F32)
    zero_c = jnp.zeros((bsz, A_CONV - 1, A_QKV), F32)
    zero_r = jnp.zeros((bsz, B_HEADS // 2, LANES, B_DV), F32)
    zero_p = jnp.zeros((bsz, POOL_MAX - 1, MIX_W), F32)
    per_layer = []
    for i in range(depth):
        def mix_fn(proj3, ab3, i=i):
            ya, s_new = _mixer_delta(proj3, ab3, zero_c, zero_s, lw['conv_w'][i], wts['par'][i],
                                     lw['a_gain'][i].reshape(1, A_DV), n_s=1, l_r=128, chunk=A_CHUNK, carry=True)
            yb, r_new = _mixer_ret(proj3, cos_t, sin_t, zero_r, n_s=1, l_r=256, chunk=B_CHUNK, carry=True)
            yc = _mixer_pool(proj3, zero_p, lw['c_w'][i].astype(BF16), lw['c_scale'][i].reshape(1, MIX_W),
                             n_s=1, l_r=256, pos0=0)
            yd = _dilated_band(proj3, band_bias)
            return ya, yb, yc, yd, (s_new, r_new)

        h2, proj3, (s_new, r_new) = _layer_common(h2, p[i].reshape(n, PLE_DIM), bsz, t_len, wts, lw, i,
                                                  mix_fn, 1024, i == depth - 1, g_final)
        wins = []
        for gi, dil in enumerate(D_DILATIONS):
            keep = min(D_SPAN * dil, t_len)
            kcol = COL_DQKV + (N_DG + gi) * DBLK
            vcol = COL_DQKV + (2 * N_DG + gi) * DBLK
            kv = jnp.stack([proj3[:, t_len - keep:, kcol:kcol + DBLK],
                            proj3[:, t_len - keep:, vcol:vcol + DBLK]], axis=2)
            wins.append(kv.reshape(bsz, keep, 2, D_HEADS, D_HD))
        per_layer.append((s_new,
                          proj3[:, t_len - (A_CONV - 1):, COL_AQKV:COL_AQKV + A_QKV],
                          r_new.reshape(bsz, B_HEADS, B_DK, B_DV),
                          proj3[:, t_len - (POOL_MAX - 1):, COL_CIN:COL_CIN + MIX_W],
                          wins[0], wins[1], wins[2]))
    stacked = tuple(jnp.stack([ns[j] for ns in per_layer]) for j in range(7))
    return h2.reshape(bsz, t_len, D_MODEL), stacked


def _run_sample(x, p, states, wts, lw, t5_bias, g_final, depth):
    s_delta, s_conv, s_ret, s_pool, caches = states
    bsz, t_len, _ = x.shape
    n = bsz * t_len
    n_s = 16
    h2 = x.reshape(n, D_MODEL)
    pos = PAST_LEN + jnp.arange(t_len, dtype=jnp.int32)
    cos_t, sin_t = _rope_tables(pos, n_s)
    cache_flat, out_bufs, step_bias = [], [], []
    for gi, dil in enumerate(D_DILATIONS):
        c = caches[gi]
        l_buf = c.shape[2]
        flat_shape = (c.shape[0] * bsz, l_buf * KV_ROWS, D_HD)
        cache_flat.append(c.reshape(flat_shape))
        out_bufs.append(lax.empty(flat_shape, F32))
        step_bias.append(_step_bias(t5_bias, l_buf, t_len, gi, dil))
    s_ret2 = s_ret.reshape(s_ret.shape[0], bsz, B_HEADS // 2, LANES, B_DV)
    per_layer = []
    for i in range(depth):
        def mix_fn(proj3, ab3, i=i):
            ya, s_new = _mixer_delta(proj3, ab3, s_conv[i], s_delta[i], lw['conv_w'][i], wts['par'][i],
                                     lw['a_gain'][i].reshape(1, A_DV), n_s=n_s, l_r=t_len, chunk=t_len, carry=False)
            yb, r_new = _mixer_ret(proj3, cos_t, sin_t, s_ret2[i], n_s=n_s, l_r=t_len, chunk=t_len, carry=False)
            yc = _mixer_pool(proj3, s_pool[i], lw['c_w'][i].astype(BF16), lw['c_scale'][i].reshape(1, MIX_W),
                             n_s=1, l_r=t_len, pos0=PAST_LEN)
            od, lse = [], []
            for gi, dil in enumerate(D_DILATIONS):
                o, l, out_bufs[gi] = _dilated_step(proj3, cache_flat[gi], out_bufs[gi], i, gi, *step_bias[gi])
                od.append(o)
                lse.append(l)
            return ya, yb, yc, (od, lse), (s_new, r_new)

        h2, proj3, (s_new, r_new) = _layer_common(h2, p[i].reshape(n, PLE_DIM), bsz, t_len, wts, lw, i,
                                                  mix_fn, n, i == depth - 1, g_final)
        pool_new = jnp.concatenate([s_pool[i][:, t_len:], proj3[:, :, COL_CIN:COL_CIN + MIX_W]], axis=1)
        per_layer.append((s_new,
                          proj3[:, t_len - (A_CONV - 1):, COL_AQKV:COL_AQKV + A_QKV],
                          r_new.reshape(bsz, B_HEADS, B_DK, B_DV),
                          pool_new))
    stacked = tuple(jnp.stack([ns[j] for ns in per_layer]) for j in range(4))
    wins = tuple(out_bufs[gi].reshape(caches[gi].shape) for gi in range(N_DG))
    return h2.reshape(bsz, t_len, D_MODEL), stacked + wins


def kernel(x_prompt, x_sample, state_delta, state_delta_conv, state_ret, state_pool, cache_win0, cache_win1, cache_win2, p_prompt, p_sample, g_mix, w_in, conv_w, a_log, dt_bias, a_gain, c_w, c_scale, t5_bias, w_br, w_o, g_ffn, w_gate, w_up, w_down, g_ple, w_ple_gate, w_ple, g_final):
    depth = w_in.shape[0]
    wts = _prep_weights(w_in, w_br, w_o, w_gate, w_up, w_down, w_ple_gate, w_ple, a_log, dt_bias)
    lw = dict(g_mix=g_mix, conv_w=conv_w, a_gain=a_gain, c_w=c_w, c_scale=c_scale, g_ffn=g_ffn, g_ple=g_ple)
    y_p, (delta_p, conv_p, ret_p, pool_p, win0_p, win1_p, win2_p) = _run_prompt(
        x_prompt, p_prompt, wts, lw, t5_bias, g_final, depth)
    y_s, (delta_s, conv_s, ret_s, pool_s, win0_s, win1_s, win2_s) = _run_sample(
        x_sample, p_sample, (state_delta, state_delta_conv, state_ret, state_pool,
                             (cache_win0, cache_win1, cache_win2)), wts, lw, t5_bias, g_final, depth)
    return (y_p, y_s, delta_p, delta_s, conv_p, conv_s, ret_p, ret_s,
            pool_p, pool_s, win0_p, win0_s, win1_p, win1_s, win2_p, win2_s)
```

```python
import functools
import math

import numpy as np
import jax
import jax.numpy as jnp
from jax import lax
from jax.experimental import pallas as pl
from jax.experimental.pallas import tpu as pltpu

F32 = jnp.float32
BF16 = jnp.bfloat16
HIGHEST = lax.Precision.HIGHEST

D_MODEL = 1024
DEPTH = 4
PAST_LEN = 8192
MIX_W = D_MODEL // 2
N_BRANCH = 4
A_HEADS = 4
A_DK = MIX_W // A_HEADS
A_DV = MIX_W // A_HEADS
A_QKV = A_HEADS * (2 * A_DK + A_DV)
A_CONV = 4
A_CHUNK = 64
B_HEADS = 4
B_DV = MIX_W // B_HEADS
B_DK = B_DV // 2
B_CHUNK = 128
ROPE_BASE = 10000.0
C_GROUPS = 4
C_GW = MIX_W // C_GROUPS
POOL_WINDOWS = (2, 4, 8, 16)
POOL_MAX = 16
D_DILATIONS = (1, 4, 16)
N_DG = 3
D_SPAN = 128
D_HEADS = 4
D_HD = MIX_W // D_HEADS
REL_BUCKETS = 32
REL_MAX_DIST = D_SPAN * 16
D_FF = -(-8 * D_MODEL // (3 * 256)) * 256
PLE_DIM = 256
EPS = 1e-6

LANES = 128
SUBLANES = 8
VMEM_LIMIT_BYTES = 56 * 1024 * 1024

PROJ_W = 12800
COL_AQKV = 0
COL_AZ = 1536
COL_BQ = 2048
COL_BK = 2304
COL_BV = 2560
COL_BG = 3072
COL_CIN = 3584
COL_GATE = 4096
COL_DQKV = 8192
DBLK = D_HEADS * D_HD


def _cparams(sem):
    return pltpu.CompilerParams(dimension_semantics=sem, vmem_limit_bytes=VMEM_LIMIT_BYTES)


def _nt(a, b, precision=None):
    return lax.dot_general(a, b, (((1,), (1,)), ((), ())), precision=precision,
                           preferred_element_type=F32)


def _tn(a, b):
    return lax.dot_general(a, b, (((0,), (0,)), ((), ())), preferred_element_type=F32)


def _mm(a, b, precision=None):
    return jnp.dot(a, b, precision=precision, preferred_element_type=F32)


def _sigmoid(x):
    return 1.0 / (1.0 + jnp.exp(-x))


def _silu(x):
    return x * _sigmoid(x)


def _rms_rows(x, g):
    return x * lax.rsqrt(jnp.mean(x * x, axis=-1, keepdims=True) + EPS) * g


def _norm_matmul_kernel(x_ref, g_ref, w_ref, o_ref, u_ref):
    @pl.when(pl.program_id(1) == 0)
    def _():
        u_ref[...] = _rms_rows(x_ref[...], g_ref[...]).astype(BF16)

    o_ref[...] = _mm(u_ref[...], w_ref[...])


def _norm_matmul(x, g, w, tm, tn):
    n, d = x.shape
    wcols = w.shape[1]
    return pl.pallas_call(
        _norm_matmul_kernel,
        grid=(n // tm, wcols // tn),
        in_specs=[pl.BlockSpec((tm, d), lambda i, j: (i, 0)),
                  pl.BlockSpec((1, d), lambda i, j: (0, 0)),
                  pl.BlockSpec((d, tn), lambda i, j: (0, j))],
        out_specs=pl.BlockSpec((tm, tn), lambda i, j: (i, j)),
        out_shape=jax.ShapeDtypeStruct((n, wcols), F32),
        scratch_shapes=[pltpu.VMEM((tm, d), BF16)],
        compiler_params=_cparams(("parallel", "arbitrary")),
        name="norm_matmul",
    )(x, g.reshape(1, d), w)


def _delta_kernel(qkv_ref, z_ref, ab_ref, cprev_ref, sprev_ref, convw_ref, par_ref, gain_ref,
                  o_ref, snew_ref, ext_ref, s_ref, *, chunk, carry):
    j = pl.program_id(1)
    n_s, l_r, _ = qkv_ref.shape
    t_b = n_s * l_r
    n_c = t_b // chunk
    hdr = SUBLANES
    pre = A_CONV - 1

    @pl.when(j == 0)
    def _():
        ext_ref[:, 0:hdr, :] = jnp.zeros((n_s, hdr, A_QKV), F32)
        ext_ref[:, hdr - pre:hdr, :] = cprev_ref[...]
        if carry:
            s_ref[...] = sprev_ref[0]

    if carry:
        @pl.when(j > 0)
        def _():
            ext_ref[:, 0:hdr, :] = ext_ref[:, l_r:l_r + hdr, :]

    ext_ref[:, hdr:hdr + l_r, :] = qkv_ref[...]
    cw = convw_ref[...]

    def conv_cols(c0):
        acc = None
        for t in range(A_CONV):
            term = (ext_ref[:, hdr - pre + t:hdr - pre + t + l_r, c0:c0 + LANES]
                    * cw[t:t + 1, c0:c0 + LANES])
            acc = term if acc is None else acc + term
        return _silu(acc).reshape(t_b, LANES)

    ab = ab_ref[...].reshape(t_b, LANES)
    par = par_ref[...]
    beta_all = _sigmoid(ab)
    xs = ab + par[1:2, :]
    softplus = jnp.maximum(xs, 0.0) + jnp.log1p(jnp.exp(-jnp.abs(xs)))
    g_all = -jnp.exp(par[0:1, :]) * softplus

    sh = int(math.log2(chunk))
    ri = lax.broadcasted_iota(jnp.int32, (t_b, t_b), 0)
    ci = lax.broadcasted_iota(jnp.int32, (t_b, t_b), 1)
    same = lax.shift_right_logical(ri, sh) == lax.shift_right_logical(ci, sh)
    causal = jnp.logical_and(same, ci <= ri)
    strict = jnp.logical_and(same, ci < ri)
    gc_all = _mm(causal.astype(F32), g_all, precision=HIGHEST)
    sel = (lax.broadcasted_iota(jnp.int32, (SUBLANES, LANES), 0)
           == lax.broadcasted_iota(jnp.int32, (SUBLANES, LANES), 1)).astype(F32)
    gc_rows = _nt(sel, gc_all, precision=HIGHEST)
    glast_all = _mm(same.astype(F32), g_all, precision=HIGHEST)

    gain = gain_ref[...]
    z = z_ref[...].reshape(t_b, MIX_W)
    row_id = lax.broadcasted_iota(jnp.int32, (t_b, 1), 0)
    heads = range(A_HEADS)

    gcc = [gc_all[:, A_HEADS + h:A_HEADS + h + 1] for h in heads]
    gcr = [gc_rows[A_HEADS + h:A_HEADS + h + 1, :] for h in heads]
    glast = [glast_all[:, A_HEADS + h:A_HEADS + h + 1] for h in heads]
    bcol = [beta_all[:, h:h + 1] for h in heads]
    q = [conv_cols(h * A_DK) for h in heads]
    k = [conv_cols(A_HEADS * A_DK + h * A_DK) for h in heads]
    v = [conv_cols(2 * A_HEADS * A_DK + h * A_DV) for h in heads]
    q = [x * lax.rsqrt(jnp.sum(x * x, axis=-1, keepdims=True) + EPS) * (A_DK ** -0.5) for x in q]
    k = [x * lax.rsqrt(jnp.sum(x * x, axis=-1, keepdims=True) + EPS) for x in k]
    kb = [x.astype(BF16) for x in k]
    kk = [_nt(kb[h], kb[h]) for h in heads]
    qkr = [_nt(q[h].astype(BF16), kb[h]) for h in heads]
    gam = [jnp.where(causal, jnp.exp(jnp.where(causal, gcc[h] - gcr[h], 0.0)), 0.0) for h in heads]
    qk = [(qkr[h] * gam[h]).astype(BF16) for h in heads]
    a = [jnp.where(strict, bcol[h] * kk[h] * gam[h], 0.0) for h in heads]
    r = [-x for x in a]
    p = a
    for _ in range(sh - 1):
        pb = [x.astype(BF16) for x in p]
        p = [_mm(x, x) for x in pb]
        pr = [_mm(p[h].astype(BF16), r[h].astype(BF16)) for h in heads]
        r = [r[h] + p[h] + pr[h] for h in heads]
    ecol = [jnp.exp(x) for x in gcc]
    rhs = [jnp.concatenate([v[h] * bcol[h], k[h] * (bcol[h] * ecol[h])], axis=1) for h in heads]
    sol = [rhs[h] + _mm(r[h].astype(BF16), rhs[h].astype(BF16)) for h in heads]
    u = [x[:, :A_DV] for x in sol]
    w = [x[:, A_DV:] for x in sol]
    qe = [q[h] * ecol[h] for h in heads]
    kd = [(k[h] * jnp.exp(glast[h] - gcc[h])).astype(BF16) for h in heads]
    sdec = [jnp.exp(x) for x in glast]
    vnews = [[] for _ in heads]
    qss = [[] for _ in heads]
    if carry:
        s = [s_ref[h] for h in heads]
        for c in range(n_c):
            r0 = c * chunk
            wq = [jnp.concatenate([w[h][r0:r0 + chunk], qe[h][r0:r0 + chunk]], axis=0).astype(BF16)
                  for h in heads]
            t = [_mm(wq[h], s[h].astype(BF16)) for h in heads]
            vnew = [u[h][r0:r0 + chunk] - t[h][:chunk] for h in heads]
            upd = [_tn(kd[h][r0:r0 + chunk], vnew[h].astype(BF16)) for h in heads]
            s = [s[h] * sdec[h][r0:r0 + 1, :] + upd[h] for h in heads]
            for h in heads:
                vnews[h].append(vnew[h])
                qss[h].append(t[h][chunk:])
        vn_all = [jnp.concatenate(x, axis=0) if n_c > 1 else x[0] for x in vnews]
        vnb = [x.astype(BF16) for x in vn_all]
    else:
        olds = [[] for _ in heads]
        for c in range(n_c):
            r0 = c * chunk
            for h in heads:
                s_c = sprev_ref[c, h]
                olds[h].append(s_c)
                wq = jnp.concatenate([w[h][r0:r0 + chunk], qe[h][r0:r0 + chunk]], axis=0).astype(BF16)
                t = _mm(wq, s_c.astype(BF16))
                vnews[h].append(u[h][r0:r0 + chunk] - t[:chunk])
                qss[h].append(t[chunk:])
        vn_all = [jnp.concatenate(x, axis=0) for x in vnews]
        vnb = [x.astype(BF16) for x in vn_all]
        for c in range(n_c):
            r0 = c * chunk
            in_c = jnp.logical_and(row_id >= r0, row_id < r0 + chunk)
            for h in heads:
                upd = _tn(jnp.where(in_c, kd[h], jnp.zeros_like(kd[h])),
                          jnp.where(in_c, vnb[h], jnp.zeros_like(vnb[h])))
                snew_ref[c, h] = olds[h][c] * sdec[h][r0:r0 + 1, :] + upd
    outs = []
    for h in heads:
        qs_all = jnp.concatenate(qss[h], axis=0) if n_c > 1 else qss[h][0]
        o = qs_all + _mm(qk[h], vnb[h])
        o = o * lax.rsqrt(jnp.mean(o * o, axis=-1, keepdims=True) + EPS) * gain
        outs.append(o * _silu(z[:, h * A_DV:(h + 1) * A_DV]))
    o_ref[...] = jnp.concatenate(outs, axis=1).reshape(n_s, l_r, MIX_W)

    if carry:
        for h in heads:
            s_ref[h] = s[h]

        @pl.when(j == pl.num_programs(1) - 1)
        def _():
            for h in heads:
                snew_ref[0, h] = s[h]


def _mixer_delta(proj3, ab3, conv_prev, s_prev, conv_w, par, gain, *, n_s, l_r, chunk, carry):
    bsz, t_len, _ = proj3.shape
    grid = (bsz // n_s, t_len // l_r)
    kern = functools.partial(_delta_kernel, chunk=chunk, carry=carry)
    return pl.pallas_call(
        kern,
        grid=grid,
        in_specs=[
            pl.BlockSpec((n_s, l_r, A_QKV), lambda b, j: (b, j, COL_AQKV // A_QKV)),
            pl.BlockSpec((n_s, l_r, MIX_W), lambda b, j: (b, j, COL_AZ // MIX_W)),
            pl.BlockSpec((n_s, l_r, LANES), lambda b, j: (b, j, 0)),
            pl.BlockSpec((n_s, A_CONV - 1, A_QKV), lambda b, j: (b, 0, 0)),
            pl.BlockSpec((n_s, A_HEADS, A_DK, A_DV), lambda b, j: (b, 0, 0, 0)),
            pl.BlockSpec((A_CONV, A_QKV), lambda b, j: (0, 0)),
            pl.BlockSpec((SUBLANES, LANES), lambda b, j: (0, 0)),
            pl.BlockSpec((1, A_DV), lambda b, j: (0, 0)),
        ],
        out_specs=[
            pl.BlockSpec((n_s, l_r, MIX_W), lambda b, j: (b, j, 0)),
            pl.BlockSpec((n_s, A_HEADS, A_DK, A_DV), lambda b, j: (b, 0, 0, 0)),
        ],
        out_shape=[jax.ShapeDtypeStruct((bsz, t_len, MIX_W), F32),
                   jax.ShapeDtypeStruct((bsz, A_HEADS, A_DK, A_DV), F32)],
        scratch_shapes=[pltpu.VMEM((n_s, SUBLANES + l_r + SUBLANES, A_QKV), F32),
                        pltpu.VMEM((A_HEADS, A_DK, A_DV), F32)],
        compiler_params=_cparams(("parallel", "arbitrary")),
        name="mixer_delta",
    )(proj3, proj3, ab3, conv_prev, s_prev, conv_w, par, gain)


def _ret_log_gamma(h):
    return math.log1p(-(2.0 ** (-5.0 - h)))


def _ret_kernel(q_ref, k_ref, v_ref, g_ref, cos_ref, sin_ref, rprev_ref,
                o_ref, rnew_ref, r_ref, *, chunk, carry):
    j = pl.program_id(1)
    n_s, l_r, _ = q_ref.shape
    t_b = n_s * l_r
    n_c = t_b // chunk
    half = B_DK // 2

    if carry:
        @pl.when(j == 0)
        def _():
            r_ref[...] = rprev_ref[0]

    cos = cos_ref[...]
    sin = sin_ref[...]
    lane = lax.broadcasted_iota(jnp.int32, (1, LANES), 1)
    first_half = (lane % B_DK) < half
    lane_head = lane // B_DK

    def rope(x):
        swapped = jnp.where(first_half, pltpu.roll(x, LANES - half, axis=1),
                            pltpu.roll(x, half, axis=1))
        return x * cos + swapped * sin

    idx_i = lax.broadcasted_iota(jnp.int32, (chunk, chunk), 0)
    idx_j = lax.broadcasted_iota(jnp.int32, (chunk, chunk), 1)
    pos_col = lax.broadcasted_iota(jnp.int32, (chunk, 1), 0).astype(F32)
    row_head = lax.broadcasted_iota(jnp.int32, (LANES, 1), 0) // B_DK
    gate = g_ref[...].reshape(t_b, MIX_W)
    vv = v_ref[...].reshape(t_b, MIX_W)

    small = chunk < 2 * SUBLANES
    if small:
        sh = int(math.log2(chunk))
        bi = lax.broadcasted_iota(jnp.int32, (t_b, t_b), 0)
        bj = lax.broadcasted_iota(jnp.int32, (t_b, t_b), 1)
        bsame = lax.shift_right_logical(bi, sh) == lax.shift_right_logical(bj, sh)
        bcausal = jnp.logical_and(bsame, bj <= bi)
        bdiff = (bi - bj).astype(F32)
        row_id = lax.broadcasted_iota(jnp.int32, (t_b, 1), 0)
        pos_in = (row_id & (chunk - 1)).astype(F32)

    outs = [None] * B_HEADS
    for p in range(B_HEADS // 2):
        q2 = rope(q_ref[...].reshape(t_b, B_HEADS * B_DK)[:, p * LANES:(p + 1) * LANES])
        k2 = rope(k_ref[...].reshape(t_b, B_HEADS * B_DK)[:, p * LANES:(p + 1) * LANES]) * (B_DK ** -0.5)
        lgs = [_ret_log_gamma(2 * p + hh) for hh in range(2)]
        qh = [jnp.where(lane_head == hh, q2, 0.0) for hh in range(2)]
        kh = [jnp.where(lane_head == hh, k2, 0.0) for hh in range(2)]
        vh = [vv[:, (2 * p + hh) * B_DV:(2 * p + hh + 1) * B_DV] for hh in range(2)]
        row_scale = jnp.where(row_head == 0, math.exp(lgs[0] * chunk), math.exp(lgs[1] * chunk))
        if small:
            intra = []
            for hh in range(2):
                dec = jnp.where(bcausal, jnp.exp(lgs[hh] * jnp.where(bcausal, bdiff, 0.0)), 0.0)
                sc = _nt(qh[hh].astype(BF16), kh[hh].astype(BF16)) * dec
                intra.append(_mm(sc.astype(BF16), vh[hh].astype(BF16)))
            cross = [[], []]
            for c in range(n_c):
                r0 = c * chunk
                rp = r_ref[p] if carry else rprev_ref[c, p]
                qq = jnp.concatenate([qh[0][r0:r0 + chunk], qh[1][r0:r0 + chunk]], axis=0)
                t = _mm(qq.astype(BF16), rp.astype(BF16))
                cross[0].append(t[:chunk])
                cross[1].append(t[chunk:])
                in_c = jnp.logical_and(row_id >= r0, row_id < r0 + chunk)
                upd = None
                for hh in range(2):
                    kd = jnp.where(in_c, kh[hh] * jnp.exp(lgs[hh] * (chunk - 1.0 - pos_in)), 0.0)
                    term = _tn(kd.astype(BF16), vh[hh].astype(BF16))
                    upd = term if upd is None else upd + term
                rn = rp * row_scale + upd
                if carry:
                    r_ref[p] = rn
                else:
                    rnew_ref[c, p] = rn
            for hh in range(2):
                cr = jnp.concatenate(cross[hh], axis=0) * jnp.exp(lgs[hh] * (pos_in + 1.0))
                outs[2 * p + hh] = intra[hh] + cr
        else:
            pieces = [[], []]
            for c in range(n_c):
                r0 = c * chunk
                rp = r_ref[p] if carry else rprev_ref[c, p]
                rpb = rp.astype(BF16)
                upd = None
                for hh in range(2):
                    qc = qh[hh][r0:r0 + chunk].astype(BF16)
                    kc = kh[hh][r0:r0 + chunk]
                    vc = vh[hh][r0:r0 + chunk].astype(BF16)
                    causal = idx_j <= idx_i
                    dec = jnp.where(causal, jnp.exp(lgs[hh] * jnp.where(causal, (idx_i - idx_j).astype(F32), 0.0)), 0.0)
                    sc = _nt(qc, kc.astype(BF16)) * dec
                    oc = _mm(sc.astype(BF16), vc) + _mm(qc, rpb) * jnp.exp(lgs[hh] * (pos_col + 1.0))
                    pieces[hh].append(oc)
                    kd = kc * jnp.exp(lgs[hh] * (chunk - 1.0 - pos_col))
                    term = _tn(kd.astype(BF16), vc)
                    upd = term if upd is None else upd + term
                rn = rp * row_scale + upd
                if carry:
                    r_ref[p] = rn
                else:
                    rnew_ref[c, p] = rn
            for hh in range(2):
                outs[2 * p + hh] = jnp.concatenate(pieces[hh], axis=0) if n_c > 1 else pieces[hh][0]

    res = []
    for h in range(B_HEADS):
        o = outs[h]
        o = o * lax.rsqrt(jnp.mean(o * o, axis=-1, keepdims=True) + EPS)
        res.append(o * _silu(gate[:, h * B_DV:(h + 1) * B_DV]))
    o_ref[...] = jnp.concatenate(res, axis=1).reshape(n_s, l_r, MIX_W)

    if carry:
        @pl.when(j == pl.num_programs(1) - 1)
        def _():
            rnew_ref[0] = r_ref[...]


def _mixer_ret(proj3, cos_t, sin_t, r_prev, *, n_s, l_r, chunk, carry):
    bsz, t_len, _ = proj3.shape
    t_b = n_s * l_r
    grid = (bsz // n_s, t_len // l_r)
    n_tab = cos_t.shape[0] // t_b
    kern = functools.partial(_ret_kernel, chunk=chunk, carry=carry)
    qw = B_HEADS * B_DK
    return pl.pallas_call(
        kern,
        grid=grid,
        in_specs=[
            pl.BlockSpec((n_s, l_r, qw), lambda b, j: (b, j, COL_BQ // qw)),
            pl.BlockSpec((n_s, l_r, qw), lambda b, j: (b, j, COL_BK // qw)),
            pl.BlockSpec((n_s, l_r, MIX_W), lambda b, j: (b, j, COL_BV // MIX_W)),
            pl.BlockSpec((n_s, l_r, MIX_W), lambda b, j: (b, j, COL_BG // MIX_W)),
            pl.BlockSpec((t_b, LANES), lambda b, j: (j % n_tab, 0)),
            pl.BlockSpec((t_b, LANES), lambda b, j: (j % n_tab, 0)),
            pl.BlockSpec((n_s, B_HEADS // 2, LANES, B_DV), lambda b, j: (b, 0, 0, 0)),
        ],
        out_specs=[
            pl.BlockSpec((n_s, l_r, MIX_W), lambda b, j: (b, j, 0)),
            pl.BlockSpec((n_s, B_HEADS // 2, LANES, B_DV), lambda b, j: (b, 0, 0, 0)),
        ],
        out_shape=[jax.ShapeDtypeStruct((bsz, t_len, MIX_W), F32),
                   jax.ShapeDtypeStruct((bsz, B_HEADS // 2, LANES, B_DV), F32)],
        scratch_shapes=[pltpu.VMEM((B_HEADS // 2, LANES, B_DV), F32)],
        compiler_params=_cparams(("parallel", "arbitrary")),
        name="mixer_ret",
    )(proj3, proj3, proj3, proj3, cos_t, sin_t, r_prev)


def _rope_tables(pos, reps):
    half = B_DK // 2
    inv = ROPE_BASE ** (-jnp.arange(half, dtype=F32) / half)
    ang = pos.astype(F32)[:, None] * inv[None, :]
    cos = jnp.cos(ang)
    sin = jnp.sin(ang)
    cos_t = jnp.concatenate([cos, cos, cos, cos], axis=1)
    sin_t = jnp.concatenate([-sin, sin, -sin, sin], axis=1)
    return jnp.tile(cos_t, (reps, 1)), jnp.tile(sin_t, (reps, 1))


def _pool_kernel(x_ref, prev_ref, cw_ref, cs_ref, o_ref, ext_ref, *, pos0):
    j = pl.program_id(1)
    n_s, l_r, _ = x_ref.shape
    t_b = n_s * l_r
    hdr = POOL_MAX

    @pl.when(j == 0)
    def _():
        ext_ref[:, 0:hdr, :] = jnp.zeros((n_s, hdr, MIX_W), F32)
        ext_ref[:, 1:hdr, :] = prev_ref[...]

    @pl.when(j > 0)
    def _():
        ext_ref[:, 0:hdr, :] = ext_ref[:, l_r:l_r + hdr, :]

    ext_ref[:, hdr:hdr + l_r, :] = x_ref[...]
    pos = pos0 + j * l_r + lax.broadcasted_iota(jnp.int32, (1, l_r, 1), 1)
    outs = []
    for gi, w in enumerate(POOL_WINDOWS):
        c0 = gi * C_GW
        acc = None
        for t in range(w):
            term = ext_ref[:, hdr - t:hdr - t + l_r, c0:c0 + C_GW]
            acc = term if acc is None else acc + term
        cnt = jnp.minimum(pos + 1, w).astype(F32)
        y = acc / cnt - ext_ref[:, hdr:hdr + l_r, c0:c0 + C_GW]
        y = _mm(y.reshape(t_b, C_GW).astype(BF16), cw_ref[gi])
        outs.append(y)
    o_ref[...] = (jnp.concatenate(outs, axis=1) * cs_ref[...]).reshape(n_s, l_r, MIX_W)


def _mixer_pool(proj3, prev, c_w, c_scale, *, n_s, l_r, pos0):
    bsz, t_len, _ = proj3.shape
    grid = (bsz // n_s, t_len // l_r)
    kern = functools.partial(_pool_kernel, pos0=pos0)
    return pl.pallas_call(
        kern,
        grid=grid,
        in_specs=[
            pl.BlockSpec((n_s, l_r, MIX_W), lambda b, j: (b, j, COL_CIN // MIX_W)),
            pl.BlockSpec((n_s, POOL_MAX - 1, MIX_W), lambda b, j: (b, 0, 0)),
            pl.BlockSpec((C_GROUPS, C_GW, C_GW), lambda b, j: (0, 0, 0)),
            pl.BlockSpec((1, MIX_W), lambda b, j: (0, 0)),
        ],
        out_specs=pl.BlockSpec((n_s, l_r, MIX_W), lambda b, j: (b, j, 0)),
        out_shape=jax.ShapeDtypeStruct((bsz, t_len, MIX_W), F32),
        scratch_shapes=[pltpu.VMEM((n_s, POOL_MAX + l_r + SUBLANES, MIX_W), F32)],
        compiler_params=_cparams(("parallel", "arbitrary")),
        name="mixer_pool",
    )(proj3, prev, c_w, c_scale)


def _lookup_kernel(oh_ref, tab_ref, o_ref):
    o_ref[...] = _mm(oh_ref[...], tab_ref[...], precision=HIGHEST)


def _bias_lookup(idx, table):
    n_valid = idx.shape[0]
    tr = 4096
    n_rows = -(-n_valid // tr) * tr
    idx = np.concatenate([idx, np.zeros((n_rows - n_valid,), idx.dtype)])
    onehot = jnp.asarray(idx[:, None] == np.arange(REL_BUCKETS)[None, :], dtype=F32)
    tab = jnp.pad(table, ((0, 0), (0, LANES - table.shape[1])))
    out = pl.pallas_call(
        _lookup_kernel,
        grid=(n_rows // tr,),
        in_specs=[pl.BlockSpec((tr, REL_BUCKETS), lambda i: (i, 0)),
                  pl.BlockSpec((REL_BUCKETS, LANES), lambda i: (0, 0))],
        out_specs=pl.BlockSpec((tr, LANES), lambda i: (i, 0)),
        out_shape=jax.ShapeDtypeStruct((n_rows, LANES), F32),
        compiler_params=_cparams(("parallel",)),
        name="bias_lookup",
    )(onehot, tab)
    return out[:n_valid, :table.shape[1]]


def _t5_bucket_np(dist):
    exact = REL_BUCKETS // 2
    n = np.maximum(dist, 0)
    nf = np.maximum(n, 1).astype(np.float32)
    large = exact + (np.log(nf / np.float32(exact)) / np.float32(math.log(REL_MAX_DIST / exact))
                     * np.float32(REL_BUCKETS - exact)).astype(np.int32)
    large = np.minimum(large, REL_BUCKETS - 1)
    return np.where(n < exact, n, large).astype(np.int32)


SUPER = D_SPAN * D_DILATIONS[-1]
TILES = SUPER // D_SPAN


def _band_fused_kernel(*refs):
    ins = refs[:5 * N_DG]
    bias_ref = refs[5 * N_DG]
    o_ref = refs[5 * N_DG + 1]
    scr = refs[5 * N_DG + 2:]
    kf = scr[0:2 * N_DG:2]
    vf = scr[1:2 * N_DG:2]
    og = scr[2 * N_DG:3 * N_DG]
    lg = scr[3 * N_DG:4 * N_DG]
    has_prev = pl.program_id(2) > 0
    ii = lax.broadcasted_iota(jnp.int32, (D_SPAN, 2 * D_SPAN), 0)
    jj = lax.broadcasted_iota(jnp.int32, (D_SPAN, 2 * D_SPAN), 1)
    band = jnp.logical_and(jj >= ii, jj <= ii + D_SPAN)
    in_cur = jj >= D_SPAN
    scale = D_HD ** -0.5
    for g, dil in enumerate(D_DILATIONS):
        q_ref, k_ref, v_ref, kp_ref, vp_ref = ins[5 * g:5 * g + 5]
        span = D_SPAN * dil
        kf[g][0:span, :] = kp_ref[...]
        kf[g][span:span + SUPER, :] = k_ref[...]
        vf[g][0:span, :] = vp_ref[...]
        vf[g][span:span + SUPER, :] = v_ref[...]
        bias = bias_ref[g]
        sh = int(math.log2(dil))

        def tile(idx, carry, g=g, dil=dil, span=span, sh=sh, q_ref=q_ref, bias=bias):
            qi = lax.shift_right_logical(idx, sh)
            res = idx & (dil - 1)
            start = qi * span + res
            if dil == 1:
                start = pl.multiple_of(start, D_SPAN)
                rows = pl.ds(start, D_SPAN)
                krows = pl.ds(start, 2 * D_SPAN)
            else:
                rows = pl.ds(start, D_SPAN, stride=dil)
                krows = pl.ds(start, 2 * D_SPAN, stride=dil)
            q = q_ref[rows, :].astype(BF16)
            kk = kf[g][krows, :].astype(BF16)
            vv = vf[g][krows, :].astype(BF16)
            s = _nt(q, kk) * scale + bias
            valid = jnp.logical_and(band, jnp.logical_or(in_cur, jnp.logical_or(has_prev, qi > 0)))
            s = jnp.where(valid, s, -jnp.inf)
            m = jnp.max(s, axis=-1, keepdims=True)
            p = jnp.exp(s - m)
            l = jnp.sum(p, axis=-1, keepdims=True)
            og[g][rows, :] = _mm(p.astype(BF16), vv) / l
            lg[g][rows, :] = jnp.broadcast_to(m + jnp.log(l), (D_SPAN, D_HD))
            return carry

        lax.fori_loop(0, TILES, tile, 0, unroll=8)

    def merge(c, carry):
        rows = pl.ds(pl.multiple_of(c * D_SPAN, D_SPAN), D_SPAN)
        l0, l1, l2 = lg[0][rows, :], lg[1][rows, :], lg[2][rows, :]
        m = jnp.maximum(jnp.maximum(l0, l1), l2)
        e0, e1, e2 = jnp.exp(l0 - m), jnp.exp(l1 - m), jnp.exp(l2 - m)
        inv = 1.0 / (e0 + e1 + e2)
        o_ref[rows, :] = (e0 * og[0][rows, :] + e1 * og[1][rows, :] + e2 * og[2][rows, :]) * inv
        return carry

    lax.fori_loop(0, TILES, merge, 0)


def _dilated_band(proj3, band_bias):
    bsz, s_len, _ = proj3.shape
    n_blk = s_len // SUPER
    base = COL_DQKV // D_HD
    in_specs = []
    args = []
    for g, dil in enumerate(D_DILATIONS):
        span = D_SPAN * dil
        per = SUPER // span

        def col(c, g=g):
            return lambda b, h, n: (b, n, base + (c * N_DG + g) * D_HEADS + h)

        def col_prev(c, g=g, per=per):
            return lambda b, h, n: (b, jnp.maximum(n * per - 1, 0), base + (c * N_DG + g) * D_HEADS + h)

        in_specs += [pl.BlockSpec((None, SUPER, D_HD), col(0)),
                     pl.BlockSpec((None, SUPER, D_HD), col(1)),
                     pl.BlockSpec((None, SUPER, D_HD), col(2)),
                     pl.BlockSpec((None, span, D_HD), col_prev(1)),
                     pl.BlockSpec((None, span, D_HD), col_prev(2))]
        args += [proj3] * 5
    in_specs.append(pl.BlockSpec((N_DG, None, D_SPAN, 2 * D_SPAN), lambda b, h, n: (0, h, 0, 0)))
    scratch = []
    for dil in D_DILATIONS:
        scratch += [pltpu.VMEM((D_SPAN * dil + SUPER, D_HD), F32)] * 2
    scratch += [pltpu.VMEM((SUPER, D_HD), F32)] * (2 * N_DG)
    return pl.pallas_call(
        _band_fused_kernel,
        grid=(bsz, D_HEADS, n_blk),
        in_specs=in_specs,
        out_specs=pl.BlockSpec((None, SUPER, D_HD), lambda b, h, n: (b, n, h)),
        out_shape=jax.ShapeDtypeStruct((bsz, s_len, MIX_W), F32),
        scratch_shapes=scratch,
        compiler_params=_cparams(("parallel", "parallel", "arbitrary")),
        name="dilated_band",
    )(*args, band_bias)


KV_ROWS = 2 * D_HEADS


def _step_fused_kernel(q_ref, kn_ref, vn_ref, cache_ref, nxt_ref, bias_c_ref, mask_c_ref, bias_n_ref,
                       mask_n_ref, buf_ref, o_ref, lse_ref, out_ref, padk_ref, padv_ref, m_ref, l_ref, acc_ref):
    del buf_ref
    c = pl.program_id(1)
    n_ch = pl.num_programs(1)
    t_new = q_ref.shape[0]
    flat = cache_ref.shape[0]
    n_pos = flat // KV_ROWS
    shift = t_new * KV_ROWS
    scale = D_HD ** -0.5
    zeros_q = jnp.zeros((SUBLANES, D_HD), F32)

    out_ref[0:flat - shift, :] = cache_ref[shift:flat, :]

    @pl.when(c < n_ch - 1)
    def _():
        out_ref[flat - shift:flat, :] = nxt_ref[...]

    @pl.when(c == n_ch - 1)
    def _():
        for h in range(D_HEADS):
            cs = slice(h * D_HD, (h + 1) * D_HD)
            out_ref[pl.ds(flat - shift + h, t_new, stride=KV_ROWS), :] = kn_ref[:, cs]
            out_ref[pl.ds(flat - shift + D_HEADS + h, t_new, stride=KV_ROWS), :] = vn_ref[:, cs]

    heads = range(D_HEADS)
    cols = [slice(h * D_HD, (h + 1) * D_HD) for h in heads]
    rws = [slice(h * t_new, (h + 1) * t_new) for h in heads]
    lhs = [jnp.concatenate([q_ref[:, cols[h]], zeros_q], axis=0).astype(BF16) for h in heads]

    def pad16(p):
        return jnp.concatenate([p, jnp.zeros_like(p)], axis=0).astype(BF16)

    @pl.when(c == 0)
    def _():
        padk_ref[...] = jnp.zeros(padk_ref.shape, F32)
        padv_ref[...] = jnp.zeros(padv_ref.shape, F32)
        for h in heads:
            padk_ref[h, 0:t_new, :] = kn_ref[:, cols[h]]
            padv_ref[h, 0:t_new, :] = vn_ref[:, cols[h]]
        s = [_nt(lhs[h], padk_ref[h].astype(BF16))[0:t_new] * scale + bias_n_ref[h] + mask_n_ref[...]
             for h in heads]
        m = [jnp.max(x, axis=-1, keepdims=True) for x in s]
        p = [jnp.exp(s[h] - m[h]) for h in heads]
        pv = [_mm(pad16(p[h]), padv_ref[h].astype(BF16))[0:t_new] for h in heads]
        for h in heads:
            m_ref[rws[h], :] = m[h]
            l_ref[rws[h], :] = jnp.sum(p[h], axis=-1, keepdims=True)
            acc_ref[rws[h], :] = pv[h]

    kh = [cache_ref[pl.ds(h, n_pos, stride=KV_ROWS), :].astype(BF16) for h in heads]
    vh = [cache_ref[pl.ds(D_HEADS + h, n_pos, stride=KV_ROWS), :].astype(BF16) for h in heads]
    s = [_nt(lhs[h], kh[h])[0:t_new] * scale + bias_c_ref[h] + mask_c_ref[...] for h in heads]
    m_old = [m_ref[rws[h], :] for h in heads]
    m_new = [jnp.maximum(m_old[h], jnp.max(s[h], axis=-1, keepdims=True)) for h in heads]
    alpha = [jnp.exp(m_old[h] - m_new[h]) for h in heads]
    p = [jnp.exp(s[h] - m_new[h]) for h in heads]
    pv = [_mm(pad16(p[h]), vh[h])[0:t_new] for h in heads]
    l_new = [alpha[h] * l_ref[rws[h], :] + jnp.sum(p[h], axis=-1, keepdims=True) for h in heads]
    acc_new = [alpha[h] * acc_ref[rws[h], :] + pv[h] for h in heads]
    for h in heads:
        l_ref[rws[h], :] = l_new[h]
        acc_ref[rws[h], :] = acc_new[h]
        m_ref[rws[h], :] = m_new[h]

    @pl.when(c == n_ch - 1)
    def _():
        for h in heads:
            o_ref[:, cols[h]] = acc_new[h] / l_new[h]
            lse_ref[:, h:h + 1] = m_new[h] + jnp.log(l_new[h])


def _dilated_step(proj3, cache_flat, out_buf, layer, gi, bias_c, mask_c, bias_n, mask_n):
    bsz, t_new, _ = proj3.shape
    l_buf = cache_flat.shape[1] // KV_ROWS
    n_pos = min(l_buf, 1024)
    n_ch = l_buf // n_pos
    flat = n_pos * KV_ROWS
    shift = t_new * KV_ROWS
    per = flat // shift
    last = l_buf * KV_ROWS // shift - 1
    base = COL_DQKV // DBLK
    cq, ck, cv = base + gi, base + N_DG + gi, base + 2 * N_DG + gi
    n_hq = D_HEADS * t_new
    return pl.pallas_call(
        _step_fused_kernel,
        grid=(bsz, n_ch),
        in_specs=[
            pl.BlockSpec((None, t_new, DBLK), lambda b, c: (b, 0, cq)),
            pl.BlockSpec((None, t_new, DBLK), lambda b, c: (b, 0, ck)),
            pl.BlockSpec((None, t_new, DBLK), lambda b, c: (b, 0, cv)),
            pl.BlockSpec((None, flat, D_HD), lambda b, c: (layer * bsz + b, c, 0)),
            pl.BlockSpec((None, shift, D_HD), lambda b, c: (layer * bsz + b, jnp.minimum((c + 1) * per, last), 0)),
            pl.BlockSpec((D_HEADS, t_new, n_pos), lambda b, c: (0, 0, c)),
            pl.BlockSpec((t_new, n_pos), lambda b, c: (0, c)),
            pl.BlockSpec((D_HEADS, t_new, LANES), lambda b, c: (0, 0, 0)),
            pl.BlockSpec((t_new, LANES), lambda b, c: (0, 0)),
            pl.BlockSpec(memory_space=pl.ANY),
        ],
        out_specs=[
            pl.BlockSpec((None, t_new, DBLK), lambda b, c: (b, 0, 0)),
            pl.BlockSpec((None, t_new, D_HEADS), lambda b, c: (b, 0, 0)),
            pl.BlockSpec((None, flat, D_HD), lambda b, c: (layer * bsz + b, c, 0)),
        ],
        out_shape=[jax.ShapeDtypeStruct((bsz, t_new, DBLK), F32),
                   jax.ShapeDtypeStruct((bsz, t_new, D_HEADS), F32),
                   jax.ShapeDtypeStruct(out_buf.shape, F32)],
        scratch_shapes=[pltpu.VMEM((D_HEADS, LANES, D_HD), F32),
                        pltpu.VMEM((D_HEADS, LANES, D_HD), F32),
                        pltpu.VMEM((n_hq, 1), F32),
                        pltpu.VMEM((n_hq, 1), F32),
                        pltpu.VMEM((n_hq, D_HD), F32)],
        input_output_aliases={9: 2},
        compiler_params=_cparams(("parallel", "arbitrary")),
        name="dilated_step",
    )(proj3, proj3, proj3, cache_flat, cache_flat, bias_c, mask_c, bias_n, mask_n, out_buf)


def _merge_kernel(*refs, split_d):
    if split_d:
        (gate_ref, ya_ref, yb_ref, yc_ref, o0_ref, o1_ref, o2_ref, l0_ref, l1_ref, l2_ref,
         wbr_ref, wo_ref, h_ref, out_ref) = refs
        l0, l1, l2 = l0_ref[...], l1_ref[...], l2_ref[...]
        m = jnp.maximum(jnp.maximum(l0, l1), l2)
        e0, e1, e2 = jnp.exp(l0 - m), jnp.exp(l1 - m), jnp.exp(l2 - m)
        inv = 1.0 / (e0 + e1 + e2)
        w0, w1, w2 = e0 * inv, e1 * inv, e2 * inv
        yd = []
        for h in range(D_HEADS):
            cs = slice(h * D_HD, (h + 1) * D_HD)
            yd.append(w0[:, h:h + 1] * o0_ref[:, cs] + w1[:, h:h + 1] * o1_ref[:, cs]
                      + w2[:, h:h + 1] * o2_ref[:, cs])
        y_d = jnp.concatenate(yd, axis=1)
    else:
        gate_ref, ya_ref, yb_ref, yc_ref, yd_ref, wbr_ref, wo_ref, h_ref, out_ref = refs
        y_d = yd_ref[...]
    branches = [ya_ref[...], yb_ref[...], yc_ref[...], y_d]
    merged = None
    for nbr in range(N_BRANCH):
        up = _mm(branches[nbr].astype(BF16), wbr_ref[nbr])
        term = _sigmoid(gate_ref[:, nbr * D_MODEL:(nbr + 1) * D_MODEL]) * up
        merged = term if merged is None else merged + term
    out_ref[...] = h_ref[...] + _mm(merged.astype(BF16), wo_ref[...])


def _merge(proj2, ya, yb, yc, yd, w_br, w_o, h2, tm):
    n = h2.shape[0]
    row = lambda width: pl.BlockSpec((tm, width), lambda i: (i, 0))
    split_d = isinstance(yd, tuple)
    if split_d:
        d_args = list(yd[0]) + list(yd[1])
        d_specs = [row(MIX_W)] * N_DG + [row(D_HEADS)] * N_DG
    else:
        d_args = [yd]
        d_specs = [row(MIX_W)]
    return pl.pallas_call(
        functools.partial(_merge_kernel, split_d=split_d),
        grid=(n // tm,),
        in_specs=[pl.BlockSpec((tm, N_BRANCH * D_MODEL), lambda i: (i, COL_GATE // (N_BRANCH * D_MODEL))),
                  row(MIX_W), row(MIX_W), row(MIX_W)] + d_specs + [
                  pl.BlockSpec((N_BRANCH, MIX_W, D_MODEL), lambda i: (0, 0, 0)),
                  pl.BlockSpec((D_MODEL, D_MODEL), lambda i: (0, 0)),
                  row(D_MODEL)],
        out_specs=row(D_MODEL),
        out_shape=jax.ShapeDtypeStruct((n, D_MODEL), F32),
        compiler_params=_cparams(("parallel",)),
        name="branch_merge",
    )(proj2, ya, yb, yc, *d_args, w_br, w_o, h2)


def _ffn_kernel(h_ref, gffn_ref, wg_ref, wu_ref, wd_ref, p_ref, gple_ref, wpg_ref, wple_ref, gfin_ref,
                out_ref, f_ref, acc_ref, *, final):
    k = pl.program_id(1)

    @pl.when(k == 0)
    def _():
        f_ref[...] = _rms_rows(h_ref[...], gffn_ref[...]).astype(BF16)
        acc_ref[...] = jnp.zeros(acc_ref.shape, F32)

    f = f_ref[...]
    a = _mm(f, wg_ref[...])
    b = _mm(f, wu_ref[...])
    acc_ref[...] += _mm((_silu(a) * b).astype(BF16), wd_ref[...])

    @pl.when(k == pl.num_programs(1) - 1)
    def _():
        h2 = h_ref[...] + acc_ref[...]
        e = _rms_rows(h2, gple_ref[...]).astype(BF16)
        gate = _sigmoid(_mm(e, wpg_ref[...]))
        h3 = h2 + gate * _mm(p_ref[...].astype(BF16), wple_ref[...])
        if final:
            out_ref[...] = _rms_rows(h3, gfin_ref[...])
        else:
            out_ref[...] = h3


def _ffn(h2, g_ffn, w_gate, w_up, w_down, p2, g_ple, w_pg, w_ple, g_fin, tm, tf, final):
    n = h2.shape[0]
    vec = lambda: pl.BlockSpec((1, D_MODEL), lambda i, k: (0, 0))
    return pl.pallas_call(
        functools.partial(_ffn_kernel, final=final),
        grid=(n // tm, D_FF // tf),
        in_specs=[pl.BlockSpec((tm, D_MODEL), lambda i, k: (i, 0)),
                  vec(),
                  pl.BlockSpec((D_MODEL, tf), lambda i, k: (0, k)),
                  pl.BlockSpec((D_MODEL, tf), lambda i, k: (0, k)),
                  pl.BlockSpec((tf, D_MODEL), lambda i, k: (k, 0)),
                  pl.BlockSpec((tm, PLE_DIM), lambda i, k: (i, 0)),
                  vec(),
                  pl.BlockSpec((D_MODEL, D_MODEL), lambda i, k: (0, 0)),
                  pl.BlockSpec((PLE_DIM, D_MODEL), lambda i, k: (0, 0)),
                  vec()],
        out_specs=pl.BlockSpec((tm, D_MODEL), lambda i, k: (i, 0)),
        out_shape=jax.ShapeDtypeStruct((n, D_MODEL), F32),
        scratch_shapes=[pltpu.VMEM((tm, D_MODEL), BF16), pltpu.VMEM((tm, D_MODEL), F32)],
        compiler_params=_cparams(("parallel", "arbitrary")),
        name="ffn_ple",
    )(h2, g_ffn.reshape(1, -1), w_gate, w_up, w_down, p2, g_ple.reshape(1, -1), w_pg, w_ple,
      g_fin.reshape(1, -1))


def _prep_weights(w_in, w_br, w_o, w_gate, w_up, w_down, w_ple_gate, w_ple, a_log, dt_bias):
    o_ab = A_QKV + MIX_W
    o_bq = o_ab + 2 * A_HEADS
    o_d = o_bq + 2 * B_HEADS * B_DK + 3 * MIX_W
    o_g = o_d + 3 * N_DG * MIX_W
    w_main = jnp.concatenate([w_in[:, :, :o_ab], w_in[:, :, o_bq:o_d], w_in[:, :, o_g:],
                              w_in[:, :, o_d:o_g]], axis=-1).astype(BF16)
    w_ab = jnp.pad(w_in[:, :, o_ab:o_bq], ((0, 0), (0, 0), (0, LANES - 2 * A_HEADS))).astype(BF16)
    par = jnp.zeros((a_log.shape[0], SUBLANES, LANES), F32)
    par = par.at[:, 0, A_HEADS:2 * A_HEADS].set(a_log)
    par = par.at[:, 1, A_HEADS:2 * A_HEADS].set(dt_bias)
    return dict(w_main=w_main, w_ab=w_ab, par=par, w_br=w_br.astype(BF16), w_o=w_o.astype(BF16),
                w_gate=w_gate.astype(BF16), w_up=w_up.astype(BF16), w_down=w_down.astype(BF16),
                w_pg=w_ple_gate.astype(BF16), w_ple=w_ple.astype(BF16))


def _band_bias(t5_bias):
    i = np.arange(D_SPAN)[:, None]
    j = np.arange(2 * D_SPAN)[None, :]
    rel = i + D_SPAN - j
    out = []
    for gi, dil in enumerate(D_DILATIONS):
        idx = _t5_bucket_np(rel * dil).reshape(-1)
        tab = _bias_lookup(idx, t5_bias[:, gi * D_HEADS:(gi + 1) * D_HEADS])
        out.append(tab.reshape(D_SPAN, 2 * D_SPAN, D_HEADS).transpose(2, 0, 1))
    return jnp.stack(out)


def _step_bias(t5_bias, l_buf, t_new, gi, dil):
    t = np.arange(t_new)[:, None]
    j = np.arange(l_buf + LANES)[None, :]
    dist = l_buf + t - j
    valid = (dist >= 0) & (dist % dil == 0) & (dist <= D_SPAN * dil) & (j < l_buf + t_new)
    idx = _t5_bucket_np(np.where(valid, dist, 0)).reshape(-1)
    tab = _bias_lookup(idx, t5_bias[:, gi * D_HEADS:(gi + 1) * D_HEADS])
    tab = tab.reshape(t_new, l_buf + LANES, D_HEADS).transpose(2, 0, 1)
    tab = jnp.where(valid[None], tab, 0.0)
    mask = jnp.asarray(np.where(valid, 0.0, -np.inf), dtype=F32)
    return tab[:, :, :l_buf], mask[:, :l_buf], tab[:, :, l_buf:], mask[:, l_buf:]


def _layer_common(h2, p2, bsz, t_len, wts, lw, i, mix_fn, tm, final, g_final):
    n = bsz * t_len
    proj2 = _norm_matmul(h2, lw['g_mix'][i], wts['w_main'][i], tm, 2560)
    ab2 = _norm_matmul(h2, lw['g_mix'][i], wts['w_ab'][i], tm, LANES)
    proj3 = proj2.reshape(bsz, t_len, PROJ_W)
    ab3 = ab2.reshape(bsz, t_len, LANES)
    ya, yb, yc, yd, states = mix_fn(proj3, ab3)
    flat = lambda x: x.reshape(n, x.shape[-1])
    if isinstance(yd, tuple):
        yd = ([flat(x) for x in yd[0]], [flat(x) for x in yd[1]])
    else:
        yd = flat(yd)
    tm2 = min(tm, 256)
    h2 = _merge(proj2, flat(ya), flat(yb), flat(yc), yd, wts['w_br'][i], wts['w_o'][i], h2, tm2)
    tm3 = min(tm, 512)
    h2 = _ffn(h2, lw['g_ffn'][i], wts['w_gate'][i], wts['w_up'][i], wts['w_down'][i], p2,
              lw['g_ple'][i], wts['w_pg'][i], wts['w_ple'][i], g_final, tm3, D_FF // 2, final)
    return h2, proj3, states


def _run_prompt(x, p, wts, lw, t5_bias, g_final, depth):
    bsz, t_len, _ = x.shape
    n = bsz * t_len
    h2 = x.reshape(n, D_MODEL)
    pos = jnp.arange(t_len, dtype=jnp.int32)
    cos_t, sin_t = _rope_tables(pos, 1)
    band_bias = _band_bias(t5_bias)
    zero_s = jnp.zeros((bsz, A_HEADS, A_DK, A_DV), F32)
    zero_c = jnp.zeros((bsz, A_CONV - 1, A_QKV), F32)
    zero_r = jnp.zeros((bsz, B_HEADS // 2, LANES, B_DV), F32)
    zero_p = jnp.zeros((bsz, POOL_MAX - 1, MIX_W), F32)
    per_layer = []
    for i in range(depth):
        def mix_fn(proj3, ab3, i=i):
            ya, s_new = _mixer_delta(proj3, ab3, zero_c, zero_s, lw['conv_w'][i], wts['par'][i],
                                     lw['a_gain'][i].reshape(1, A_DV), n_s=1, l_r=256, chunk=A_CHUNK, carry=True)
            yb, r_new = _mixer_ret(proj3, cos_t, sin_t, zero_r, n_s=1, l_r=256, chunk=B_CHUNK, carry=True)
            yc = _mixer_pool(proj3, zero_p, lw['c_w'][i].astype(BF16), lw['c_scale'][i].reshape(1, MIX_W),
                             n_s=1, l_r=256, pos0=0)
            yd = _dilated_band(proj3, band_bias)
            return ya, yb, yc, yd, (s_new, r_new)

        h2, proj3, (s_new, r_new) = _layer_common(h2, p[i].reshape(n, PLE_DIM), bsz, t_len, wts, lw, i,
                                                  mix_fn, 1024, i == depth - 1, g_final)
        wins = []
        for gi, dil in enumerate(D_DILATIONS):
            keep = min(D_SPAN * dil, t_len)
            kcol = COL_DQKV + (N_DG + gi) * DBLK
            vcol = COL_DQKV + (2 * N_DG + gi) * DBLK
            kv = jnp.stack([proj3[:, t_len - keep:, kcol:kcol + DBLK],
                            proj3[:, t_len - keep:, vcol:vcol + DBLK]], axis=2)
            wins.append(kv.reshape(bsz, keep, 2, D_HEADS, D_HD))
        per_layer.append((s_new,
                          proj3[:, t_len - (A_CONV - 1):, COL_AQKV:COL_AQKV + A_QKV],
                          r_new.reshape(bsz, B_HEADS, B_DK, B_DV),
                          proj3[:, t_len - (POOL_MAX - 1):, COL_CIN:COL_CIN + MIX_W],
                          wins[0], wins[1], wins[2]))
    stacked = tuple(jnp.stack([ns[j] for ns in per_layer]) for j in range(7))
    return h2.reshape(bsz, t_len, D_MODEL), stacked


def _run_sample(x, p, states, wts, lw, t5_bias, g_final, depth):
    s_delta, s_conv, s_ret, s_pool, caches = states
    bsz, t_len, _ = x.shape
    n = bsz * t_len
    n_s = 16
    h2 = x.reshape(n, D_MODEL)
    pos = PAST_LEN + jnp.arange(t_len, dtype=jnp.int32)
    cos_t, sin_t = _rope_tables(pos, n_s)
    cache_flat, out_bufs, step_bias = [], [], []
    for gi, dil in enumerate(D_DILATIONS):
        c = caches[gi]
        l_buf = c.shape[2]
        flat_shape = (c.shape[0] * bsz, l_buf * KV_ROWS, D_HD)
        cache_flat.append(c.reshape(flat_shape))
        out_bufs.append(lax.empty(flat_shape, F32))
        step_bias.append(_step_bias(t5_bias, l_buf, t_len, gi, dil))
    s_ret2 = s_ret.reshape(s_ret.shape[0], bsz, B_HEADS // 2, LANES, B_DV)
    per_layer = []
    for i in range(depth):
        def mix_fn(proj3, ab3, i=i):
            ya, s_new = _mixer_delta(proj3, ab3, s_conv[i], s_delta[i], lw['conv_w'][i], wts['par'][i],
                                     lw['a_gain'][i].reshape(1, A_DV), n_s=n_s, l_r=t_len, chunk=t_len, carry=False)
            yb, r_new = _mixer_ret(proj3, cos_t, sin_t, s_ret2[i], n_s=n_s, l_r=t_len, chunk=t_len, carry=False)
            yc = _mixer_pool(proj3, s_pool[i], lw['c_w'][i].astype(BF16), lw['c_scale'][i].reshape(1, MIX_W),
                             n_s=1, l_r=t_len, pos0=PAST_LEN)
            od, lse = [], []
            for gi, dil in enumerate(D_DILATIONS):
                o, l, out_bufs[gi] = _dilated_step(proj3, cache_flat[gi], out_bufs[gi], i, gi, *step_bias[gi])
                od.append(o)
                lse.append(l)
            return ya, yb, yc, (od, lse), (s_new, r_new)

        h2, proj3, (s_new, r_new) = _layer_common(h2, p[i].reshape(n, PLE_DIM), bsz, t_len, wts, lw, i,
                                                  mix_fn, n, i == depth - 1, g_final)
        pool_new = jnp.concatenate([s_pool[i][:, t_len:], proj3[:, :, COL_CIN:COL_CIN + MIX_W]], axis=1)
        per_layer.append((s_new,
                          proj3[:, t_len - (A_CONV - 1):, COL_AQKV:COL_AQKV + A_QKV],
                          r_new.reshape(bsz, B_HEADS, B_DK, B_DV),
                          pool_new))
    stacked = tuple(jnp.stack([ns[j] for ns in per_layer]) for j in range(4))
    wins = tuple(out_bufs[gi].reshape(caches[gi].shape) for gi in range(N_DG))
    return h2.reshape(bsz, t_len, D_MODEL), stacked + wins


def kernel(x_prompt, x_sample, state_delta, state_delta_conv, state_ret, state_pool, cache_win0, cache_win1, cache_win2, p_prompt, p_sample, g_mix, w_in, conv_w, a_log, dt_bias, a_gain, c_w, c_scale, t5_bias, w_br, w_o, g_ffn, w_gate, w_up, w_down, g_ple, w_ple_gate, w_ple, g_final):
    depth = w_in.shape[0]
    wts = _prep_weights(w_in, w_br, w_o, w_gate, w_up, w_down, w_ple_gate, w_ple, a_log, dt_bias)
    lw = dict(g_mix=g_mix, conv_w=conv_w, a_gain=a_gain, c_w=c_w, c_scale=c_scale, g_ffn=g_ffn, g_ple=g_ple)
    y_p, (delta_p, conv_p, ret_p, pool_p, win0_p, win1_p, win2_p) = _run_prompt(
        x_prompt, p_prompt, wts, lw, t5_bias, g_final, depth)
    y_s, (delta_s, conv_s, ret_s, pool_s, win0_s, win1_s, win2_s) = _run_sample(
        x_sample, p_sample, (state_delta, state_delta_conv, state_ret, state_pool,
                             (cache_win0, cache_win1, cache_win2)), wts, lw, t5_bias, g_final, depth)
    return (y_p, y_s, delta_p, delta_s, conv_p, conv_s, ret_p, ret_s,
            pool_p, pool_s, win0_p, win0_s, win1_p, win1_s, win2_p, win2_s)
```

```python
import functools
import math

import numpy as np
import jax
import jax.numpy as jnp
from jax import lax
from jax.experimental import pallas as pl
from jax.experimental.pallas import tpu as pltpu

F32 = jnp.float32
BF16 = jnp.bfloat16
HIGHEST = lax.Precision.HIGHEST

D_MODEL = 1024
DEPTH = 4
PAST_LEN = 8192
MIX_W = D_MODEL // 2
N_BRANCH = 4
A_HEADS = 4
A_DK = MIX_W // A_HEADS
A_DV = MIX_W // A_HEADS
A_QKV = A_HEADS * (2 * A_DK + A_DV)
A_CONV = 4
A_CHUNK = 64
B_HEADS = 4
B_DV = MIX_W // B_HEADS
B_DK = B_DV // 2
B_CHUNK = 128
ROPE_BASE = 10000.0
C_GROUPS = 4
C_GW = MIX_W // C_GROUPS
POOL_WINDOWS = (2, 4, 8, 16)
POOL_MAX = 16
D_DILATIONS = (1, 4, 16)
N_DG = 3
D_SPAN = 128
D_HEADS = 4
D_HD = MIX_W // D_HEADS
REL_BUCKETS = 32
REL_MAX_DIST = D_SPAN * 16
D_FF = -(-8 * D_MODEL // (3 * 256)) * 256
PLE_DIM = 256
EPS = 1e-6

LANES = 128
SUBLANES = 8
VMEM_LIMIT_BYTES = 56 * 1024 * 1024

PROJ_W = 12800
COL_AQKV = 0
COL_AZ = 1536
COL_BQ = 2048
COL_BK = 2304
COL_BV = 2560
COL_BG = 3072
COL_CIN = 3584
COL_GATE = 4096
COL_DQKV = 8192
DBLK = D_HEADS * D_HD


def _cparams(sem):
    return pltpu.CompilerParams(dimension_semantics=sem, vmem_limit_bytes=VMEM_LIMIT_BYTES)


def _nt(a, b, precision=None):
    return lax.dot_general(a, b, (((1,), (1,)), ((), ())), precision=precision,
                           preferred_element_type=F32)


def _tn(a, b):
    return lax.dot_general(a, b, (((0,), (0,)), ((), ())), preferred_element_type=F32)


def _mm(a, b, precision=None):
    return jnp.dot(a, b, precision=precision, preferred_element_type=F32)


def _sigmoid(x):
    return 1.0 / (1.0 + jnp.exp(-x))


def _silu(x):
    return x * _sigmoid(x)


def _rms_rows(x, g):
    return x * lax.rsqrt(jnp.mean(x * x, axis=-1, keepdims=True) + EPS) * g


def _norm_matmul_kernel(x_ref, g_ref, w_ref, o_ref, u_ref):
    @pl.when(pl.program_id(1) == 0)
    def _():
        u_ref[...] = _rms_rows(x_ref[...], g_ref[...]).astype(BF16)

    o_ref[...] = _mm(u_ref[...], w_ref[...])


def _norm_matmul(x, g, w, tm, tn):
    n, d = x.shape
    wcols = w.shape[1]
    return pl.pallas_call(
        _norm_matmul_kernel,
        grid=(n // tm, wcols // tn),
        in_specs=[pl.BlockSpec((tm, d), lambda i, j: (i, 0)),
                  pl.BlockSpec((1, d), lambda i, j: (0, 0)),
                  pl.BlockSpec((d, tn), lambda i, j: (0, j))],
        out_specs=pl.BlockSpec((tm, tn), lambda i, j: (i, j)),
        out_shape=jax.ShapeDtypeStruct((n, wcols), F32),
        scratch_shapes=[pltpu.VMEM((tm, d), BF16)],
        compiler_params=_cparams(("parallel", "arbitrary")),
        name="norm_matmul",
    )(x, g.reshape(1, d), w)


def _delta_kernel(qkv_ref, z_ref, ab_ref, cprev_ref, sprev_ref, convw_ref, par_ref, gain_ref,
                  o_ref, snew_ref, ext_ref, s_ref, *, chunk, carry):
    j = pl.program_id(1)
    n_s, l_r, _ = qkv_ref.shape
    t_b = n_s * l_r
    n_c = t_b // chunk
    hdr = SUBLANES
    pre = A_CONV - 1

    @pl.when(j == 0)
    def _():
        ext_ref[:, 0:hdr, :] = jnp.zeros((n_s, hdr, A_QKV), F32)
        ext_ref[:, hdr - pre:hdr, :] = cprev_ref[...]
        if carry:
            s_ref[...] = sprev_ref[0]

    if carry:
        @pl.when(j > 0)
        def _():
            ext_ref[:, 0:hdr, :] = ext_ref[:, l_r:l_r + hdr, :]

    ext_ref[:, hdr:hdr + l_r, :] = qkv_ref[...]
    cw = convw_ref[...]

    def conv_cols(c0):
        acc = None
        for t in range(A_CONV):
            term = (ext_ref[:, hdr - pre + t:hdr - pre + t + l_r, c0:c0 + LANES]
                    * cw[t:t + 1, c0:c0 + LANES])
            acc = term if acc is None else acc + term
        return _silu(acc).reshape(t_b, LANES)

    ab = ab_ref[...].reshape(t_b, LANES)
    par = par_ref[...]
    beta_all = _sigmoid(ab)
    xs = ab + par[1:2, :]
    softplus = jnp.maximum(xs, 0.0) + jnp.log1p(jnp.exp(-jnp.abs(xs)))
    g_all = -jnp.exp(par[0:1, :]) * softplus

    sh = int(math.log2(chunk))
    ri = lax.broadcasted_iota(jnp.int32, (t_b, t_b), 0)
    ci = lax.broadcasted_iota(jnp.int32, (t_b, t_b), 1)
    same = lax.shift_right_logical(ri, sh) == lax.shift_right_logical(ci, sh)
    causal = jnp.logical_and(same, ci <= ri)
    strict = jnp.logical_and(same, ci < ri)
    gc_all = _mm(causal.astype(F32), g_all, precision=HIGHEST)
    sel = (lax.broadcasted_iota(jnp.int32, (SUBLANES, LANES), 0)
           == lax.broadcasted_iota(jnp.int32, (SUBLANES, LANES), 1)).astype(F32)
    gc_rows = _nt(sel, gc_all, precision=HIGHEST)
    glast_all = _mm(same.astype(F32), g_all, precision=HIGHEST)

    gain = gain_ref[...]
    z = z_ref[...].reshape(t_b, MIX_W)
    row_id = lax.broadcasted_iota(jnp.int32, (t_b, 1), 0)
    heads = range(A_HEADS)

    gcc = [gc_all[:, A_HEADS + h:A_HEADS + h + 1] for h in heads]
    gcr = [gc_rows[A_HEADS + h:A_HEADS + h + 1, :] for h in heads]
    glast = [glast_all[:, A_HEADS + h:A_HEADS + h + 1] for h in heads]
    bcol = [beta_all[:, h:h + 1] for h in heads]
    q = [conv_cols(h * A_DK) for h in heads]
    k = [conv_cols(A_HEADS * A_DK + h * A_DK) for h in heads]
    v = [conv_cols(2 * A_HEADS * A_DK + h * A_DV) for h in heads]
    q = [x * lax.rsqrt(jnp.sum(x * x, axis=-1, keepdims=True) + EPS) * (A_DK ** -0.5) for x in q]
    k = [x * lax.rsqrt(jnp.sum(x * x, axis=-1, keepdims=True) + EPS) for x in k]
    kb = [x.astype(BF16) for x in k]
    kk = [_nt(kb[h], kb[h]) for h in heads]
    qkr = [_nt(q[h].astype(BF16), kb[h]) for h in heads]
    gam = [jnp.where(causal, jnp.exp(jnp.where(causal, gcc[h] - gcr[h], 0.0)), 0.0) for h in heads]
    qk = [(qkr[h] * gam[h]).astype(BF16) for h in heads]
    a = [jnp.where(strict, bcol[h] * kk[h] * gam[h], 0.0) for h in heads]
    r = [-x for x in a]
    p = a
    for _ in range(sh - 1):
        pb = [x.astype(BF16) for x in p]
        p = [_mm(x, x) for x in pb]
        pr = [_mm(p[h].astype(BF16), r[h].astype(BF16)) for h in heads]
        r = [r[h] + p[h] + pr[h] for h in heads]
    ecol = [jnp.exp(x) for x in gcc]
    rhs = [jnp.concatenate([v[h] * bcol[h], k[h] * (bcol[h] * ecol[h])], axis=1) for h in heads]
    sol = [rhs[h] + _mm(r[h].astype(BF16), rhs[h].astype(BF16)) for h in heads]
    u = [x[:, :A_DV] for x in sol]
    w = [x[:, A_DV:] for x in sol]
    qe = [q[h] * ecol[h] for h in heads]
    kd = [(k[h] * jnp.exp(glast[h] - gcc[h])).astype(BF16) for h in heads]
    sdec = [jnp.exp(x) for x in glast]
    vnews = [[] for _ in heads]
    qss = [[] for _ in heads]
    if carry:
        s = [s_ref[h] for h in heads]
        for c in range(n_c):
            r0 = c * chunk
            wq = [jnp.concatenate([w[h][r0:r0 + chunk], qe[h][r0:r0 + chunk]], axis=0).astype(BF16)
                  for h in heads]
            t = [_mm(wq[h], s[h].astype(BF16)) for h in heads]
            vnew = [u[h][r0:r0 + chunk] - t[h][:chunk] for h in heads]
            upd = [_tn(kd[h][r0:r0 + chunk], vnew[h].astype(BF16)) for h in heads]
            s = [s[h] * sdec[h][r0:r0 + 1, :] + upd[h] for h in heads]
            for h in heads:
                vnews[h].append(vnew[h])
                qss[h].append(t[h][chunk:])
        vn_all = [jnp.concatenate(x, axis=0) if n_c > 1 else x[0] for x in vnews]
        vnb = [x.astype(BF16) for x in vn_all]
    else:
        olds = [[] for _ in heads]
        for c in range(n_c):
            r0 = c * chunk
            for h in heads:
                s_c = sprev_ref[c, h]
                olds[h].append(s_c)
                wq = jnp.concatenate([w[h][r0:r0 + chunk], qe[h][r0:r0 + chunk]], axis=0).astype(BF16)
                t = _mm(wq, s_c.astype(BF16))
                vnews[h].append(u[h][r0:r0 + chunk] - t[:chunk])
                qss[h].append(t[chunk:])
        vn_all = [jnp.concatenate(x, axis=0) for x in vnews]
        vnb = [x.astype(BF16) for x in vn_all]
        for c in range(n_c):
            r0 = c * chunk
            in_c = jnp.logical_and(row_id >= r0, row_id < r0 + chunk)
            for h in heads:
                upd = _tn(jnp.where(in_c, kd[h], jnp.zeros_like(kd[h])),
                          jnp.where(in_c, vnb[h], jnp.zeros_like(vnb[h])))
                snew_ref[c, h] = olds[h][c] * sdec[h][r0:r0 + 1, :] + upd
    outs = []
    for h in heads:
        qs_all = jnp.concatenate(qss[h], axis=0) if n_c > 1 else qss[h][0]
        o = qs_all + _mm(qk[h], vnb[h])
        o = o * lax.rsqrt(jnp.mean(o * o, axis=-1, keepdims=True) + EPS) * gain
        outs.append(o * _silu(z[:, h * A_DV:(h + 1) * A_DV]))
    o_ref[...] = jnp.concatenate(outs, axis=1).reshape(n_s, l_r, MIX_W)

    if carry:
        for h in heads:
            s_ref[h] = s[h]

        @pl.when(j == pl.num_programs(1) - 1)
        def _():
            for h in heads:
                snew_ref[0, h] = s[h]


def _mixer_delta(proj3, ab3, conv_prev, s_prev, conv_w, par, gain, *, n_s, l_r, chunk, carry):
    bsz, t_len, _ = proj3.shape
    grid = (bsz // n_s, t_len // l_r)
    kern = functools.partial(_delta_kernel, chunk=chunk, carry=carry)
    return pl.pallas_call(
        kern,
        grid=grid,
        in_specs=[
            pl.BlockSpec((n_s, l_r, A_QKV), lambda b, j: (b, j, COL_AQKV // A_QKV)),
            pl.BlockSpec((n_s, l_r, MIX_W), lambda b, j: (b, j, COL_AZ // MIX_W)),
            pl.BlockSpec((n_s, l_r, LANES), lambda b, j: (b, j, 0)),
            pl.BlockSpec((n_s, A_CONV - 1, A_QKV), lambda b, j: (b, 0, 0)),
            pl.BlockSpec((n_s, A_HEADS, A_DK, A_DV), lambda b, j: (b, 0, 0, 0)),
            pl.BlockSpec((A_CONV, A_QKV), lambda b, j: (0, 0)),
            pl.BlockSpec((SUBLANES, LANES), lambda b, j: (0, 0)),
            pl.BlockSpec((1, A_DV), lambda b, j: (0, 0)),
        ],
        out_specs=[
            pl.BlockSpec((n_s, l_r, MIX_W), lambda b, j: (b, j, 0)),
            pl.BlockSpec((n_s, A_HEADS, A_DK, A_DV), lambda b, j: (b, 0, 0, 0)),
        ],
        out_shape=[jax.ShapeDtypeStruct((bsz, t_len, MIX_W), F32),
                   jax.ShapeDtypeStruct((bsz, A_HEADS, A_DK, A_DV), F32)],
        scratch_shapes=[pltpu.VMEM((n_s, SUBLANES + l_r + SUBLANES, A_QKV), F32),
                        pltpu.VMEM((A_HEADS, A_DK, A_DV), F32)],
        compiler_params=_cparams(("parallel", "arbitrary")),
        name="mixer_delta",
    )(proj3, proj3, ab3, conv_prev, s_prev, conv_w, par, gain)


def _ret_log_gamma(h):
    return math.log1p(-(2.0 ** (-5.0 - h)))


def _ret_kernel(q_ref, k_ref, v_ref, g_ref, cos_ref, sin_ref, rprev_ref,
                o_ref, rnew_ref, r_ref, *, chunk, carry):
    j = pl.program_id(1)
    n_s, l_r, _ = q_ref.shape
    t_b = n_s * l_r
    n_c = t_b // chunk
    half = B_DK // 2

    if carry:
        @pl.when(j == 0)
        def _():
            r_ref[...] = rprev_ref[0]

    cos = cos_ref[...]
    sin = sin_ref[...]
    lane = lax.broadcasted_iota(jnp.int32, (1, LANES), 1)
    first_half = (lane % B_DK) < half
    lane_head = lane // B_DK

    def rope(x):
        swapped = jnp.where(first_half, pltpu.roll(x, LANES - half, axis=1),
                            pltpu.roll(x, half, axis=1))
        return x * cos + swapped * sin

    idx_i = lax.broadcasted_iota(jnp.int32, (chunk, chunk), 0)
    idx_j = lax.broadcasted_iota(jnp.int32, (chunk, chunk), 1)
    pos_col = lax.broadcasted_iota(jnp.int32, (chunk, 1), 0).astype(F32)
    row_head = lax.broadcasted_iota(jnp.int32, (LANES, 1), 0) // B_DK
    gate = g_ref[...].reshape(t_b, MIX_W)
    vv = v_ref[...].reshape(t_b, MIX_W)

    small = chunk < 2 * SUBLANES
    if small:
        sh = int(math.log2(chunk))
        bi = lax.broadcasted_iota(jnp.int32, (t_b, t_b), 0)
        bj = lax.broadcasted_iota(jnp.int32, (t_b, t_b), 1)
        bsame = lax.shift_right_logical(bi, sh) == lax.shift_right_logical(bj, sh)
        bcausal = jnp.logical_and(bsame, bj <= bi)
        bdiff = (bi - bj).astype(F32)
        row_id = lax.broadcasted_iota(jnp.int32, (t_b, 1), 0)
        pos_in = (row_id & (chunk - 1)).astype(F32)

    outs = [None] * B_HEADS
    for p in range(B_HEADS // 2):
        q2 = rope(q_ref[...].reshape(t_b, B_HEADS * B_DK)[:, p * LANES:(p + 1) * LANES])
        k2 = rope(k_ref[...].reshape(t_b, B_HEADS * B_DK)[:, p * LANES:(p + 1) * LANES]) * (B_DK ** -0.5)
        lgs = [_ret_log_gamma(2 * p + hh) for hh in range(2)]
        qh = [jnp.where(lane_head == hh, q2, 0.0) for hh in range(2)]
        kh = [jnp.where(lane_head == hh, k2, 0.0) for hh in range(2)]
        vh = [vv[:, (2 * p + hh) * B_DV:(2 * p + hh + 1) * B_DV] for hh in range(2)]
        row_scale = jnp.where(row_head == 0, math.exp(lgs[0] * chunk), math.exp(lgs[1] * chunk))
        if small:
            intra = []
            for hh in range(2):
                dec = jnp.where(bcausal, jnp.exp(lgs[hh] * jnp.where(bcausal, bdiff, 0.0)), 0.0)
                sc = _nt(qh[hh].astype(BF16), kh[hh].astype(BF16)) * dec
                intra.append(_mm(sc.astype(BF16), vh[hh].astype(BF16)))
            cross = [[], []]
            for c in range(n_c):
                r0 = c * chunk
                rp = r_ref[p] if carry else rprev_ref[c, p]
                qq = jnp.concatenate([qh[0][r0:r0 + chunk], qh[1][r0:r0 + chunk]], axis=0)
                t = _mm(qq.astype(BF16), rp.astype(BF16))
                cross[0].append(t[:chunk])
                cross[1].append(t[chunk:])
                in_c = jnp.logical_and(row_id >= r0, row_id < r0 + chunk)
                upd = None
                for hh in range(2):
                    kd = jnp.where(in_c, kh[hh] * jnp.exp(lgs[hh] * (chunk - 1.0 - pos_in)), 0.0)
                    term = _tn(kd.astype(BF16), vh[hh].astype(BF16))
                    upd = term if upd is None else upd + term
                rn = rp * row_scale + upd
                if carry:
                    r_ref[p] = rn
                else:
                    rnew_ref[c, p] = rn
            for hh in range(2):
                cr = jnp.concatenate(cross[hh], axis=0) * jnp.exp(lgs[hh] * (pos_in + 1.0))
                outs[2 * p + hh] = intra[hh] + cr
        else:
            pieces = [[], []]
            for c in range(n_c):
                r0 = c * chunk
                rp = r_ref[p] if carry else rprev_ref[c, p]
                rpb = rp.astype(BF16)
                upd = None
                for hh in range(2):
                    qc = qh[hh][r0:r0 + chunk].astype(BF16)
                    kc = kh[hh][r0:r0 + chunk]
                    vc = vh[hh][r0:r0 + chunk].astype(BF16)
                    causal = idx_j <= idx_i
                    dec = jnp.where(causal, jnp.exp(lgs[hh] * jnp.where(causal, (idx_i - idx_j).astype(F32), 0.0)), 0.0)
                    sc = _nt(qc, kc.astype(BF16)) * dec
                    oc = _mm(sc.astype(BF16), vc) + _mm(qc, rpb) * jnp.exp(lgs[hh] * (pos_col + 1.0))
                    pieces[hh].append(oc)
                    kd = kc * jnp.exp(lgs[hh] * (chunk - 1.0 - pos_col))
                    term = _tn(kd.astype(BF16), vc)
                    upd = term if upd is None else upd + term
                rn = rp * row_scale + upd
                if carry:
                    r_ref[p] = rn
                else:
                    rnew_ref[c, p] = rn
            for hh in range(2):
                outs[2 * p + hh] = jnp.concatenate(pieces[hh], axis=0) if n_c > 1 else pieces[hh][0]

    res = []
    for h in range(B_HEADS):
        o = outs[h]
        o = o * lax.rsqrt(jnp.mean(o * o, axis=-1, keepdims=True) + EPS)
        res.append(o * _silu(gate[:, h * B_DV:(h + 1) * B_DV]))
    o_ref[...] = jnp.concatenate(res, axis=1).reshape(n_s, l_r, MIX_W)

    if carry:
        @pl.when(j == pl.num_programs(1) - 1)
        def _():
            rnew_ref[0] = r_ref[...]


def _mixer_ret(proj3, cos_t, sin_t, r_prev, *, n_s, l_r, chunk, carry):
    bsz, t_len, _ = proj3.shape
    t_b = n_s * l_r
    grid = (bsz // n_s, t_len // l_r)
    n_tab = cos_t.shape[0] // t_b
    kern = functools.partial(_ret_kernel, chunk=chunk, carry=carry)
    qw = B_HEADS * B_DK
    return pl.pallas_call(
        kern,
        grid=grid,
        in_specs=[
            pl.BlockSpec((n_s, l_r, qw), lambda b, j: (b, j, COL_BQ // qw)),
            pl.BlockSpec((n_s, l_r, qw), lambda b, j: (b, j, COL_BK // qw)),
            pl.BlockSpec((n_s, l_r, MIX_W), lambda b, j: (b, j, COL_BV // MIX_W)),
            pl.BlockSpec((n_s, l_r, MIX_W), lambda b, j: (b, j, COL_BG // MIX_W)),
            pl.BlockSpec((t_b, LANES), lambda b, j: (j % n_tab, 0)),
            pl.BlockSpec((t_b, LANES), lambda b, j: (j % n_tab, 0)),
            pl.BlockSpec((n_s, B_HEADS // 2, LANES, B_DV), lambda b, j: (b, 0, 0, 0)),
        ],
        out_specs=[
            pl.BlockSpec((n_s, l_r, MIX_W), lambda b, j: (b, j, 0)),
            pl.BlockSpec((n_s, B_HEADS // 2, LANES, B_DV), lambda b, j: (b, 0, 0, 0)),
        ],
        out_shape=[jax.ShapeDtypeStruct((bsz, t_len, MIX_W), F32),
                   jax.ShapeDtypeStruct((bsz, B_HEADS // 2, LANES, B_DV), F32)],
        scratch_shapes=[pltpu.VMEM((B_HEADS // 2, LANES, B_DV), F32)],
        compiler_params=_cparams(("parallel", "arbitrary")),
        name="mixer_ret",
    )(proj3, proj3, proj3, proj3, cos_t, sin_t, r_prev)


def _rope_tables(pos, reps):
    half = B_DK // 2
    inv = ROPE_BASE ** (-jnp.arange(half, dtype=F32) / half)
    ang = pos.astype(F32)[:, None] * inv[None, :]
    cos = jnp.cos(ang)
    sin = jnp.sin(ang)
    cos_t = jnp.concatenate([cos, cos, cos, cos], axis=1)
    sin_t = jnp.concatenate([-sin, sin, -sin, sin], axis=1)
    return jnp.tile(cos_t, (reps, 1)), jnp.tile(sin_t, (reps, 1))


def _pool_kernel(x_ref, prev_ref, cw_ref, cs_ref, o_ref, ext_ref, *, pos0):
    j = pl.program_id(1)
    n_s, l_r, _ = x_ref.shape
    t_b = n_s * l_r
    hdr = POOL_MAX

    @pl.when(j == 0)
    def _():
        ext_ref[:, 0:hdr, :] = jnp.zeros((n_s, hdr, MIX_W), F32)
        ext_ref[:, 1:hdr, :] = prev_ref[...]

    @pl.when(j > 0)
    def _():
        ext_ref[:, 0:hdr, :] = ext_ref[:, l_r:l_r + hdr, :]

    ext_ref[:, hdr:hdr + l_r, :] = x_ref[...]
    pos = pos0 + j * l_r + lax.broadcasted_iota(jnp.int32, (1, l_r, 1), 1)
    outs = []
    for gi, w in enumerate(POOL_WINDOWS):
        c0 = gi * C_GW
        acc = None
        for t in range(w):
            term = ext_ref[:, hdr - t:hdr - t + l_r, c0:c0 + C_GW]
            acc = term if acc is None else acc + term
        cnt = jnp.minimum(pos + 1, w).astype(F32)
        y = acc / cnt - ext_ref[:, hdr:hdr + l_r, c0:c0 + C_GW]
        y = _mm(y.reshape(t_b, C_GW).astype(BF16), cw_ref[gi])
        outs.append(y)
    o_ref[...] = (jnp.concatenate(outs, axis=1) * cs_ref[...]).reshape(n_s, l_r, MIX_W)


def _mixer_pool(proj3, prev, c_w, c_scale, *, n_s, l_r, pos0):
    bsz, t_len, _ = proj3.shape
    grid = (bsz // n_s, t_len // l_r)
    kern = functools.partial(_pool_kernel, pos0=pos0)
    return pl.pallas_call(
        kern,
        grid=grid,
        in_specs=[
            pl.BlockSpec((n_s, l_r, MIX_W), lambda b, j: (b, j, COL_CIN // MIX_W)),
            pl.BlockSpec((n_s, POOL_MAX - 1, MIX_W), lambda b, j: (b, 0, 0)),
            pl.BlockSpec((C_GROUPS, C_GW, C_GW), lambda b, j: (0, 0, 0)),
            pl.BlockSpec((1, MIX_W), lambda b, j: (0, 0)),
        ],
        out_specs=pl.BlockSpec((n_s, l_r, MIX_W), lambda b, j: (b, j, 0)),
        out_shape=jax.ShapeDtypeStruct((bsz, t_len, MIX_W), F32),
        scratch_shapes=[pltpu.VMEM((n_s, POOL_MAX + l_r + SUBLANES, MIX_W), F32)],
        compiler_params=_cparams(("parallel", "arbitrary")),
        name="mixer_pool",
    )(proj3, prev, c_w, c_scale)


def _lookup_kernel(oh_ref, tab_ref, o_ref):
    o_ref[...] = _mm(oh_ref[...], tab_ref[...], precision=HIGHEST)


def _bias_lookup(idx, table):
    n_valid = idx.shape[0]
    tr = 4096
    n_rows = -(-n_valid // tr) * tr
    idx = np.concatenate([idx, np.zeros((n_rows - n_valid,), idx.dtype)])
    onehot = jnp.asarray(idx[:, None] == np.arange(REL_BUCKETS)[None, :], dtype=F32)
    tab = jnp.pad(table, ((0, 0), (0, LANES - table.shape[1])))
    out = pl.pallas_call(
        _lookup_kernel,
        grid=(n_rows // tr,),
        in_specs=[pl.BlockSpec((tr, REL_BUCKETS), lambda i: (i, 0)),
                  pl.BlockSpec((REL_BUCKETS, LANES), lambda i: (0, 0))],
        out_specs=pl.BlockSpec((tr, LANES), lambda i: (i, 0)),
        out_shape=jax.ShapeDtypeStruct((n_rows, LANES), F32),
        compiler_params=_cparams(("parallel",)),
        name="bias_lookup",
    )(onehot, tab)
    return out[:n_valid, :table.shape[1]]


def _t5_bucket_np(dist):
    exact = REL_BUCKETS // 2
    n = np.maximum(dist, 0)
    nf = np.maximum(n, 1).astype(np.float32)
    large = exact + (np.log(nf / np.float32(exact)) / np.float32(math.log(REL_MAX_DIST / exact))
                     * np.float32(REL_BUCKETS - exact)).astype(np.int32)
    large = np.minimum(large, REL_BUCKETS - 1)
    return np.where(n < exact, n, large).astype(np.int32)


SUPER = D_SPAN * D_DILATIONS[-1]
TILES = SUPER // D_SPAN


def _band_fused_kernel(*refs):
    ins = refs[:5 * N_DG]
    bias_ref = refs[5 * N_DG]
    o_ref = refs[5 * N_DG + 1]
    scr = refs[5 * N_DG + 2:]
    kf = scr[0:2 * N_DG:2]
    vf = scr[1:2 * N_DG:2]
    og = scr[2 * N_DG:3 * N_DG]
    lg = scr[3 * N_DG:4 * N_DG]
    has_prev = pl.program_id(2) > 0
    scale = D_HD ** -0.5
    for g, dil in enumerate(D_DILATIONS):
        q_ref, k_ref, v_ref, kp_ref, vp_ref = ins[5 * g:5 * g + 5]
        span = D_SPAN * dil
        kf[g][0:span, :] = kp_ref[...]
        kf[g][span:span + SUPER, :] = k_ref[...]
        vf[g][0:span, :] = vp_ref[...]
        vf[g][span:span + SUPER, :] = v_ref[...]
        sh = int(math.log2(dil))

        def tile(idx, carry, g=g, dil=dil, span=span, sh=sh, q_ref=q_ref):
            qi = lax.shift_right_logical(idx, sh)
            res = idx & (dil - 1)
            start = qi * span + res
            if dil == 1:
                start = pl.multiple_of(start, D_SPAN)
                rows = pl.ds(start, D_SPAN)
                krows = pl.ds(start, 2 * D_SPAN)
            else:
                rows = pl.ds(start, D_SPAN, stride=dil)
                krows = pl.ds(start, 2 * D_SPAN, stride=dil)
            q = q_ref[rows, :].astype(BF16)
            kk = kf[g][krows, :].astype(BF16)
            vv = vf[g][krows, :].astype(BF16)
            which = jnp.where(jnp.logical_or(has_prev, qi > 0), 1, 0)
            s = _nt(q, kk) * scale + bias_ref[g, which]
            m = jnp.max(s, axis=-1, keepdims=True)
            p = jnp.exp(s - m)
            l = jnp.sum(p, axis=-1, keepdims=True)
            og[g][rows, :] = _mm(p.astype(BF16), vv) / l
            lg[g][rows, :] = jnp.broadcast_to(m + jnp.log(l), (D_SPAN, D_HD))
            return carry

        lax.fori_loop(0, TILES, tile, 0, unroll=8)

    def merge(c, carry):
        rows = pl.ds(pl.multiple_of(c * D_SPAN, D_SPAN), D_SPAN)
        l0, l1, l2 = lg[0][rows, :], lg[1][rows, :], lg[2][rows, :]
        m = jnp.maximum(jnp.maximum(l0, l1), l2)
        e0, e1, e2 = jnp.exp(l0 - m), jnp.exp(l1 - m), jnp.exp(l2 - m)
        inv = 1.0 / (e0 + e1 + e2)
        o_ref[rows, :] = (e0 * og[0][rows, :] + e1 * og[1][rows, :] + e2 * og[2][rows, :]) * inv
        return carry

    lax.fori_loop(0, TILES, merge, 0)


def _dilated_band(proj3, band_bias):
    bsz, s_len, _ = proj3.shape
    n_blk = s_len // SUPER
    base = COL_DQKV // D_HD
    in_specs = []
    args = []
    for g, dil in enumerate(D_DILATIONS):
        span = D_SPAN * dil
        per = SUPER // span

        def col(c, g=g):
            return lambda b, h, n: (b, n, base + (c * N_DG + g) * D_HEADS + h)

        def col_prev(c, g=g, per=per):
            return lambda b, h, n: (b, jnp.maximum(n * per - 1, 0), base + (c * N_DG + g) * D_HEADS + h)

        in_specs += [pl.BlockSpec((None, SUPER, D_HD), col(0)),
                     pl.BlockSpec((None, SUPER, D_HD), col(1)),
                     pl.BlockSpec((None, SUPER, D_HD), col(2)),
                     pl.BlockSpec((None, span, D_HD), col_prev(1)),
                     pl.BlockSpec((None, span, D_HD), col_prev(2))]
        args += [proj3] * 5
    in_specs.append(pl.BlockSpec((N_DG, None, 2, D_SPAN, 2 * D_SPAN), lambda b, h, n: (0, h, 0, 0, 0)))
    scratch = []
    for dil in D_DILATIONS:
        scratch += [pltpu.VMEM((D_SPAN * dil + SUPER, D_HD), F32)] * 2
    scratch += [pltpu.VMEM((SUPER, D_HD), F32)] * (2 * N_DG)
    return pl.pallas_call(
        _band_fused_kernel,
        grid=(bsz, D_HEADS, n_blk),
        in_specs=in_specs,
        out_specs=pl.BlockSpec((None, SUPER, D_HD), lambda b, h, n: (b, n, h)),
        out_shape=jax.ShapeDtypeStruct((bsz, s_len, MIX_W), F32),
        scratch_shapes=scratch,
        compiler_params=_cparams(("parallel", "parallel", "arbitrary")),
        name="dilated_band",
    )(*args, band_bias)


KV_ROWS = 2 * D_HEADS


def _step_fused_kernel(q_ref, kn_ref, vn_ref, cache_ref, nxt_ref, bias_c_ref, mask_c_ref, bias_n_ref,
                       mask_n_ref, buf_ref, o_ref, lse_ref, out_ref, padk_ref, padv_ref, m_ref, l_ref, acc_ref):
    del buf_ref
    c = pl.program_id(1)
    n_ch = pl.num_programs(1)
    t_new = q_ref.shape[0]
    flat = cache_ref.shape[0]
    n_pos = flat // KV_ROWS
    shift = t_new * KV_ROWS
    scale = D_HD ** -0.5
    zeros_q = jnp.zeros((SUBLANES, D_HD), F32)

    out_ref[0:flat - shift, :] = cache_ref[shift:flat, :]

    @pl.when(c < n_ch - 1)
    def _():
        out_ref[flat - shift:flat, :] = nxt_ref[...]

    @pl.when(c == n_ch - 1)
    def _():
        for h in range(D_HEADS):
            cs = slice(h * D_HD, (h + 1) * D_HD)
            out_ref[pl.ds(flat - shift + h, t_new, stride=KV_ROWS), :] = kn_ref[:, cs]
            out_ref[pl.ds(flat - shift + D_HEADS + h, t_new, stride=KV_ROWS), :] = vn_ref[:, cs]

    heads = range(D_HEADS)
    cols = [slice(h * D_HD, (h + 1) * D_HD) for h in heads]
    rws = [slice(h * t_new, (h + 1) * t_new) for h in heads]
    lhs = [jnp.concatenate([q_ref[:, cols[h]], zeros_q], axis=0).astype(BF16) for h in heads]

    def pad16(p):
        return jnp.concatenate([p, jnp.zeros_like(p)], axis=0).astype(BF16)

    @pl.when(c == 0)
    def _():
        padk_ref[...] = jnp.zeros(padk_ref.shape, F32)
        padv_ref[...] = jnp.zeros(padv_ref.shape, F32)
        for h in heads:
            padk_ref[h, 0:t_new, :] = kn_ref[:, cols[h]]
            padv_ref[h, 0:t_new, :] = vn_ref[:, cols[h]]
        s = [_nt(lhs[h], padk_ref[h].astype(BF16))[0:t_new] * scale + bias_n_ref[h] + mask_n_ref[...]
             for h in heads]
        m = [jnp.max(x, axis=-1, keepdims=True) for x in s]
        p = [jnp.exp(s[h] - m[h]) for h in heads]
        pv = [_mm(pad16(p[h]), padv_ref[h].astype(BF16))[0:t_new] for h in heads]
        for h in heads:
            m_ref[rws[h], :] = m[h]
            l_ref[rws[h], :] = jnp.sum(p[h], axis=-1, keepdims=True)
            acc_ref[rws[h], :] = pv[h]

    kh = [cache_ref[pl.ds(h, n_pos, stride=KV_ROWS), :].astype(BF16) for h in heads]
    vh = [cache_ref[pl.ds(D_HEADS + h, n_pos, stride=KV_ROWS), :].astype(BF16) for h in heads]
    s = [_nt(lhs[h], kh[h])[0:t_new] * scale + bias_c_ref[h] + mask_c_ref[...] for h in heads]
    m_old = [m_ref[rws[h], :] for h in heads]
    m_new = [jnp.maximum(m_old[h], jnp.max(s[h], axis=-1, keepdims=True)) for h in heads]
    alpha = [jnp.exp(m_old[h] - m_new[h]) for h in heads]
    p = [jnp.exp(s[h] - m_new[h]) for h in heads]
    pv = [_mm(pad16(p[h]), vh[h])[0:t_new] for h in heads]
    l_new = [alpha[h] * l_ref[rws[h], :] + jnp.sum(p[h], axis=-1, keepdims=True) for h in heads]
    acc_new = [alpha[h] * acc_ref[rws[h], :] + pv[h] for h in heads]
    for h in heads:
        l_ref[rws[h], :] = l_new[h]
        acc_ref[rws[h], :] = acc_new[h]
        m_ref[rws[h], :] = m_new[h]

    @pl.when(c == n_ch - 1)
    def _():
        for h in heads:
            o_ref[:, cols[h]] = acc_new[h] / l_new[h]
            lse_ref[:, h:h + 1] = m_new[h] + jnp.log(l_new[h])


def _dilated_step(proj3, cache_flat, out_buf, layer, gi, bias_c, mask_c, bias_n, mask_n):
    bsz, t_new, _ = proj3.shape
    l_buf = cache_flat.shape[1] // KV_ROWS
    n_pos = min(l_buf, 1024)
    n_ch = l_buf // n_pos
    flat = n_pos * KV_ROWS
    shift = t_new * KV_ROWS
    per = flat // shift
    last = l_buf * KV_ROWS // shift - 1
    base = COL_DQKV // DBLK
    cq, ck, cv = base + gi, base + N_DG + gi, base + 2 * N_DG + gi
    n_hq = D_HEADS * t_new
    return pl.pallas_call(
        _step_fused_kernel,
        grid=(bsz, n_ch),
        in_specs=[
            pl.BlockSpec((None, t_new, DBLK), lambda b, c: (b, 0, cq)),
            pl.BlockSpec((None, t_new, DBLK), lambda b, c: (b, 0, ck)),
            pl.BlockSpec((None, t_new, DBLK), lambda b, c: (b, 0, cv)),
            pl.BlockSpec((None, flat, D_HD), lambda b, c: (layer * bsz + b, c, 0)),
            pl.BlockSpec((None, shift, D_HD), lambda b, c: (layer * bsz + b, jnp.minimum((c + 1) * per, last), 0)),
            pl.BlockSpec((D_HEADS, t_new, n_pos), lambda b, c: (0, 0, c)),
            pl.BlockSpec((t_new, n_pos), lambda b, c: (0, c)),
            pl.BlockSpec((D_HEADS, t_new, LANES), lambda b, c: (0, 0, 0)),
            pl.BlockSpec((t_new, LANES), lambda b, c: (0, 0)),
            pl.BlockSpec(memory_space=pl.ANY),
        ],
        out_specs=[
            pl.BlockSpec((None, t_new, DBLK), lambda b, c: (b, 0, 0)),
            pl.BlockSpec((None, t_new, D_HEADS), lambda b, c: (b, 0, 0)),
            pl.BlockSpec((None, flat, D_HD), lambda b, c: (layer * bsz + b, c, 0)),
        ],
        out_shape=[jax.ShapeDtypeStruct((bsz, t_new, DBLK), F32),
                   jax.ShapeDtypeStruct((bsz, t_new, D_HEADS), F32),
                   jax.ShapeDtypeStruct(out_buf.shape, F32)],
        scratch_shapes=[pltpu.VMEM((D_HEADS, LANES, D_HD), F32),
                        pltpu.VMEM((D_HEADS, LANES, D_HD), F32),
                        pltpu.VMEM((n_hq, 1), F32),
                        pltpu.VMEM((n_hq, 1), F32),
                        pltpu.VMEM((n_hq, D_HD), F32)],
        input_output_aliases={9: 2},
        compiler_params=_cparams(("parallel", "arbitrary")),
        name="dilated_step",
    )(proj3, proj3, proj3, cache_flat, cache_flat, bias_c, mask_c, bias_n, mask_n, out_buf)


def _win_kernel(k_ref, v_ref, buf_ref, out_ref):
    del buf_ref
    n_pos = k_ref.shape[0]
    for h in range(D_HEADS):
        cs = slice(h * D_HD, (h + 1) * D_HD)
        out_ref[pl.ds(h, n_pos, stride=KV_ROWS), :] = k_ref[:, cs]
        out_ref[pl.ds(D_HEADS + h, n_pos, stride=KV_ROWS), :] = v_ref[:, cs]


def _win_extract(proj3, out_buf, layer, gi, keep):
    bsz, t_len, _ = proj3.shape
    n_pos = min(keep, 512)
    first = (t_len - keep) // n_pos
    base = COL_DQKV // DBLK
    ck, cv = base + N_DG + gi, base + 2 * N_DG + gi
    return pl.pallas_call(
        _win_kernel,
        grid=(bsz, keep // n_pos),
        in_specs=[pl.BlockSpec((None, n_pos, DBLK), lambda b, c: (b, first + c, ck)),
                  pl.BlockSpec((None, n_pos, DBLK), lambda b, c: (b, first + c, cv)),
                  pl.BlockSpec(memory_space=pl.ANY)],
        out_specs=pl.BlockSpec((None, n_pos * KV_ROWS, D_HD), lambda b, c: (layer * bsz + b, c, 0)),
        out_shape=jax.ShapeDtypeStruct(out_buf.shape, F32),
        input_output_aliases={2: 0},
        compiler_params=_cparams(("parallel", "arbitrary")),
        name="win_extract",
    )(proj3, proj3, out_buf)


def _merge_kernel(*refs, split_d):
    if split_d:
        (gate_ref, ya_ref, yb_ref, yc_ref, o0_ref, o1_ref, o2_ref, l0_ref, l1_ref, l2_ref,
         wbr_ref, wo_ref, h_ref, out_ref) = refs
        l0, l1, l2 = l0_ref[...], l1_ref[...], l2_ref[...]
        m = jnp.maximum(jnp.maximum(l0, l1), l2)
        e0, e1, e2 = jnp.exp(l0 - m), jnp.exp(l1 - m), jnp.exp(l2 - m)
        inv = 1.0 / (e0 + e1 + e2)
        w0, w1, w2 = e0 * inv, e1 * inv, e2 * inv
        yd = []
        for h in range(D_HEADS):
            cs = slice(h * D_HD, (h + 1) * D_HD)
            yd.append(w0[:, h:h + 1] * o0_ref[:, cs] + w1[:, h:h + 1] * o1_ref[:, cs]
                      + w2[:, h:h + 1] * o2_ref[:, cs])
        y_d = jnp.concatenate(yd, axis=1)
    else:
        gate_ref, ya_ref, yb_ref, yc_ref, yd_ref, wbr_ref, wo_ref, h_ref, out_ref = refs
        y_d = yd_ref[...]
    branches = [ya_ref[...], yb_ref[...], yc_ref[...], y_d]
    merged = None
    for nbr in range(N_BRANCH):
        up = _mm(branches[nbr].astype(BF16), wbr_ref[nbr])
        term = _sigmoid(gate_ref[:, nbr * D_MODEL:(nbr + 1) * D_MODEL]) * up
        merged = term if merged is None else merged + term
    out_ref[...] = h_ref[...] + _mm(merged.astype(BF16), wo_ref[...])


def _merge(proj2, ya, yb, yc, yd, w_br, w_o, h2, tm):
    n = h2.shape[0]
    row = lambda width: pl.BlockSpec((tm, width), lambda i: (i, 0))
    split_d = isinstance(yd, tuple)
    if split_d:
        d_args = list(yd[0]) + list(yd[1])
        d_specs = [row(MIX_W)] * N_DG + [row(D_HEADS)] * N_DG
    else:
        d_args = [yd]
        d_specs = [row(MIX_W)]
    return pl.pallas_call(
        functools.partial(_merge_kernel, split_d=split_d),
        grid=(n // tm,),
        in_specs=[pl.BlockSpec((tm, N_BRANCH * D_MODEL), lambda i: (i, COL_GATE // (N_BRANCH * D_MODEL))),
                  row(MIX_W), row(MIX_W), row(MIX_W)] + d_specs + [
                  pl.BlockSpec((N_BRANCH, MIX_W, D_MODEL), lambda i: (0, 0, 0), pipeline_mode=pl.Buffered(1)),
                  pl.BlockSpec((D_MODEL, D_MODEL), lambda i: (0, 0), pipeline_mode=pl.Buffered(1)),
                  row(D_MODEL)],
        out_specs=row(D_MODEL),
        out_shape=jax.ShapeDtypeStruct((n, D_MODEL), F32),
        compiler_params=_cparams(("parallel",)),
        name="branch_merge",
    )(proj2, ya, yb, yc, *d_args, w_br, w_o, h2)


def _ffn_kernel(h_ref, gffn_ref, wg_ref, wu_ref, wd_ref, p_ref, gple_ref, wpg_ref, wple_ref, gfin_ref,
                out_ref, *, final, n_split):
    h = h_ref[...]
    f = _rms_rows(h, gffn_ref[...]).astype(BF16)
    tf = D_FF // n_split
    acc = None
    for k in range(n_split):
        cs = slice(k * tf, (k + 1) * tf)
        a = _mm(f, wg_ref[:, cs])
        b = _mm(f, wu_ref[:, cs])
        part = _mm((_silu(a) * b).astype(BF16), wd_ref[cs, :])
        acc = part if acc is None else acc + part
    h2 = h + acc
    e = _rms_rows(h2, gple_ref[...]).astype(BF16)
    gate = _sigmoid(_mm(e, wpg_ref[...]))
    h3 = h2 + gate * _mm(p_ref[...].astype(BF16), wple_ref[...])
    if final:
        out_ref[...] = _rms_rows(h3, gfin_ref[...])
    else:
        out_ref[...] = h3


def _ffn(h2, g_ffn, w_gate, w_up, w_down, p2, g_ple, w_pg, w_ple, g_fin, tm, n_split, final):
    n = h2.shape[0]
    vec = lambda: pl.BlockSpec((1, D_MODEL), lambda i: (0, 0))
    res = lambda shape: pl.BlockSpec(shape, lambda i: (0, 0), pipeline_mode=pl.Buffered(1))
    return pl.pallas_call(
        functools.partial(_ffn_kernel, final=final, n_split=n_split),
        grid=(n // tm,),
        in_specs=[pl.BlockSpec((tm, D_MODEL), lambda i: (i, 0)),
                  vec(),
                  res((D_MODEL, D_FF)),
                  res((D_MODEL, D_FF)),
                  res((D_FF, D_MODEL)),
                  pl.BlockSpec((tm, PLE_DIM), lambda i: (i, 0)),
                  vec(),
                  res((D_MODEL, D_MODEL)),
                  res((PLE_DIM, D_MODEL)),
                  vec()],
        out_specs=pl.BlockSpec((tm, D_MODEL), lambda i: (i, 0)),
        out_shape=jax.ShapeDtypeStruct((n, D_MODEL), F32),
        compiler_params=_cparams(("parallel",)),
        name="ffn_ple",
    )(h2, g_ffn.reshape(1, -1), w_gate, w_up, w_down, p2, g_ple.reshape(1, -1), w_pg, w_ple,
      g_fin.reshape(1, -1))


def _prep_weights(w_in, w_br, w_o, w_gate, w_up, w_down, w_ple_gate, w_ple, a_log, dt_bias):
    o_ab = A_QKV + MIX_W
    o_bq = o_ab + 2 * A_HEADS
    o_d = o_bq + 2 * B_HEADS * B_DK + 3 * MIX_W
    o_g = o_d + 3 * N_DG * MIX_W
    w_main = jnp.concatenate([w_in[:, :, :o_ab], w_in[:, :, o_bq:o_d], w_in[:, :, o_g:],
                              w_in[:, :, o_d:o_g]], axis=-1).astype(BF16)
    w_ab = jnp.pad(w_in[:, :, o_ab:o_bq], ((0, 0), (0, 0), (0, LANES - 2 * A_HEADS))).astype(BF16)
    par = jnp.zeros((a_log.shape[0], SUBLANES, LANES), F32)
    par = par.at[:, 0, A_HEADS:2 * A_HEADS].set(a_log)
    par = par.at[:, 1, A_HEADS:2 * A_HEADS].set(dt_bias)
    return dict(w_main=w_main, w_ab=w_ab, par=par, w_br=w_br.astype(BF16), w_o=w_o.astype(BF16),
                w_gate=w_gate.astype(BF16), w_up=w_up.astype(BF16), w_down=w_down.astype(BF16),
                w_pg=w_ple_gate.astype(BF16), w_ple=w_ple.astype(BF16))


def _band_bias(t5_bias):
    i = np.arange(D_SPAN)[:, None]
    j = np.arange(2 * D_SPAN)[None, :]
    rel = i + D_SPAN - j
    band = (rel >= 0) & (rel <= D_SPAN)
    masks = np.stack([band & (j >= D_SPAN), band])
    out = []
    for gi, dil in enumerate(D_DILATIONS):
        idx = _t5_bucket_np(rel * dil).reshape(-1)
        tab = _bias_lookup(idx, t5_bias[:, gi * D_HEADS:(gi + 1) * D_HEADS])
        tab = tab.reshape(D_SPAN, 2 * D_SPAN, D_HEADS).transpose(2, 0, 1)
        out.append(jnp.where(masks[None], tab[:, None], -jnp.inf))
    return jnp.stack(out)


def _step_bias(t5_bias, l_buf, t_new, gi, dil):
    t = np.arange(t_new)[:, None]
    j = np.arange(l_buf + LANES)[None, :]
    dist = l_buf + t - j
    valid = (dist >= 0) & (dist % dil == 0) & (dist <= D_SPAN * dil) & (j < l_buf + t_new)
    idx = _t5_bucket_np(np.where(valid, dist, 0)).reshape(-1)
    tab = _bias_lookup(idx, t5_bias[:, gi * D_HEADS:(gi + 1) * D_HEADS])
    tab = tab.reshape(t_new, l_buf + LANES, D_HEADS).transpose(2, 0, 1)
    tab = jnp.where(valid[None], tab, 0.0)
    mask = jnp.asarray(np.where(valid, 0.0, -np.inf), dtype=F32)
    return tab[:, :, :l_buf], mask[:, :l_buf], tab[:, :, l_buf:], mask[:, l_buf:]


def _layer_common(h2, p2, bsz, t_len, wts, lw, i, mix_fn, tm, final, g_final):
    n = bsz * t_len
    proj2 = _norm_matmul(h2, lw['g_mix'][i], wts['w_main'][i], tm, 2560)
    ab2 = _norm_matmul(h2, lw['g_mix'][i], wts['w_ab'][i], tm, LANES)
    proj3 = proj2.reshape(bsz, t_len, PROJ_W)
    ab3 = ab2.reshape(bsz, t_len, LANES)
    ya, yb, yc, yd, states = mix_fn(proj3, ab3)
    flat = lambda x: x.reshape(n, x.shape[-1])
    if isinstance(yd, tuple):
        yd = ([flat(x) for x in yd[0]], [flat(x) for x in yd[1]])
    else:
        yd = flat(yd)
    tm2 = min(tm, 512)
    h2 = _merge(proj2, flat(ya), flat(yb), flat(yc), yd, wts['w_br'][i], wts['w_o'][i], h2, tm2)
    tm3 = min(tm, 512)
    h2 = _ffn(h2, lw['g_ffn'][i], wts['w_gate'][i], wts['w_up'][i], wts['w_down'][i], p2,
              lw['g_ple'][i], wts['w_pg'][i], wts['w_ple'][i], g_final, tm3, 2, final)
    return h2, proj3, states


def _run_prompt(x, p, wts, lw, t5_bias, g_final, depth):
    bsz, t_len, _ = x.shape
    n = bsz * t_len
    h2 = x.reshape(n, D_MODEL)
    pos = jnp.arange(t_len, dtype=jnp.int32)
    cos_t, sin_t = _rope_tables(pos, 1)
    band_bias = _band_bias(t5_bias)
    zero_s = jnp.zeros((bsz, A_HEADS, A_DK, A_DV), F32)
    zero_c = jnp.zeros((bsz, A_CONV - 1, A_QKV), F32)
    zero_r = jnp.zeros((bsz, B_HEADS // 2, LANES, B_DV), F32)
    zero_p = jnp.zeros((bsz, POOL_MAX - 1, MIX_W), F32)
    keeps = [min(D_SPAN * dil, t_len) for dil in D_DILATIONS]
    win_bufs = [lax.empty((depth * bsz, keep * KV_ROWS, D_HD), F32) for keep in keeps]
    per_layer = []
    for i in range(depth):
        def mix_fn(proj3, ab3, i=i):
            ya, s_new = _mixer_delta(proj3, ab3, zero_c, zero_s, lw['conv_w'][i], wts['par'][i],
                                     lw['a_gain'][i].reshape(1, A_DV), n_s=1, l_r=256, chunk=A_CHUNK, carry=True)
            yb, r_new = _mixer_ret(proj3, cos_t, sin_t, zero_r, n_s=1, l_r=512, chunk=B_CHUNK, carry=True)
            yc = _mixer_pool(proj3, zero_p, lw['c_w'][i].astype(BF16), lw['c_scale'][i].reshape(1, MIX_W),
                             n_s=1, l_r=1024, pos0=0)
            yd = _dilated_band(proj3, band_bias)
            return ya, yb, yc, yd, (s_new, r_new)

        h2, proj3, (s_new, r_new) = _layer_common(h2, p[i].reshape(n, PLE_DIM), bsz, t_len, wts, lw, i,
                                                  mix_fn, 1024, i == depth - 1, g_final)
        for gi in range(N_DG):
            win_bufs[gi] = _win_extract(proj3, win_bufs[gi], i, gi, keeps[gi])
        per_layer.append((s_new,
                          proj3[:, t_len - (A_CONV - 1):, COL_AQKV:COL_AQKV + A_QKV],
                          r_new.reshape(bsz, B_HEADS, B_DK, B_DV),
                          proj3[:, t_len - (POOL_MAX - 1):, COL_CIN:COL_CIN + MIX_W]))
    stacked = tuple(jnp.stack([ns[j] for ns in per_layer]) for j in range(4))
    wins = tuple(win_bufs[gi].reshape(depth, bsz, keeps[gi], 2, D_HEADS, D_HD) for gi in range(N_DG))
    return h2.reshape(bsz, t_len, D_MODEL), stacked + wins


def _run_sample(x, p, states, wts, lw, t5_bias, g_final, depth):
    s_delta, s_conv, s_ret, s_pool, caches = states
    bsz, t_len, _ = x.shape
    n = bsz * t_len
    n_s = 16
    h2 = x.reshape(n, D_MODEL)
    pos = PAST_LEN + jnp.arange(t_len, dtype=jnp.int32)
    cos_t, sin_t = _rope_tables(pos, n_s)
    cache_flat, out_bufs, step_bias = [], [], []
    for gi, dil in enumerate(D_DILATIONS):
        c = caches[gi]
        l_buf = c.shape[2]
        flat_shape = (c.shape[0] * bsz, l_buf * KV_ROWS, D_HD)
        cache_flat.append(c.reshape(flat_shape))
        out_bufs.append(lax.empty(flat_shape, F32))
        step_bias.append(_step_bias(t5_bias, l_buf, t_len, gi, dil))
    s_ret2 = s_ret.reshape(s_ret.shape[0], bsz, B_HEADS // 2, LANES, B_DV)
    per_layer = []
    for i in range(depth):
        def mix_fn(proj3, ab3, i=i):
            ya, s_new = _mixer_delta(proj3, ab3, s_conv[i], s_delta[i], lw['conv_w'][i], wts['par'][i],
                                     lw['a_gain'][i].reshape(1, A_DV), n_s=n_s, l_r=t_len, chunk=t_len, carry=False)
            yb, r_new = _mixer_ret(proj3, cos_t, sin_t, s_ret2[i], n_s=n_s, l_r=t_len, chunk=t_len, carry=False)
            yc = _mixer_pool(proj3, s_pool[i], lw['c_w'][i].astype(BF16), lw['c_scale'][i].reshape(1, MIX_W),
                             n_s=1, l_r=t_len, pos0=PAST_LEN)
            od, lse = [], []
            for gi, dil in enumerate(D_DILATIONS):
                o, l, out_bufs[gi] = _dilated_step(proj3, cache_flat[gi], out_bufs[gi], i, gi, *step_bias[gi])
                od.append(o)
                lse.append(l)
            return ya, yb, yc, (od, lse), (s_new, r_new)

        h2, proj3, (s_new, r_new) = _layer_common(h2, p[i].reshape(n, PLE_DIM), bsz, t_len, wts, lw, i,
                                                  mix_fn, n, i == depth - 1, g_final)
        pool_new = jnp.concatenate([s_pool[i][:, t_len:], proj3[:, :, COL_CIN:COL_CIN + MIX_W]], axis=1)
        per_layer.append((s_new,
                          proj3[:, t_len - (A_CONV - 1):, COL_AQKV:COL_AQKV + A_QKV],
                          r_new.reshape(bsz, B_HEADS, B_DK, B_DV),
                          pool_new))
    stacked = tuple(jnp.stack([ns[j] for ns in per_layer]) for j in range(4))
    wins = tuple(out_bufs[gi].reshape(caches[gi].shape) for gi in range(N_DG))
    return h2.reshape(bsz, t_len, D_MODEL), stacked + wins


def kernel(x_prompt, x_sample, state_delta, state_delta_conv, state_ret, state_pool, cache_win0, cache_win1, cache_win2, p_prompt, p_sample, g_mix, w_in, conv_w, a_log, dt_bias, a_gain, c_w, c_scale, t5_bias, w_br, w_o, g_ffn, w_gate, w_up, w_down, g_ple, w_ple_gate, w_ple, g_final):
    depth = w_in.shape[0]
    wts = _prep_weights(w_in, w_br, w_o, w_gate, w_up, w_down, w_ple_gate, w_ple, a_log, dt_bias)
    lw = dict(g_mix=g_mix, conv_w=conv_w, a_gain=a_gain, c_w=c_w, c_scale=c_scale, g_ffn=g_ffn, g_ple=g_ple)
    y_p, (delta_p, conv_p, ret_p, pool_p, win0_p, win1_p, win2_p) = _run_prompt(
        x_prompt, p_prompt, wts, lw, t5_bias, g_final, depth)
    y_s, (delta_s, conv_s, ret_s, pool_s, win0_s, win1_s, win2_s) = _run_sample(
        x_sample, p_sample, (state_delta, state_delta_conv, state_ret, state_pool,
                             (cache_win0, cache_win1, cache_win2)), wts, lw, t5_bias, g_final, depth)
    return (y_p, y_s, delta_p, delta_s, conv_p, conv_s, ret_p, ret_s,
            pool_p, pool_s, win0_p, win0_s, win1_p, win1_s, win2_p, win2_s)
```

```python
import functools
import math

import numpy as np
import jax
import jax.numpy as jnp
from jax import lax
from jax.experimental import pallas as pl
from jax.experimental.pallas import tpu as pltpu

F32 = jnp.float32
BF16 = jnp.bfloat16
HIGHEST = lax.Precision.HIGHEST

D_MODEL = 1024
DEPTH = 4
PAST_LEN = 8192
MIX_W = D_MODEL // 2
N_BRANCH = 4
A_HEADS = 4
A_DK = MIX_W // A_HEADS
A_DV = MIX_W // A_HEADS
A_QKV = A_HEADS * (2 * A_DK + A_DV)
A_CONV = 4
A_CHUNK = 64
B_HEADS = 4
B_DV = MIX_W // B_HEADS
B_DK = B_DV // 2
B_CHUNK = 128
ROPE_BASE = 10000.0
C_GROUPS = 4
C_GW = MIX_W // C_GROUPS
POOL_WINDOWS = (2, 4, 8, 16)
POOL_MAX = 16
D_DILATIONS = (1, 4, 16)
N_DG = 3
D_SPAN = 128
D_HEADS = 4
D_HD = MIX_W // D_HEADS
REL_BUCKETS = 32
REL_MAX_DIST = D_SPAN * 16
D_FF = -(-8 * D_MODEL // (3 * 256)) * 256
PLE_DIM = 256
EPS = 1e-6

LANES = 128
SUBLANES = 8
VMEM_LIMIT_BYTES = 56 * 1024 * 1024

PROJ_W = 12800
COL_AQKV = 0
COL_AZ = 1536
COL_BQ = 2048
COL_BK = 2304
COL_BV = 2560
COL_BG = 3072
COL_CIN = 3584
COL_GATE = 4096
COL_DQKV = 8192
DBLK = D_HEADS * D_HD


def _cparams(sem):
    return pltpu.CompilerParams(dimension_semantics=sem, vmem_limit_bytes=VMEM_LIMIT_BYTES)


def _nt(a, b, precision=None):
    return lax.dot_general(a, b, (((1,), (1,)), ((), ())), precision=precision,
                           preferred_element_type=F32)


def _tn(a, b):
    return lax.dot_general(a, b, (((0,), (0,)), ((), ())), preferred_element_type=F32)


def _mm(a, b, precision=None):
    return jnp.dot(a, b, precision=precision, preferred_element_type=F32)


def _sigmoid(x):
    return 1.0 / (1.0 + jnp.exp(-x))


def _silu(x):
    return x * _sigmoid(x)


def _rms_rows(x, g):
    return x * lax.rsqrt(jnp.mean(x * x, axis=-1, keepdims=True) + EPS) * g


def _norm_matmul_kernel(x_ref, g_ref, w_ref, o_ref, u_ref):
    @pl.when(pl.program_id(1) == 0)
    def _():
        u_ref[...] = _rms_rows(x_ref[...], g_ref[...]).astype(BF16)

    o_ref[...] = _mm(u_ref[...], w_ref[...])


def _norm_matmul(x, g, w, layer, tm, tn):
    n, d = x.shape
    wcols = w.shape[2]
    return pl.pallas_call(
        _norm_matmul_kernel,
        grid=(n // tm, wcols // tn),
        in_specs=[pl.BlockSpec((tm, d), lambda i, j: (i, 0)),
                  pl.BlockSpec((1, d), lambda i, j: (0, 0)),
                  pl.BlockSpec((None, d, tn), lambda i, j: (layer, 0, j))],
        out_specs=pl.BlockSpec((tm, tn), lambda i, j: (i, j)),
        out_shape=jax.ShapeDtypeStruct((n, wcols), F32),
        scratch_shapes=[pltpu.VMEM((tm, d), BF16)],
        compiler_params=_cparams(("parallel", "arbitrary")),
        name="norm_matmul",
    )(x, g.reshape(1, d), w)


def _delta_kernel(qkv_ref, z_ref, ab_ref, cprev_ref, sprev_ref, convw_ref, par_ref, gain_ref,
                  o_ref, snew_ref, ext_ref, s_ref, *, chunk, carry):
    j = pl.program_id(1)
    n_s, l_r, _ = qkv_ref.shape
    t_b = n_s * l_r
    n_c = t_b // chunk
    hdr = SUBLANES
    pre = A_CONV - 1

    @pl.when(j == 0)
    def _():
        ext_ref[:, 0:hdr, :] = jnp.zeros((n_s, hdr, A_QKV), F32)
        ext_ref[:, hdr - pre:hdr, :] = cprev_ref[...]
        if carry:
            s_ref[...] = sprev_ref[0]

    if carry:
        @pl.when(j > 0)
        def _():
            ext_ref[:, 0:hdr, :] = ext_ref[:, l_r:l_r + hdr, :]

    ext_ref[:, hdr:hdr + l_r, :] = qkv_ref[...]
    cw = convw_ref[...]

    def conv_cols(c0):
        acc = None
        for t in range(A_CONV):
            term = (ext_ref[:, hdr - pre + t:hdr - pre + t + l_r, c0:c0 + LANES]
                    * cw[t:t + 1, c0:c0 + LANES])
            acc = term if acc is None else acc + term
        return _silu(acc).reshape(t_b, LANES)

    ab = ab_ref[...].reshape(t_b, LANES)
    par = par_ref[...]
    beta_all = _sigmoid(ab)
    xs = ab + par[1:2, :]
    softplus = jnp.maximum(xs, 0.0) + jnp.log1p(jnp.exp(-jnp.abs(xs)))
    g_all = -jnp.exp(par[0:1, :]) * softplus

    sh = int(math.log2(chunk))
    ri = lax.broadcasted_iota(jnp.int32, (t_b, t_b), 0)
    ci = lax.broadcasted_iota(jnp.int32, (t_b, t_b), 1)
    same = lax.shift_right_logical(ri, sh) == lax.shift_right_logical(ci, sh)
    causal = jnp.logical_and(same, ci <= ri)
    strict = jnp.logical_and(same, ci < ri)
    gc_all = _mm(causal.astype(F32), g_all, precision=HIGHEST)
    sel = (lax.broadcasted_iota(jnp.int32, (SUBLANES, LANES), 0)
           == lax.broadcasted_iota(jnp.int32, (SUBLANES, LANES), 1)).astype(F32)
    gc_rows = _nt(sel, gc_all, precision=HIGHEST)
    glast_all = _mm(same.astype(F32), g_all, precision=HIGHEST)

    gain = gain_ref[...]
    z = z_ref[...].reshape(t_b, MIX_W)
    row_id = lax.broadcasted_iota(jnp.int32, (t_b, 1), 0)
    heads = range(A_HEADS)

    gcc = [gc_all[:, A_HEADS + h:A_HEADS + h + 1] for h in heads]
    gcr = [gc_rows[A_HEADS + h:A_HEADS + h + 1, :] for h in heads]
    glast = [glast_all[:, A_HEADS + h:A_HEADS + h + 1] for h in heads]
    bcol = [beta_all[:, h:h + 1] for h in heads]
    q = [conv_cols(h * A_DK) for h in heads]
    k = [conv_cols(A_HEADS * A_DK + h * A_DK) for h in heads]
    v = [conv_cols(2 * A_HEADS * A_DK + h * A_DV) for h in heads]
    q = [x * lax.rsqrt(jnp.sum(x * x, axis=-1, keepdims=True) + EPS) * (A_DK ** -0.5) for x in q]
    k = [x * lax.rsqrt(jnp.sum(x * x, axis=-1, keepdims=True) + EPS) for x in k]
    kb = [x.astype(BF16) for x in k]
    kk = [_nt(kb[h], kb[h]) for h in heads]
    qkr = [_nt(q[h].astype(BF16), kb[h]) for h in heads]
    gam = [jnp.where(causal, jnp.exp(jnp.where(causal, gcc[h] - gcr[h], 0.0)), 0.0) for h in heads]
    qk = [(qkr[h] * gam[h]).astype(BF16) for h in heads]
    a = [jnp.where(strict, bcol[h] * kk[h] * gam[h], 0.0) for h in heads]
    r = [-x for x in a]
    p = a
    for _ in range(sh - 1):
        pb = [x.astype(BF16) for x in p]
        p = [_mm(x, x) for x in pb]
        pr = [_mm(p[h].astype(BF16), r[h].astype(BF16)) for h in heads]
        r = [r[h] + p[h] + pr[h] for h in heads]
    ecol = [jnp.exp(x) for x in gcc]
    rhs = [jnp.concatenate([v[h] * bcol[h], k[h] * (bcol[h] * ecol[h])], axis=1) for h in heads]
    sol = [rhs[h] + _mm(r[h].astype(BF16), rhs[h].astype(BF16)) for h in heads]
    u = [x[:, :A_DV] for x in sol]
    w = [x[:, A_DV:] for x in sol]
    qe = [q[h] * ecol[h] for h in heads]
    kd = [(k[h] * jnp.exp(glast[h] - gcc[h])).astype(BF16) for h in heads]
    sdec = [jnp.exp(x) for x in glast]
    vnews = [[] for _ in heads]
    qss = [[] for _ in heads]
    if carry:
        s = [s_ref[h] for h in heads]
        for c in range(n_c):
            r0 = c * chunk
            wq = [jnp.concatenate([w[h][r0:r0 + chunk], qe[h][r0:r0 + chunk]], axis=0).astype(BF16)
                  for h in heads]
            t = [_mm(wq[h], s[h].astype(BF16)) for h in heads]
            vnew = [u[h][r0:r0 + chunk] - t[h][:chunk] for h in heads]
            upd = [_tn(kd[h][r0:r0 + chunk], vnew[h].astype(BF16)) for h in heads]
            s = [s[h] * sdec[h][r0:r0 + 1, :] + upd[h] for h in heads]
            for h in heads:
                vnews[h].append(vnew[h])
                qss[h].append(t[h][chunk:])
        vn_all = [jnp.concatenate(x, axis=0) if n_c > 1 else x[0] for x in vnews]
        vnb = [x.astype(BF16) for x in vn_all]
    else:
        olds = [[] for _ in heads]
        for c in range(n_c):
            r0 = c * chunk
            for h in heads:
                s_c = sprev_ref[c, h]
                olds[h].append(s_c)
                wq = jnp.concatenate([w[h][r0:r0 + chunk], qe[h][r0:r0 + chunk]], axis=0).astype(BF16)
                t = _mm(wq, s_c.astype(BF16))
                vnews[h].append(u[h][r0:r0 + chunk] - t[:chunk])
                qss[h].append(t[chunk:])
        vn_all = [jnp.concatenate(x, axis=0) for x in vnews]
        vnb = [x.astype(BF16) for x in vn_all]
        for c in range(n_c):
            r0 = c * chunk
            in_c = jnp.logical_and(row_id >= r0, row_id < r0 + chunk)
            for h in heads:
                upd = _tn(jnp.where(in_c, kd[h], jnp.zeros_like(kd[h])),
                          jnp.where(in_c, vnb[h], jnp.zeros_like(vnb[h])))
                snew_ref[c, h] = olds[h][c] * sdec[h][r0:r0 + 1, :] + upd
    outs = []
    for h in heads:
        qs_all = jnp.concatenate(qss[h], axis=0) if n_c > 1 else qss[h][0]
        o = qs_all + _mm(qk[h], vnb[h])
        o = o * lax.rsqrt(jnp.mean(o * o, axis=-1, keepdims=True) + EPS) * gain
        outs.append(o * _silu(z[:, h * A_DV:(h + 1) * A_DV]))
    o_ref[...] = jnp.concatenate(outs, axis=1).reshape(n_s, l_r, MIX_W)

    if carry:
        for h in heads:
            s_ref[h] = s[h]

        @pl.when(j == pl.num_programs(1) - 1)
        def _():
            for h in heads:
                snew_ref[0, h] = s[h]


def _mixer_delta(proj3, ab3, conv_prev, s_prev, layer, conv_w, par, gain, *, n_s, l_r, chunk, carry):
    bsz, t_len, _ = proj3.shape
    grid = (bsz // n_s, t_len // l_r)
    kern = functools.partial(_delta_kernel, chunk=chunk, carry=carry)
    return pl.pallas_call(
        kern,
        grid=grid,
        in_specs=[
            pl.BlockSpec((n_s, l_r, A_QKV), lambda b, j: (b, j, COL_AQKV // A_QKV)),
            pl.BlockSpec((n_s, l_r, MIX_W), lambda b, j: (b, j, COL_AZ // MIX_W)),
            pl.BlockSpec((n_s, l_r, LANES), lambda b, j: (b, j, 0)),
            pl.BlockSpec((n_s, A_CONV - 1, A_QKV), lambda b, j: (b, 0, 0)),
            pl.BlockSpec((None, n_s, A_HEADS, A_DK, A_DV), lambda b, j: (layer, b, 0, 0, 0)),
            pl.BlockSpec((A_CONV, A_QKV), lambda b, j: (0, 0)),
            pl.BlockSpec((SUBLANES, LANES), lambda b, j: (0, 0)),
            pl.BlockSpec((1, A_DV), lambda b, j: (0, 0)),
        ],
        out_specs=[
            pl.BlockSpec((n_s, l_r, MIX_W), lambda b, j: (b, j, 0)),
            pl.BlockSpec((n_s, A_HEADS, A_DK, A_DV), lambda b, j: (b, 0, 0, 0)),
        ],
        out_shape=[jax.ShapeDtypeStruct((bsz, t_len, MIX_W), F32),
                   jax.ShapeDtypeStruct((bsz, A_HEADS, A_DK, A_DV), F32)],
        scratch_shapes=[pltpu.VMEM((n_s, SUBLANES + l_r + SUBLANES, A_QKV), F32),
                        pltpu.VMEM((A_HEADS, A_DK, A_DV), F32)],
        compiler_params=_cparams(("parallel", "arbitrary")),
        name="mixer_delta",
    )(proj3, proj3, ab3, conv_prev, s_prev, conv_w, par, gain)


def _ret_log_gamma(h):
    return math.log1p(-(2.0 ** (-5.0 - h)))


def _ret_kernel(q_ref, k_ref, v_ref, g_ref, cos_ref, sin_ref, rprev_ref,
                o_ref, rnew_ref, r_ref, *, chunk, carry):
    j = pl.program_id(1)
    n_s, l_r, _ = q_ref.shape
    t_b = n_s * l_r
    n_c = t_b // chunk
    half = B_DK // 2

    if carry:
        @pl.when(j == 0)
        def _():
            r_ref[...] = rprev_ref[0]

    cos = cos_ref[...]
    sin = sin_ref[...]
    lane = lax.broadcasted_iota(jnp.int32, (1, LANES), 1)
    first_half = (lane % B_DK) < half
    lane_head = lane // B_DK

    def rope(x):
        swapped = jnp.where(first_half, pltpu.roll(x, LANES - half, axis=1),
                            pltpu.roll(x, half, axis=1))
        return x * cos + swapped * sin

    idx_i = lax.broadcasted_iota(jnp.int32, (chunk, chunk), 0)
    idx_j = lax.broadcasted_iota(jnp.int32, (chunk, chunk), 1)
    pos_col = lax.broadcasted_iota(jnp.int32, (chunk, 1), 0).astype(F32)
    row_head = lax.broadcasted_iota(jnp.int32, (LANES, 1), 0) // B_DK
    gate = g_ref[...].reshape(t_b, MIX_W)
    vv = v_ref[...].reshape(t_b, MIX_W)

    small = chunk < 2 * SUBLANES
    if small:
        sh = int(math.log2(chunk))
        bi = lax.broadcasted_iota(jnp.int32, (t_b, t_b), 0)
        bj = lax.broadcasted_iota(jnp.int32, (t_b, t_b), 1)
        bsame = lax.shift_right_logical(bi, sh) == lax.shift_right_logical(bj, sh)
        bcausal = jnp.logical_and(bsame, bj <= bi)
        bdiff = (bi - bj).astype(F32)
        row_id = lax.broadcasted_iota(jnp.int32, (t_b, 1), 0)
        pos_in = (row_id & (chunk - 1)).astype(F32)

    outs = [None] * B_HEADS
    for p in range(B_HEADS // 2):
        q2 = rope(q_ref[...].reshape(t_b, B_HEADS * B_DK)[:, p * LANES:(p + 1) * LANES])
        k2 = rope(k_ref[...].reshape(t_b, B_HEADS * B_DK)[:, p * LANES:(p + 1) * LANES]) * (B_DK ** -0.5)
        lgs = [_ret_log_gamma(2 * p + hh) for hh in range(2)]
        qh = [jnp.where(lane_head == hh, q2, 0.0) for hh in range(2)]
        kh = [jnp.where(lane_head == hh, k2, 0.0) for hh in range(2)]
        vh = [vv[:, (2 * p + hh) * B_DV:(2 * p + hh + 1) * B_DV] for hh in range(2)]
        row_scale = jnp.where(row_head == 0, math.exp(lgs[0] * chunk), math.exp(lgs[1] * chunk))
        if small:
            intra = []
            for hh in range(2):
                dec = jnp.where(bcausal, jnp.exp(lgs[hh] * jnp.where(bcausal, bdiff, 0.0)), 0.0)
                sc = _nt(qh[hh].astype(BF16), kh[hh].astype(BF16)) * dec
                intra.append(_mm(sc.astype(BF16), vh[hh].astype(BF16)))
            cross = [[], []]
            for c in range(n_c):
                r0 = c * chunk
                rp = r_ref[p] if carry else rprev_ref[c, p]
                qq = jnp.concatenate([qh[0][r0:r0 + chunk], qh[1][r0:r0 + chunk]], axis=0)
                t = _mm(qq.astype(BF16), rp.astype(BF16))
                cross[0].append(t[:chunk])
                cross[1].append(t[chunk:])
                in_c = jnp.logical_and(row_id >= r0, row_id < r0 + chunk)
                upd = None
                for hh in range(2):
                    kd = jnp.where(in_c, kh[hh] * jnp.exp(lgs[hh] * (chunk - 1.0 - pos_in)), 0.0)
                    term = _tn(kd.astype(BF16), vh[hh].astype(BF16))
                    upd = term if upd is None else upd + term
                rn = rp * row_scale + upd
                if carry:
                    r_ref[p] = rn
                else:
                    rnew_ref[c, p] = rn
            for hh in range(2):
                cr = jnp.concatenate(cross[hh], axis=0) * jnp.exp(lgs[hh] * (pos_in + 1.0))
                outs[2 * p + hh] = intra[hh] + cr
        else:
            pieces = [[], []]
            for c in range(n_c):
                r0 = c * chunk
                rp = r_ref[p] if carry else rprev_ref[c, p]
                rpb = rp.astype(BF16)
                upd = None
                for hh in range(2):
                    qc = qh[hh][r0:r0 + chunk].astype(BF16)
                    kc = kh[hh][r0:r0 + chunk]
                    vc = vh[hh][r0:r0 + chunk].astype(BF16)
                    causal = idx_j <= idx_i
                    dec = jnp.where(causal, jnp.exp(lgs[hh] * jnp.where(causal, (idx_i - idx_j).astype(F32), 0.0)), 0.0)
                    sc = _nt(qc, kc.astype(BF16)) * dec
                    oc = _mm(sc.astype(BF16), vc) + _mm(qc, rpb) * jnp.exp(lgs[hh] * (pos_col + 1.0))
                    pieces[hh].append(oc)
                    kd = kc * jnp.exp(lgs[hh] * (chunk - 1.0 - pos_col))
                    term = _tn(kd.astype(BF16), vc)
                    upd = term if upd is None else upd + term
                rn = rp * row_scale + upd
                if carry:
                    r_ref[p] = rn
                else:
                    rnew_ref[c, p] = rn
            for hh in range(2):
                outs[2 * p + hh] = jnp.concatenate(pieces[hh], axis=0) if n_c > 1 else pieces[hh][0]

    res = []
    for h in range(B_HEADS):
        o = outs[h]
        o = o * lax.rsqrt(jnp.mean(o * o, axis=-1, keepdims=True) + EPS)
        res.append(o * _silu(gate[:, h * B_DV:(h + 1) * B_DV]))
    o_ref[...] = jnp.concatenate(res, axis=1).reshape(n_s, l_r, MIX_W)

    if carry:
        @pl.when(j == pl.num_programs(1) - 1)
        def _():
            rnew_ref[0] = r_ref[...]


def _mixer_ret(proj3, cos_t, sin_t, r_prev, *, n_s, l_r, chunk, carry):
    bsz, t_len, _ = proj3.shape
    t_b = n_s * l_r
    grid = (bsz // n_s, t_len // l_r)
    n_tab = cos_t.shape[0] // t_b
    kern = functools.partial(_ret_kernel, chunk=chunk, carry=carry)
    qw = B_HEADS * B_DK
    return pl.pallas_call(
        kern,
        grid=grid,
        in_specs=[
            pl.BlockSpec((n_s, l_r, qw), lambda b, j: (b, j, COL_BQ // qw)),
            pl.BlockSpec((n_s, l_r, qw), lambda b, j: (b, j, COL_BK // qw)),
            pl.BlockSpec((n_s, l_r, MIX_W), lambda b, j: (b, j, COL_BV // MIX_W)),
            pl.BlockSpec((n_s, l_r, MIX_W), lambda b, j: (b, j, COL_BG // MIX_W)),
            pl.BlockSpec((t_b, LANES), lambda b, j: (j % n_tab, 0)),
            pl.BlockSpec((t_b, LANES), lambda b, j: (j % n_tab, 0)),
            pl.BlockSpec((n_s, B_HEADS // 2, LANES, B_DV), lambda b, j: (b, 0, 0, 0)),
        ],
        out_specs=[
            pl.BlockSpec((n_s, l_r, MIX_W), lambda b, j: (b, j, 0)),
            pl.BlockSpec((n_s, B_HEADS // 2, LANES, B_DV), lambda b, j: (b, 0, 0, 0)),
        ],
        out_shape=[jax.ShapeDtypeStruct((bsz, t_len, MIX_W), F32),
                   jax.ShapeDtypeStruct((bsz, B_HEADS // 2, LANES, B_DV), F32)],
        scratch_shapes=[pltpu.VMEM((B_HEADS // 2, LANES, B_DV), F32)],
        compiler_params=_cparams(("parallel", "arbitrary")),
        name="mixer_ret",
    )(proj3, proj3, proj3, proj3, cos_t, sin_t, r_prev)


def _rope_tables(pos, reps):
    half = B_DK // 2
    inv = ROPE_BASE ** (-jnp.arange(half, dtype=F32) / half)
    ang = pos.astype(F32)[:, None] * inv[None, :]
    cos = jnp.cos(ang)
    sin = jnp.sin(ang)
    cos_t = jnp.concatenate([cos, cos, cos, cos], axis=1)
    sin_t = jnp.concatenate([-sin, sin, -sin, sin], axis=1)
    return jnp.tile(cos_t, (reps, 1)), jnp.tile(sin_t, (reps, 1))


def _pool_kernel(x_ref, prev_ref, cw_ref, cs_ref, o_ref, ext_ref, *, pos0):
    j = pl.program_id(1)
    n_s, l_r, _ = x_ref.shape
    t_b = n_s * l_r
    hdr = POOL_MAX

    @pl.when(j == 0)
    def _():
        ext_ref[:, 0:hdr, :] = jnp.zeros((n_s, hdr, MIX_W), F32)
        ext_ref[:, 1:hdr, :] = prev_ref[...]

    @pl.when(j > 0)
    def _():
        ext_ref[:, 0:hdr, :] = ext_ref[:, l_r:l_r + hdr, :]

    ext_ref[:, hdr:hdr + l_r, :] = x_ref[...]
    pos = pos0 + j * l_r + lax.broadcasted_iota(jnp.int32, (1, l_r, 1), 1)
    outs = []
    for gi, w in enumerate(POOL_WINDOWS):
        c0 = gi * C_GW
        acc = None
        for t in range(w):
            term = ext_ref[:, hdr - t:hdr - t + l_r, c0:c0 + C_GW]
            acc = term if acc is None else acc + term
        cnt = jnp.minimum(pos + 1, w).astype(F32)
        y = acc / cnt - ext_ref[:, hdr:hdr + l_r, c0:c0 + C_GW]
        y = _mm(y.reshape(t_b, C_GW).astype(BF16), cw_ref[gi])
        outs.append(y)
    o_ref[...] = (jnp.concatenate(outs, axis=1) * cs_ref[...]).reshape(n_s, l_r, MIX_W)


def _mixer_pool(proj3, prev, c_w, c_scale, *, n_s, l_r, pos0):
    bsz, t_len, _ = proj3.shape
    grid = (bsz // n_s, t_len // l_r)
    kern = functools.partial(_pool_kernel, pos0=pos0)
    return pl.pallas_call(
        kern,
        grid=grid,
        in_specs=[
            pl.BlockSpec((n_s, l_r, MIX_W), lambda b, j: (b, j, COL_CIN // MIX_W)),
            pl.BlockSpec((n_s, POOL_MAX - 1, MIX_W), lambda b, j: (b, 0, 0)),
            pl.BlockSpec((C_GROUPS, C_GW, C_GW), lambda b, j: (0, 0, 0)),
            pl.BlockSpec((1, MIX_W), lambda b, j: (0, 0)),
        ],
        out_specs=pl.BlockSpec((n_s, l_r, MIX_W), lambda b, j: (b, j, 0)),
        out_shape=jax.ShapeDtypeStruct((bsz, t_len, MIX_W), F32),
        scratch_shapes=[pltpu.VMEM((n_s, POOL_MAX + l_r + SUBLANES, MIX_W), F32)],
        compiler_params=_cparams(("parallel", "arbitrary")),
        name="mixer_pool",
    )(proj3, prev, c_w, c_scale)


def _lookup_kernel(oh_ref, tab_ref, o_ref):
    o_ref[...] = _mm(oh_ref[...], tab_ref[...], precision=HIGHEST)


def _bias_lookup(idx, table):
    n_valid = idx.shape[0]
    tr = 4096
    n_rows = -(-n_valid // tr) * tr
    idx = np.concatenate([idx, np.zeros((n_rows - n_valid,), idx.dtype)])
    onehot = jnp.asarray(idx[:, None] == np.arange(REL_BUCKETS)[None, :], dtype=F32)
    tab = jnp.pad(table, ((0, 0), (0, LANES - table.shape[1])))
    out = pl.pallas_call(
        _lookup_kernel,
        grid=(n_rows // tr,),
        in_specs=[pl.BlockSpec((tr, REL_BUCKETS), lambda i: (i, 0)),
                  pl.BlockSpec((REL_BUCKETS, LANES), lambda i: (0, 0))],
        out_specs=pl.BlockSpec((tr, LANES), lambda i: (i, 0)),
        out_shape=jax.ShapeDtypeStruct((n_rows, LANES), F32),
        compiler_params=_cparams(("parallel",)),
        name="bias_lookup",
    )(onehot, tab)
    return out[:n_valid, :table.shape[1]]


def _t5_bucket_np(dist):
    exact = REL_BUCKETS // 2
    n = np.maximum(dist, 0)
    nf = np.maximum(n, 1).astype(np.float32)
    large = exact + (np.log(nf / np.float32(exact)) / np.float32(math.log(REL_MAX_DIST / exact))
                     * np.float32(REL_BUCKETS - exact)).astype(np.int32)
    large = np.minimum(large, REL_BUCKETS - 1)
    return np.where(n < exact, n, large).astype(np.int32)


SUPER = D_SPAN * D_DILATIONS[-1]
TILES = SUPER // D_SPAN


def _band_fused_kernel(*refs):
    ins = refs[:5 * N_DG]
    bias_ref = refs[5 * N_DG]
    o_ref = refs[5 * N_DG + 1]
    scr = refs[5 * N_DG + 2:]
    kf = scr[0:2 * N_DG:2]
    vf = scr[1:2 * N_DG:2]
    og = scr[2 * N_DG:3 * N_DG]
    lg = scr[3 * N_DG:4 * N_DG]
    has_prev = pl.program_id(2) > 0
    scale = D_HD ** -0.5
    for g, dil in enumerate(D_DILATIONS):
        q_ref, k_ref, v_ref, kp_ref, vp_ref = ins[5 * g:5 * g + 5]
        span = D_SPAN * dil
        kf[g][0:span, :] = kp_ref[...]
        kf[g][span:span + SUPER, :] = k_ref[...]
        vf[g][0:span, :] = vp_ref[...]
        vf[g][span:span + SUPER, :] = v_ref[...]
        sh = int(math.log2(dil))

        def tile(idx, carry, g=g, dil=dil, span=span, sh=sh, q_ref=q_ref):
            qi = lax.shift_right_logical(idx, sh)
            res = idx & (dil - 1)
            start = qi * span + res
            if dil == 1:
                start = pl.multiple_of(start, D_SPAN)
                rows = pl.ds(start, D_SPAN)
                krows = pl.ds(start, 2 * D_SPAN)
            else:
                rows = pl.ds(start, D_SPAN, stride=dil)
                krows = pl.ds(start, 2 * D_SPAN, stride=dil)
            q = q_ref[rows, :].astype(BF16)
            kk = kf[g][krows, :].astype(BF16)
            vv = vf[g][krows, :].astype(BF16)
            which = jnp.where(jnp.logical_or(has_prev, qi > 0), 1, 0)
            s = _nt(q, kk) * scale + bias_ref[g, which]
            m = jnp.max(s, axis=-1, keepdims=True)
            p = jnp.exp(s - m)
            l = jnp.sum(p, axis=-1, keepdims=True)
            og[g][rows, :] = _mm(p.astype(BF16), vv) / l
            lg[g][rows, :] = jnp.broadcast_to(m + jnp.log(l), (D_SPAN, D_HD))
            return carry

        lax.fori_loop(0, TILES, tile, 0, unroll=8)

    def merge(c, carry):
        rows = pl.ds(pl.multiple_of(c * D_SPAN, D_SPAN), D_SPAN)
        l0, l1, l2 = lg[0][rows, :], lg[1][rows, :], lg[2][rows, :]
        m = jnp.maximum(jnp.maximum(l0, l1), l2)
        e0, e1, e2 = jnp.exp(l0 - m), jnp.exp(l1 - m), jnp.exp(l2 - m)
        inv = 1.0 / (e0 + e1 + e2)
        o_ref[rows, :] = (e0 * og[0][rows, :] + e1 * og[1][rows, :] + e2 * og[2][rows, :]) * inv
        return carry

    lax.fori_loop(0, TILES, merge, 0)


def _dilated_band(proj3, band_bias):
    bsz, s_len, _ = proj3.shape
    n_blk = s_len // SUPER
    base = COL_DQKV // D_HD
    in_specs = []
    args = []
    for g, dil in enumerate(D_DILATIONS):
        span = D_SPAN * dil
        per = SUPER // span

        def col(c, g=g):
            return lambda b, h, n: (b, n, base + (c * N_DG + g) * D_HEADS + h)

        def col_prev(c, g=g, per=per):
            return lambda b, h, n: (b, jnp.maximum(n * per - 1, 0), base + (c * N_DG + g) * D_HEADS + h)

        in_specs += [pl.BlockSpec((None, SUPER, D_HD), col(0)),
                     pl.BlockSpec((None, SUPER, D_HD), col(1)),
                     pl.BlockSpec((None, SUPER, D_HD), col(2)),
                     pl.BlockSpec((None, span, D_HD), col_prev(1)),
                     pl.BlockSpec((None, span, D_HD), col_prev(2))]
        args += [proj3] * 5
    in_specs.append(pl.BlockSpec((N_DG, None, 2, D_SPAN, 2 * D_SPAN), lambda b, h, n: (0, h, 0, 0, 0)))
    scratch = []
    for dil in D_DILATIONS:
        scratch += [pltpu.VMEM((D_SPAN * dil + SUPER, D_HD), F32)] * 2
    scratch += [pltpu.VMEM((SUPER, D_HD), F32)] * (2 * N_DG)
    return pl.pallas_call(
        _band_fused_kernel,
        grid=(bsz, D_HEADS, n_blk),
        in_specs=in_specs,
        out_specs=pl.BlockSpec((None, SUPER, D_HD), lambda b, h, n: (b, n, h)),
        out_shape=jax.ShapeDtypeStruct((bsz, s_len, MIX_W), F32),
        scratch_shapes=scratch,
        compiler_params=_cparams(("parallel", "parallel", "arbitrary")),
        name="dilated_band",
    )(*args, band_bias)


KV_ROWS = 2 * D_HEADS


def _step_fused_kernel(q_ref, kn_ref, vn_ref, cache_ref, nxt_ref, bias_c_ref, mask_c_ref, bias_n_ref,
                       mask_n_ref, buf_ref, o_ref, lse_ref, out_ref, padk_ref, padv_ref, m_ref, l_ref, acc_ref):
    del buf_ref
    c = pl.program_id(1)
    n_ch = pl.num_programs(1)
    t_new = q_ref.shape[0]
    flat = cache_ref.shape[0]
    n_pos = flat // KV_ROWS
    shift = t_new * KV_ROWS
    scale = D_HD ** -0.5
    zeros_q = jnp.zeros((SUBLANES, D_HD), F32)

    out_ref[0:flat - shift, :] = cache_ref[shift:flat, :]

    @pl.when(c < n_ch - 1)
    def _():
        out_ref[flat - shift:flat, :] = nxt_ref[...]

    @pl.when(c == n_ch - 1)
    def _():
        for h in range(D_HEADS):
            cs = slice(h * D_HD, (h + 1) * D_HD)
            out_ref[pl.ds(flat - shift + h, t_new, stride=KV_ROWS), :] = kn_ref[:, cs]
            out_ref[pl.ds(flat - shift + D_HEADS + h, t_new, stride=KV_ROWS), :] = vn_ref[:, cs]

    heads = range(D_HEADS)
    cols = [slice(h * D_HD, (h + 1) * D_HD) for h in heads]
    rws = [slice(h * t_new, (h + 1) * t_new) for h in heads]
    lhs = [jnp.concatenate([q_ref[:, cols[h]], zeros_q], axis=0).astype(BF16) for h in heads]

    def pad16(p):
        return jnp.concatenate([p, jnp.zeros_like(p)], axis=0).astype(BF16)

    @pl.when(c == 0)
    def _():
        padk_ref[...] = jnp.zeros(padk_ref.shape, F32)
        padv_ref[...] = jnp.zeros(padv_ref.shape, F32)
        for h in heads:
            padk_ref[h, 0:t_new, :] = kn_ref[:, cols[h]]
            padv_ref[h, 0:t_new, :] = vn_ref[:, cols[h]]
        s = [_nt(lhs[h], padk_ref[h].astype(BF16))[0:t_new] * scale + bias_n_ref[h] + mask_n_ref[...]
             for h in heads]
        m = [jnp.max(x, axis=-1, keepdims=True) for x in s]
        p = [jnp.exp(s[h] - m[h]) for h in heads]
        pv = [_mm(pad16(p[h]), padv_ref[h].astype(BF16))[0:t_new] for h in heads]
        for h in heads:
            m_ref[rws[h], :] = m[h]
            l_ref[rws[h], :] = jnp.sum(p[h], axis=-1, keepdims=True)
            acc_ref[rws[h], :] = pv[h]

    kh = [cache_ref[pl.ds(h, n_pos, stride=KV_ROWS), :].astype(BF16) for h in heads]
    vh = [cache_ref[pl.ds(D_HEADS + h, n_pos, stride=KV_ROWS), :].astype(BF16) for h in heads]
    s = [_nt(lhs[h], kh[h])[0:t_new] * scale + bias_c_ref[h] + mask_c_ref[...] for h in heads]
    m_old = [m_ref[rws[h], :] for h in heads]
    m_new = [jnp.maximum(m_old[h], jnp.max(s[h], axis=-1, keepdims=True)) for h in heads]
    alpha = [jnp.exp(m_old[h] - m_new[h]) for h in heads]
    p = [jnp.exp(s[h] - m_new[h]) for h in heads]
    pv = [_mm(pad16(p[h]), vh[h])[0:t_new] for h in heads]
    l_new = [alpha[h] * l_ref[rws[h], :] + jnp.sum(p[h], axis=-1, keepdims=True) for h in heads]
    acc_new = [alpha[h] * acc_ref[rws[h], :] + pv[h] for h in heads]
    for h in heads:
        l_ref[rws[h], :] = l_new[h]
        acc_ref[rws[h], :] = acc_new[h]
        m_ref[rws[h], :] = m_new[h]

    @pl.when(c == n_ch - 1)
    def _():
        for h in heads:
            o_ref[:, cols[h]] = acc_new[h] / l_new[h]
            lse_ref[:, h:h + 1] = m_new[h] + jnp.log(l_new[h])


def _dilated_step(proj3, cache_flat, out_buf, layer, gi, bias_c, mask_c, bias_n, mask_n):
    bsz, t_new, _ = proj3.shape
    l_buf = cache_flat.shape[1] // KV_ROWS
    n_pos = min(l_buf, 1024)
    n_ch = l_buf // n_pos
    flat = n_pos * KV_ROWS
    shift = t_new * KV_ROWS
    per = flat // shift
    last = l_buf * KV_ROWS // shift - 1
    base = COL_DQKV // DBLK
    cq, ck, cv = base + gi, base + N_DG + gi, base + 2 * N_DG + gi
    n_hq = D_HEADS * t_new
    return pl.pallas_call(
        _step_fused_kernel,
        grid=(bsz, n_ch),
        in_specs=[
            pl.BlockSpec((None, t_new, DBLK), lambda b, c: (b, 0, cq)),
            pl.BlockSpec((None, t_new, DBLK), lambda b, c: (b, 0, ck)),
            pl.BlockSpec((None, t_new, DBLK), lambda b, c: (b, 0, cv)),
            pl.BlockSpec((None, flat, D_HD), lambda b, c: (layer * bsz + b, c, 0)),
            pl.BlockSpec((None, shift, D_HD), lambda b, c: (layer * bsz + b, jnp.minimum((c + 1) * per, last), 0)),
            pl.BlockSpec((D_HEADS, t_new, n_pos), lambda b, c: (0, 0, c)),
            pl.BlockSpec((t_new, n_pos), lambda b, c: (0, c)),
            pl.BlockSpec((D_HEADS, t_new, LANES), lambda b, c: (0, 0, 0)),
            pl.BlockSpec((t_new, LANES), lambda b, c: (0, 0)),
            pl.BlockSpec(memory_space=pl.ANY),
        ],
        out_specs=[
            pl.BlockSpec((None, t_new, DBLK), lambda b, c: (b, 0, 0)),
            pl.BlockSpec((None, t_new, D_HEADS), lambda b, c: (b, 0, 0)),
            pl.BlockSpec((None, flat, D_HD), lambda b, c: (layer * bsz + b, c, 0)),
        ],
        out_shape=[jax.ShapeDtypeStruct((bsz, t_new, DBLK), F32),
                   jax.ShapeDtypeStruct((bsz, t_new, D_HEADS), F32),
                   jax.ShapeDtypeStruct(out_buf.shape, F32)],
        scratch_shapes=[pltpu.VMEM((D_HEADS, LANES, D_HD), F32),
                        pltpu.VMEM((D_HEADS, LANES, D_HD), F32),
                        pltpu.VMEM((n_hq, 1), F32),
                        pltpu.VMEM((n_hq, 1), F32),
                        pltpu.VMEM((n_hq, D_HD), F32)],
        input_output_aliases={9: 2},
        compiler_params=_cparams(("parallel", "arbitrary")),
        name="dilated_step",
    )(proj3, proj3, proj3, cache_flat, cache_flat, bias_c, mask_c, bias_n, mask_n, out_buf)


def _win_kernel(k_ref, v_ref, buf_ref, out_ref):
    del buf_ref
    n_pos = k_ref.shape[0]
    for h in range(D_HEADS):
        cs = slice(h * D_HD, (h + 1) * D_HD)
        out_ref[pl.ds(h, n_pos, stride=KV_ROWS), :] = k_ref[:, cs]
        out_ref[pl.ds(D_HEADS + h, n_pos, stride=KV_ROWS), :] = v_ref[:, cs]


def _win_extract(proj3, out_buf, layer, gi, keep):
    bsz, t_len, _ = proj3.shape
    n_pos = min(keep, 512)
    first = (t_len - keep) // n_pos
    base = COL_DQKV // DBLK
    ck, cv = base + N_DG + gi, base + 2 * N_DG + gi
    return pl.pallas_call(
        _win_kernel,
        grid=(bsz, keep // n_pos),
        in_specs=[pl.BlockSpec((None, n_pos, DBLK), lambda b, c: (b, first + c, ck)),
                  pl.BlockSpec((None, n_pos, DBLK), lambda b, c: (b, first + c, cv)),
                  pl.BlockSpec(memory_space=pl.ANY)],
        out_specs=pl.BlockSpec((None, n_pos * KV_ROWS, D_HD), lambda b, c: (layer * bsz + b, c, 0)),
        out_shape=jax.ShapeDtypeStruct(out_buf.shape, F32),
        input_output_aliases={2: 0},
        compiler_params=_cparams(("parallel", "arbitrary")),
        name="win_extract",
    )(proj3, proj3, out_buf)


def _merge_kernel(*refs, split_d):
    if split_d:
        (gate_ref, ya_ref, yb_ref, yc_ref, o0_ref, o1_ref, o2_ref, l0_ref, l1_ref, l2_ref,
         wbr_ref, wo_ref, h_ref, out_ref) = refs
        l0, l1, l2 = l0_ref[...], l1_ref[...], l2_ref[...]
        m = jnp.maximum(jnp.maximum(l0, l1), l2)
        e0, e1, e2 = jnp.exp(l0 - m), jnp.exp(l1 - m), jnp.exp(l2 - m)
        inv = 1.0 / (e0 + e1 + e2)
        w0, w1, w2 = e0 * inv, e1 * inv, e2 * inv
        yd = []
        for h in range(D_HEADS):
            cs = slice(h * D_HD, (h + 1) * D_HD)
            yd.append(w0[:, h:h + 1] * o0_ref[:, cs] + w1[:, h:h + 1] * o1_ref[:, cs]
                      + w2[:, h:h + 1] * o2_ref[:, cs])
        y_d = jnp.concatenate(yd, axis=1)
    else:
        gate_ref, ya_ref, yb_ref, yc_ref, yd_ref, wbr_ref, wo_ref, h_ref, out_ref = refs
        y_d = yd_ref[...]
    branches = [ya_ref[...], yb_ref[...], yc_ref[...], y_d]
    merged = None
    for nbr in range(N_BRANCH):
        up = _mm(branches[nbr].astype(BF16), wbr_ref[nbr])
        term = _sigmoid(gate_ref[:, nbr * D_MODEL:(nbr + 1) * D_MODEL]) * up
        merged = term if merged is None else merged + term
    out_ref[...] = h_ref[...] + _mm(merged.astype(BF16), wo_ref[...])


def _merge(proj2, ya, yb, yc, yd, w_br, w_o, layer, h2, tm):
    n = h2.shape[0]
    row = lambda width: pl.BlockSpec((tm, width), lambda i: (i, 0))
    split_d = isinstance(yd, tuple)
    if split_d:
        d_args = list(yd[0]) + list(yd[1])
        d_specs = [row(MIX_W)] * N_DG + [row(D_HEADS)] * N_DG
    else:
        d_args = [yd]
        d_specs = [row(MIX_W)]
    return pl.pallas_call(
        functools.partial(_merge_kernel, split_d=split_d),
        grid=(n // tm,),
        in_specs=[pl.BlockSpec((tm, N_BRANCH * D_MODEL), lambda i: (i, COL_GATE // (N_BRANCH * D_MODEL))),
                  row(MIX_W), row(MIX_W), row(MIX_W)] + d_specs + [
                  pl.BlockSpec((None, N_BRANCH, MIX_W, D_MODEL), lambda i: (layer, 0, 0, 0),
                               pipeline_mode=pl.Buffered(1)),
                  pl.BlockSpec((None, D_MODEL, D_MODEL), lambda i: (layer, 0, 0), pipeline_mode=pl.Buffered(1)),
                  row(D_MODEL)],
        out_specs=row(D_MODEL),
        out_shape=jax.ShapeDtypeStruct((n, D_MODEL), F32),
        compiler_params=_cparams(("parallel",)),
        name="branch_merge",
    )(proj2, ya, yb, yc, *d_args, w_br, w_o, h2)


def _ffn_kernel(h_ref, gffn_ref, wg_ref, wu_ref, wd_ref, p_ref, gple_ref, wpg_ref, wple_ref, gfin_ref,
                out_ref, *, final, n_split):
    h = h_ref[...]
    f = _rms_rows(h, gffn_ref[...]).astype(BF16)
    tf = D_FF // n_split
    acc = None
    for k in range(n_split):
        cs = slice(k * tf, (k + 1) * tf)
        a = _mm(f, wg_ref[:, cs])
        b = _mm(f, wu_ref[:, cs])
        part = _mm((_silu(a) * b).astype(BF16), wd_ref[cs, :])
        acc = part if acc is None else acc + part
    h2 = h + acc
    e = _rms_rows(h2, gple_ref[...]).astype(BF16)
    gate = _sigmoid(_mm(e, wpg_ref[...]))
    h3 = h2 + gate * _mm(p_ref[...].astype(BF16), wple_ref[...])
    if final:
        out_ref[...] = _rms_rows(h3, gfin_ref[...])
    else:
        out_ref[...] = h3


def _ffn(h2, g_ffn, w_gate, w_up, w_down, p3, g_ple, w_pg, w_ple, g_fin, layer, tm, n_split, final):
    n = h2.shape[0]
    vec = lambda: pl.BlockSpec((1, D_MODEL), lambda i: (0, 0))
    res = lambda shape: pl.BlockSpec((None,) + shape, lambda i: (layer, 0, 0), pipeline_mode=pl.Buffered(1))
    return pl.pallas_call(
        functools.partial(_ffn_kernel, final=final, n_split=n_split),
        grid=(n // tm,),
        in_specs=[pl.BlockSpec((tm, D_MODEL), lambda i: (i, 0)),
                  vec(),
                  res((D_MODEL, D_FF)),
                  res((D_MODEL, D_FF)),
                  res((D_FF, D_MODEL)),
                  pl.BlockSpec((None, tm, PLE_DIM), lambda i: (layer, i, 0)),
                  vec(),
                  res((D_MODEL, D_MODEL)),
                  res((PLE_DIM, D_MODEL)),
                  vec()],
        out_specs=pl.BlockSpec((tm, D_MODEL), lambda i: (i, 0)),
        out_shape=jax.ShapeDtypeStruct((n, D_MODEL), F32),
        compiler_params=_cparams(("parallel",)),
        name="ffn_ple",
    )(h2, g_ffn.reshape(1, -1), w_gate, w_up, w_down, p3, g_ple.reshape(1, -1), w_pg, w_ple,
      g_fin.reshape(1, -1))


def _prep_weights(w_in, w_br, w_o, w_gate, w_up, w_down, w_ple_gate, w_ple, a_log, dt_bias):
    o_ab = A_QKV + MIX_W
    o_bq = o_ab + 2 * A_HEADS
    o_d = o_bq + 2 * B_HEADS * B_DK + 3 * MIX_W
    o_g = o_d + 3 * N_DG * MIX_W
    w_main = jnp.concatenate([w_in[:, :, :o_ab], w_in[:, :, o_bq:o_d], w_in[:, :, o_g:],
                              w_in[:, :, o_d:o_g]], axis=-1).astype(BF16)
    w_ab = jnp.pad(w_in[:, :, o_ab:o_bq], ((0, 0), (0, 0), (0, LANES - 2 * A_HEADS))).astype(BF16)
    par = jnp.zeros((a_log.shape[0], SUBLANES, LANES), F32)
    par = par.at[:, 0, A_HEADS:2 * A_HEADS].set(a_log)
    par = par.at[:, 1, A_HEADS:2 * A_HEADS].set(dt_bias)
    return dict(w_main=w_main, w_ab=w_ab, par=par, w_br=w_br.astype(BF16), w_o=w_o.astype(BF16),
                w_gate=w_gate.astype(BF16), w_up=w_up.astype(BF16), w_down=w_down.astype(BF16),
                w_pg=w_ple_gate.astype(BF16), w_ple=w_ple.astype(BF16))


def _band_bias(t5_bias):
    i = np.arange(D_SPAN)[:, None]
    j = np.arange(2 * D_SPAN)[None, :]
    rel = i + D_SPAN - j
    band = (rel >= 0) & (rel <= D_SPAN)
    masks = np.stack([band & (j >= D_SPAN), band])
    out = []
    for gi, dil in enumerate(D_DILATIONS):
        idx = _t5_bucket_np(rel * dil).reshape(-1)
        tab = _bias_lookup(idx, t5_bias[:, gi * D_HEADS:(gi + 1) * D_HEADS])
        tab = tab.reshape(D_SPAN, 2 * D_SPAN, D_HEADS).transpose(2, 0, 1)
        out.append(jnp.where(masks[None], tab[:, None], -jnp.inf))
    return jnp.stack(out)


def _step_bias(t5_bias, l_buf, t_new, gi, dil):
    t = np.arange(t_new)[:, None]
    j = np.arange(l_buf + LANES)[None, :]
    dist = l_buf + t - j
    valid = (dist >= 0) & (dist % dil == 0) & (dist <= D_SPAN * dil) & (j < l_buf + t_new)
    idx = _t5_bucket_np(np.where(valid, dist, 0)).reshape(-1)
    tab = _bias_lookup(idx, t5_bias[:, gi * D_HEADS:(gi + 1) * D_HEADS])
    tab = tab.reshape(t_new, l_buf + LANES, D_HEADS).transpose(2, 0, 1)
    tab = jnp.where(valid[None], tab, 0.0)
    mask = jnp.asarray(np.where(valid, 0.0, -np.inf), dtype=F32)
    return tab[:, :, :l_buf], mask[:, :l_buf], tab[:, :, l_buf:], mask[:, l_buf:]


def _layer_common(h2, p3, bsz, t_len, wts, lw, i, mix_fn, tm, final, g_final):
    n = bsz * t_len
    tm_proj = min(n, 2 * tm)
    proj2 = _norm_matmul(h2, lw['g_mix'][i], wts['w_main'], i, tm_proj, 1280)
    ab2 = _norm_matmul(h2, lw['g_mix'][i], wts['w_ab'], i, tm, LANES)
    proj3 = proj2.reshape(bsz, t_len, PROJ_W)
    ab3 = ab2.reshape(bsz, t_len, LANES)
    ya, yb, yc, yd, states = mix_fn(proj3, ab3)
    flat = lambda x: x.reshape(n, x.shape[-1])
    if isinstance(yd, tuple):
        yd = ([flat(x) for x in yd[0]], [flat(x) for x in yd[1]])
    else:
        yd = flat(yd)
    tm2 = min(tm, 512)
    h2 = _merge(proj2, flat(ya), flat(yb), flat(yc), yd, wts['w_br'], wts['w_o'], i, h2, tm2)
    tm3 = min(tm, 512)
    h2 = _ffn(h2, lw['g_ffn'][i], wts['w_gate'], wts['w_up'], wts['w_down'], p3,
              lw['g_ple'][i], wts['w_pg'], wts['w_ple'], g_final, i, tm3, 2, final)
    return h2, proj3, states


def _run_prompt(x, p, wts, lw, t5_bias, g_final, depth):
    bsz, t_len, _ = x.shape
    n = bsz * t_len
    h2 = x.reshape(n, D_MODEL)
    p3 = p.reshape(depth, n, PLE_DIM)
    pos = jnp.arange(t_len, dtype=jnp.int32)
    cos_t, sin_t = _rope_tables(pos, 1)
    band_bias = _band_bias(t5_bias)
    zero_s = jnp.zeros((1, bsz, A_HEADS, A_DK, A_DV), F32)
    zero_c = jnp.zeros((bsz, A_CONV - 1, A_QKV), F32)
    zero_r = jnp.zeros((bsz, B_HEADS // 2, LANES, B_DV), F32)
    zero_p = jnp.zeros((bsz, POOL_MAX - 1, MIX_W), F32)
    keeps = [min(D_SPAN * dil, t_len) for dil in D_DILATIONS]
    win_bufs = [lax.empty((depth * bsz, keep * KV_ROWS, D_HD), F32) for keep in keeps]
    per_layer = []
    for i in range(depth):
        def mix_fn(proj3, ab3, i=i):
            ya, s_new = _mixer_delta(proj3, ab3, zero_c, zero_s, 0, lw['conv_w'][i], wts['par'][i],
                                     lw['a_gain'][i].reshape(1, A_DV), n_s=1, l_r=256, chunk=A_CHUNK, carry=True)
            yb, r_new = _mixer_ret(proj3, cos_t, sin_t, zero_r, n_s=1, l_r=512, chunk=B_CHUNK, carry=True)
            yc = _mixer_pool(proj3, zero_p, lw['c_w'][i].astype(BF16), lw['c_scale'][i].reshape(1, MIX_W),
                             n_s=1, l_r=1024, pos0=0)
            yd = _dilated_band(proj3, band_bias)
            return ya, yb, yc, yd, (s_new, r_new)

        h2, proj3, (s_new, r_new) = _layer_common(h2, p3, bsz, t_len, wts, lw, i,
                                                  mix_fn, 1024, i == depth - 1, g_final)
        for gi in range(N_DG):
            win_bufs[gi] = _win_extract(proj3, win_bufs[gi], i, gi, keeps[gi])
        per_layer.append((s_new,
                          proj3[:, t_len - (A_CONV - 1):, COL_AQKV:COL_AQKV + A_QKV],
                          r_new.reshape(bsz, B_HEADS, B_DK, B_DV),
                          proj3[:, t_len - (POOL_MAX - 1):, COL_CIN:COL_CIN + MIX_W]))
    stacked = tuple(jnp.stack([ns[j] for ns in per_layer]) for j in range(4))
    wins = tuple(win_bufs[gi].reshape(depth, bsz, keeps[gi], 2, D_HEADS, D_HD) for gi in range(N_DG))
    return h2.reshape(bsz, t_len, D_MODEL), stacked + wins


def _run_sample(x, p, states, wts, lw, t5_bias, g_final, depth):
    s_delta, s_conv, s_ret, s_pool, caches = states
    bsz, t_len, _ = x.shape
    n = bsz * t_len
    n_s = 16
    h2 = x.reshape(n, D_MODEL)
    p3 = p.reshape(depth, n, PLE_DIM)
    pos = PAST_LEN + jnp.arange(t_len, dtype=jnp.int32)
    cos_t, sin_t = _rope_tables(pos, n_s)
    cache_flat, out_bufs, step_bias = [], [], []
    for gi, dil in enumerate(D_DILATIONS):
        c = caches[gi]
        l_buf = c.shape[2]
        flat_shape = (c.shape[0] * bsz, l_buf * KV_ROWS, D_HD)
        cache_flat.append(c.reshape(flat_shape))
        out_bufs.append(lax.empty(flat_shape, F32))
        step_bias.append(_step_bias(t5_bias, l_buf, t_len, gi, dil))
    s_ret2 = s_ret.reshape(s_ret.shape[0], bsz, B_HEADS // 2, LANES, B_DV)
    per_layer = []
    for i in range(depth):
        def mix_fn(proj3, ab3, i=i):
            ya, s_new = _mixer_delta(proj3, ab3, s_conv[i], s_delta, i, lw['conv_w'][i], wts['par'][i],
                                     lw['a_gain'][i].reshape(1, A_DV), n_s=n_s, l_r=t_len, chunk=t_len, carry=False)
            yb, r_new = _mixer_ret(proj3, cos_t, sin_t, s_ret2[i], n_s=n_s, l_r=t_len, chunk=t_len, carry=False)
            yc = _mixer_pool(proj3, s_pool[i], lw['c_w'][i].astype(BF16), lw['c_scale'][i].reshape(1, MIX_W),
                             n_s=n_s, l_r=t_len, pos0=PAST_LEN)
            od, lse = [], []
            for gi, dil in enumerate(D_DILATIONS):
                o, l, out_bufs[gi] = _dilated_step(proj3, cache_flat[gi], out_bufs[gi], i, gi, *step_bias[gi])
                od.append(o)
                lse.append(l)
            return ya, yb, yc, (od, lse), (s_new, r_new)

        h2, proj3, (s_new, r_new) = _layer_common(h2, p3, bsz, t_len, wts, lw, i,
                                                  mix_fn, n, i == depth - 1, g_final)
        pool_new = jnp.concatenate([s_pool[i][:, t_len:], proj3[:, :, COL_CIN:COL_CIN + MIX_W]], axis=1)
        per_layer.append((s_new,
                          proj3[:, t_len - (A_CONV - 1):, COL_AQKV:COL_AQKV + A_QKV],
                          r_new.reshape(bsz, B_HEADS, B_DK, B_DV),
                          pool_new))
    stacked = tuple(jnp.stack([ns[j] for ns in per_layer]) for j in range(4))
    wins = tuple(out_bufs[gi].reshape(caches[gi].shape) for gi in range(N_DG))
    return h2.reshape(bsz, t_len, D_MODEL), stacked + wins


def kernel(x_prompt, x_sample, state_delta, state_delta_conv, state_ret, state_pool, cache_win0, cache_win1, cache_win2, p_prompt, p_sample, g_mix, w_in, conv_w, a_log, dt_bias, a_gain, c_w, c_scale, t5_bias, w_br, w_o, g_ffn, w_gate, w_up, w_down, g_ple, w_ple_gate, w_ple, g_final):
    depth = w_in.shape[0]
    wts = _prep_weights(w_in, w_br, w_o, w_gate, w_up, w_down, w_ple_gate, w_ple, a_log, dt_bias)
    lw = dict(g_mix=g_mix, conv_w=conv_w, a_gain=a_gain, c_w=c_w, c_scale=c_scale, g_ffn=g_ffn, g_ple=g_ple)
    y_p, (delta_p, conv_p, ret_p, pool_p, win0_p, win1_p, win2_p) = _run_prompt(
        x_prompt, p_prompt, wts, lw, t5_bias, g_final, depth)
    y_s, (delta_s, conv_s, ret_s, pool_s, win0_s, win1_s, win2_s) = _run_sample(
        x_sample, p_sample, (state_delta, state_delta_conv, state_ret, state_pool,
                             (cache_win0, cache_win1, cache_win2)), wts, lw, t5_bias, g_final, depth)
    return (y_p, y_s, delta_p, delta_s, conv_p, conv_s, ret_p, ret_s,
            pool_p, pool_s, win0_p, win0_s, win1_p, win1_s, win2_p, win2_s)
```

```python
import functools
import math

import numpy as np
import jax
import jax.numpy as jnp
from jax import lax
from jax.experimental import pallas as pl
from jax.experimental.pallas import tpu as pltpu

F32 = jnp.float32
BF16 = jnp.bfloat16
HIGHEST = lax.Precision.HIGHEST

D_MODEL = 1024
DEPTH = 4
PAST_LEN = 8192
MIX_W = D_MODEL // 2
N_BRANCH = 4
A_HEADS = 4
A_DK = MIX_W // A_HEADS
A_DV = MIX_W // A_HEADS
A_QKV = A_HEADS * (2 * A_DK + A_DV)
A_CONV = 4
A_CHUNK = 64
B_HEADS = 4
B_DV = MIX_W // B_HEADS
B_DK = B_DV // 2
B_CHUNK = 128
ROPE_BASE = 10000.0
C_GROUPS = 4
C_GW = MIX_W // C_GROUPS
POOL_WINDOWS = (2, 4, 8, 16)
POOL_MAX = 16
D_DILATIONS = (1, 4, 16)
N_DG = 3
D_SPAN = 128
D_HEADS = 4
D_HD = MIX_W // D_HEADS
REL_BUCKETS = 32
REL_MAX_DIST = D_SPAN * 16
D_FF = -(-8 * D_MODEL // (3 * 256)) * 256
PLE_DIM = 256
EPS = 1e-6

LANES = 128
SUBLANES = 8
VMEM_LIMIT_BYTES = 56 * 1024 * 1024

PROJ_W = 12800
COL_AQKV = 0
COL_AZ = 1536
COL_BQ = 2048
COL_BK = 2304
COL_BV = 2560
COL_BG = 3072
COL_CIN = 3584
COL_GATE = 4096
COL_DQKV = 8192
DBLK = D_HEADS * D_HD


def _cparams(sem):
    return pltpu.CompilerParams(dimension_semantics=sem, vmem_limit_bytes=VMEM_LIMIT_BYTES)


def _nt(a, b, precision=None):
    return lax.dot_general(a, b, (((1,), (1,)), ((), ())), precision=precision,
                           preferred_element_type=F32)


def _tn(a, b):
    return lax.dot_general(a, b, (((0,), (0,)), ((), ())), preferred_element_type=F32)


def _mm(a, b, precision=None):
    return jnp.dot(a, b, precision=precision, preferred_element_type=F32)


def _sigmoid(x):
    return 1.0 / (1.0 + jnp.exp(-x))


def _silu(x):
    return x * _sigmoid(x)


def _rms_rows(x, g):
    return x * lax.rsqrt(jnp.mean(x * x, axis=-1, keepdims=True) + EPS) * g


def _norm_matmul_kernel(x_ref, g_ref, w_ref, wab_ref, o_ref, ab_ref, u_ref):
    @pl.when(pl.program_id(1) == 0)
    def _():
        u_ref[...] = _rms_rows(x_ref[...], g_ref[...]).astype(BF16)
        ab_ref[...] = _mm(u_ref[...], wab_ref[...])

    o_ref[...] = _mm(u_ref[...], w_ref[...])


def _norm_matmul(x, g, w, w_ab, layer, tm, tn):
    n, d = x.shape
    wcols = w.shape[2]
    return pl.pallas_call(
        _norm_matmul_kernel,
        grid=(n // tm, wcols // tn),
        in_specs=[pl.BlockSpec((tm, d), lambda i, j: (i, 0)),
                  pl.BlockSpec((1, d), lambda i, j: (0, 0)),
                  pl.BlockSpec((None, d, tn), lambda i, j: (layer, 0, j)),
                  pl.BlockSpec((None, d, LANES), lambda i, j: (layer, 0, 0))],
        out_specs=[pl.BlockSpec((tm, tn), lambda i, j: (i, j)),
                   pl.BlockSpec((tm, LANES), lambda i, j: (i, 0))],
        out_shape=[jax.ShapeDtypeStruct((n, wcols), F32),
                   jax.ShapeDtypeStruct((n, LANES), F32)],
        scratch_shapes=[pltpu.VMEM((tm, d), BF16)],
        compiler_params=_cparams(("parallel", "arbitrary")),
        name="norm_matmul",
    )(x, g.reshape(1, d), w, w_ab)


def _delta_kernel(qkv_ref, z_ref, ab_ref, cprev_ref, sprev_ref, convw_ref, par_ref, gain_ref,
                  *rest, chunk, carry, n_host, host_chunks):
    host_in = rest[:3 * n_host]
    o_ref, snew_ref = rest[3 * n_host:3 * n_host + 2]
    host_out = rest[3 * n_host + 2:4 * n_host + 2]
    ext_ref, s_ref = rest[4 * n_host + 2:]
    j = pl.program_id(1)
    for g in range(n_host):
        src, nxt, dst = host_in[3 * g], host_in[3 * g + 1], host_out[g]
        rows, tail = src.shape[0], nxt.shape[0]
        dst[0:rows - tail, :] = src[tail:rows, :]
        last = (j % host_chunks) == host_chunks - 1
        dst[rows - tail:rows, :] = jnp.where(last, 0.0, nxt[...])
    n_s, l_r, _ = qkv_ref.shape
    t_b = n_s * l_r
    n_c = t_b // chunk
    hdr = SUBLANES
    pre = A_CONV - 1

    @pl.when(j == 0)
    def _():
        ext_ref[:, 0:hdr, :] = jnp.zeros((n_s, hdr, A_QKV), F32)
        ext_ref[:, hdr - pre:hdr, :] = cprev_ref[...]
        if carry:
            s_ref[...] = sprev_ref[0]

    if carry:
        @pl.when(j > 0)
        def _():
            ext_ref[:, 0:hdr, :] = ext_ref[:, l_r:l_r + hdr, :]

    ext_ref[:, hdr:hdr + l_r, :] = qkv_ref[...]
    cw = convw_ref[...]

    def conv_cols(c0):
        acc = None
        for t in range(A_CONV):
            term = (ext_ref[:, hdr - pre + t:hdr - pre + t + l_r, c0:c0 + LANES]
                    * cw[t:t + 1, c0:c0 + LANES])
            acc = term if acc is None else acc + term
        return _silu(acc).reshape(t_b, LANES)

    ab = ab_ref[...].reshape(t_b, LANES)
    par = par_ref[...]
    beta_all = _sigmoid(ab)
    xs = ab + par[1:2, :]
    softplus = jnp.maximum(xs, 0.0) + jnp.log1p(jnp.exp(-jnp.abs(xs)))
    g_all = -jnp.exp(par[0:1, :]) * softplus

    sh = int(math.log2(chunk))
    ri = lax.broadcasted_iota(jnp.int32, (t_b, t_b), 0)
    ci = lax.broadcasted_iota(jnp.int32, (t_b, t_b), 1)
    same = lax.shift_right_logical(ri, sh) == lax.shift_right_logical(ci, sh)
    causal = jnp.logical_and(same, ci <= ri)
    strict = jnp.logical_and(same, ci < ri)
    gc_all = _mm(causal.astype(F32), g_all, precision=HIGHEST)
    sel = (lax.broadcasted_iota(jnp.int32, (SUBLANES, LANES), 0)
           == lax.broadcasted_iota(jnp.int32, (SUBLANES, LANES), 1)).astype(F32)
    gc_rows = _nt(sel, gc_all, precision=HIGHEST)
    glast_all = _mm(same.astype(F32), g_all, precision=HIGHEST)

    gain = gain_ref[...]
    z = z_ref[...].reshape(t_b, MIX_W)
    row_id = lax.broadcasted_iota(jnp.int32, (t_b, 1), 0)
    heads = range(A_HEADS)

    gcc = [gc_all[:, A_HEADS + h:A_HEADS + h + 1] for h in heads]
    gcr = [gc_rows[A_HEADS + h:A_HEADS + h + 1, :] for h in heads]
    glast = [glast_all[:, A_HEADS + h:A_HEADS + h + 1] for h in heads]
    bcol = [beta_all[:, h:h + 1] for h in heads]
    q = [conv_cols(h * A_DK) for h in heads]
    k = [conv_cols(A_HEADS * A_DK + h * A_DK) for h in heads]
    v = [conv_cols(2 * A_HEADS * A_DK + h * A_DV) for h in heads]
    q = [x * lax.rsqrt(jnp.sum(x * x, axis=-1, keepdims=True) + EPS) * (A_DK ** -0.5) for x in q]
    k = [x * lax.rsqrt(jnp.sum(x * x, axis=-1, keepdims=True) + EPS) for x in k]
    kb = [x.astype(BF16) for x in k]
    kk = [_nt(kb[h], kb[h]) for h in heads]
    qkr = [_nt(q[h].astype(BF16), kb[h]) for h in heads]
    gam = [jnp.where(causal, jnp.exp(jnp.where(causal, gcc[h] - gcr[h], 0.0)), 0.0) for h in heads]
    qk = [(qkr[h] * gam[h]).astype(BF16) for h in heads]
    a = [jnp.where(strict, bcol[h] * kk[h] * gam[h], 0.0) for h in heads]
    r = [-x for x in a]
    p = a
    for _ in range(sh - 1):
        pb = [x.astype(BF16) for x in p]
        p = [_mm(x, x) for x in pb]
        pr = [_mm(p[h].astype(BF16), r[h].astype(BF16)) for h in heads]
        r = [r[h] + p[h] + pr[h] for h in heads]
    ecol = [jnp.exp(x) for x in gcc]
    rhs = [jnp.concatenate([v[h] * bcol[h], k[h] * (bcol[h] * ecol[h])], axis=1) for h in heads]
    sol = [rhs[h] + _mm(r[h].astype(BF16), rhs[h].astype(BF16)) for h in heads]
    u = [x[:, :A_DV] for x in sol]
    w = [x[:, A_DV:] for x in sol]
    qe = [q[h] * ecol[h] for h in heads]
    kd = [(k[h] * jnp.exp(glast[h] - gcc[h])).astype(BF16) for h in heads]
    sdec = [jnp.exp(x) for x in glast]
    vnews = [[] for _ in heads]
    qss = [[] for _ in heads]
    if carry:
        s = [s_ref[h] for h in heads]
        for c in range(n_c):
            r0 = c * chunk
            wq = [jnp.concatenate([w[h][r0:r0 + chunk], qe[h][r0:r0 + chunk]], axis=0).astype(BF16)
                  for h in heads]
            t = [_mm(wq[h], s[h].astype(BF16)) for h in heads]
            vnew = [u[h][r0:r0 + chunk] - t[h][:chunk] for h in heads]
            upd = [_tn(kd[h][r0:r0 + chunk], vnew[h].astype(BF16)) for h in heads]
            s = [s[h] * sdec[h][r0:r0 + 1, :] + upd[h] for h in heads]
            for h in heads:
                vnews[h].append(vnew[h])
                qss[h].append(t[h][chunk:])
        vn_all = [jnp.concatenate(x, axis=0) if n_c > 1 else x[0] for x in vnews]
        vnb = [x.astype(BF16) for x in vn_all]
    else:
        olds = [[] for _ in heads]
        for c in range(n_c):
            r0 = c * chunk
            for h in heads:
                s_c = sprev_ref[c, h]
                olds[h].append(s_c)
                wq = jnp.concatenate([w[h][r0:r0 + chunk], qe[h][r0:r0 + chunk]], axis=0).astype(BF16)
                t = _mm(wq, s_c.astype(BF16))
                vnews[h].append(u[h][r0:r0 + chunk] - t[:chunk])
                qss[h].append(t[chunk:])
        vn_all = [jnp.concatenate(x, axis=0) for x in vnews]
        vnb = [x.astype(BF16) for x in vn_all]
        for c in range(n_c):
            r0 = c * chunk
            in_c = jnp.logical_and(row_id >= r0, row_id < r0 + chunk)
            for h in heads:
                upd = _tn(jnp.where(in_c, kd[h], jnp.zeros_like(kd[h])),
                          jnp.where(in_c, vnb[h], jnp.zeros_like(vnb[h])))
                snew_ref[c, h] = olds[h][c] * sdec[h][r0:r0 + 1, :] + upd
    outs = []
    for h in heads:
        qs_all = jnp.concatenate(qss[h], axis=0) if n_c > 1 else qss[h][0]
        o = qs_all + _mm(qk[h], vnb[h])
        o = o * lax.rsqrt(jnp.mean(o * o, axis=-1, keepdims=True) + EPS) * gain
        outs.append(o * _silu(z[:, h * A_DV:(h + 1) * A_DV]))
    o_ref[...] = jnp.concatenate(outs, axis=1).reshape(n_s, l_r, MIX_W)

    if carry:
        for h in heads:
            s_ref[h] = s[h]

        @pl.when(j == pl.num_programs(1) - 1)
        def _():
            for h in heads:
                snew_ref[0, h] = s[h]


def _mixer_delta(proj3, ab3, conv_prev, s_prev, layer, conv_w, par, gain, *, n_s, l_r, chunk, carry,
                 host=None):
    bsz, t_len, _ = proj3.shape
    grid = (bsz // n_s, t_len // l_r)
    host_specs, host_args, host_out_specs, host_out_shapes, aliases = [], [], [], [], {}
    n_host, host_chunks = 0, 1
    if host is not None:
        bufs, dsts, slab0, n_slabs, tail = host
        n_host = len(bufs)
        steps = grid[0] * grid[1]
        host_chunks = steps // n_slabs
        per_b = grid[1] // host_chunks
        assert steps == host_chunks * n_slabs and grid[1] == per_b * host_chunks
        for g, (buf, dst) in enumerate(zip(bufs, dsts)):
            rows = buf.shape[1] // host_chunks
            per, last = rows // tail, buf.shape[1] // tail - 1
            assert rows % tail == 0 and rows * host_chunks == buf.shape[1]
            slab = lambda b, j: slab0 + b * per_b + j // host_chunks
            host_specs += [
                pl.BlockSpec((None, rows, LANES), lambda b, j, slab=slab: (slab(b, j), j % host_chunks, 0)),
                pl.BlockSpec((None, tail, LANES), lambda b, j, slab=slab, per=per, last=last: (
                    slab(b, j), jnp.minimum((j % host_chunks + 1) * per, last), 0)),
                pl.BlockSpec(memory_space=pl.ANY)]
            host_args += [buf, buf, dst]
            host_out_specs.append(
                pl.BlockSpec((None, rows, LANES), lambda b, j, slab=slab: (slab(b, j), j % host_chunks, 0)))
            host_out_shapes.append(jax.ShapeDtypeStruct(dst.shape, F32))
            aliases[8 + 3 * g + 2] = 2 + g
    kern = functools.partial(_delta_kernel, chunk=chunk, carry=carry, n_host=n_host, host_chunks=host_chunks)
    return pl.pallas_call(
        kern,
        grid=grid,
        in_specs=[
            pl.BlockSpec((n_s, l_r, A_QKV), lambda b, j: (b, j, COL_AQKV // A_QKV)),
            pl.BlockSpec((n_s, l_r, MIX_W), lambda b, j: (b, j, COL_AZ // MIX_W)),
            pl.BlockSpec((n_s, l_r, LANES), lambda b, j: (b, j, 0)),
            pl.BlockSpec((n_s, A_CONV - 1, A_QKV), lambda b, j: (b, 0, 0)),
            pl.BlockSpec((None, n_s, A_HEADS, A_DK, A_DV), lambda b, j: (layer, b, 0, 0, 0)),
            pl.BlockSpec((A_CONV, A_QKV), lambda b, j: (0, 0)),
            pl.BlockSpec((SUBLANES, LANES), lambda b, j: (0, 0)),
            pl.BlockSpec((1, A_DV), lambda b, j: (0, 0)),
        ] + host_specs,
        out_specs=[
            pl.BlockSpec((n_s, l_r, MIX_W), lambda b, j: (b, j, 0)),
            pl.BlockSpec((n_s, A_HEADS, A_DK, A_DV), lambda b, j: (b, 0, 0, 0)),
        ] + host_out_specs,
        out_shape=[jax.ShapeDtypeStruct((bsz, t_len, MIX_W), F32),
                   jax.ShapeDtypeStruct((bsz, A_HEADS, A_DK, A_DV), F32)] + host_out_shapes,
        scratch_shapes=[pltpu.VMEM((n_s, SUBLANES + l_r + SUBLANES, A_QKV), F32),
                        pltpu.VMEM((A_HEADS, A_DK, A_DV), F32)],
        input_output_aliases=aliases,
        compiler_params=_cparams(("parallel", "arbitrary")),
        name="mixer_delta",
    )(proj3, proj3, ab3, conv_prev, s_prev, conv_w, par, gain, *host_args)


def _ret_log_gamma(h):
    return math.log1p(-(2.0 ** (-5.0 - h)))


def _ret_kernel(q_ref, k_ref, v_ref, g_ref, cos_ref, sin_ref, rprev_ref,
                o_ref, rnew_ref, r_ref, *, chunk, carry):
    j = pl.program_id(1)
    n_s, l_r, _ = q_ref.shape
    t_b = n_s * l_r
    n_c = t_b // chunk
    half = B_DK // 2

    if carry:
        @pl.when(j == 0)
        def _():
            r_ref[...] = rprev_ref[0]

    cos = cos_ref[...]
    sin = sin_ref[...]
    lane = lax.broadcasted_iota(jnp.int32, (1, LANES), 1)
    first_half = (lane % B_DK) < half
    lane_head = lane // B_DK

    def rope(x):
        swapped = jnp.where(first_half, pltpu.roll(x, LANES - half, axis=1),
                            pltpu.roll(x, half, axis=1))
        return x * cos + swapped * sin

    idx_i = lax.broadcasted_iota(jnp.int32, (chunk, chunk), 0)
    idx_j = lax.broadcasted_iota(jnp.int32, (chunk, chunk), 1)
    pos_col = lax.broadcasted_iota(jnp.int32, (chunk, 1), 0).astype(F32)
    row_head = lax.broadcasted_iota(jnp.int32, (LANES, 1), 0) // B_DK
    gate = g_ref[...].reshape(t_b, MIX_W)
    vv = v_ref[...].reshape(t_b, MIX_W)

    small = chunk < 2 * SUBLANES
    if small:
        sh = int(math.log2(chunk))
        bi = lax.broadcasted_iota(jnp.int32, (t_b, t_b), 0)
        bj = lax.broadcasted_iota(jnp.int32, (t_b, t_b), 1)
        bsame = lax.shift_right_logical(bi, sh) == lax.shift_right_logical(bj, sh)
        bcausal = jnp.logical_and(bsame, bj <= bi)
        bdiff = (bi - bj).astype(F32)
        row_id = lax.broadcasted_iota(jnp.int32, (t_b, 1), 0)
        pos_in = (row_id & (chunk - 1)).astype(F32)

    outs = [None] * B_HEADS
    for p in range(B_HEADS // 2):
        q2 = rope(q_ref[...].reshape(t_b, B_HEADS * B_DK)[:, p * LANES:(p + 1) * LANES])
        k2 = rope(k_ref[...].reshape(t_b, B_HEADS * B_DK)[:, p * LANES:(p + 1) * LANES]) * (B_DK ** -0.5)
        lgs = [_ret_log_gamma(2 * p + hh) for hh in range(2)]
        qh = [jnp.where(lane_head == hh, q2, 0.0) for hh in range(2)]
        kh = [jnp.where(lane_head == hh, k2, 0.0) for hh in range(2)]
        vh = [vv[:, (2 * p + hh) * B_DV:(2 * p + hh + 1) * B_DV] for hh in range(2)]
        row_scale = jnp.where(row_head == 0, math.exp(lgs[0] * chunk), math.exp(lgs[1] * chunk))
        if small:
            intra = []
            for hh in range(2):
                dec = jnp.where(bcausal, jnp.exp(lgs[hh] * jnp.where(bcausal, bdiff, 0.0)), 0.0)
                sc = _nt(qh[hh].astype(BF16), kh[hh].astype(BF16)) * dec
                intra.append(_mm(sc.astype(BF16), vh[hh].astype(BF16)))
            cross = [[], []]
            for c in range(n_c):
                r0 = c * chunk
                rp = r_ref[p] if carry else rprev_ref[c, p]
                qq = jnp.concatenate([qh[0][r0:r0 + chunk], qh[1][r0:r0 + chunk]], axis=0)
                t = _mm(qq.astype(BF16), rp.astype(BF16))
                cross[0].append(t[:chunk])
                cross[1].append(t[chunk:])
                in_c = jnp.logical_and(row_id >= r0, row_id < r0 + chunk)
                upd = None
                for hh in range(2):
                    kd = jnp.where(in_c, kh[hh] * jnp.exp(lgs[hh] * (chunk - 1.0 - pos_in)), 0.0)
                    term = _tn(kd.astype(BF16), vh[hh].astype(BF16))
                    upd = term if upd is None else upd + term
                rn = rp * row_scale + upd
                if carry:
                    r_ref[p] = rn
                else:
                    rnew_ref[c, p] = rn
            for hh in range(2):
                cr = jnp.concatenate(cross[hh], axis=0) * jnp.exp(lgs[hh] * (pos_in + 1.0))
                outs[2 * p + hh] = intra[hh] + cr
        else:
            pieces = [[], []]
            for c in range(n_c):
                r0 = c * chunk
                rp = r_ref[p] if carry else rprev_ref[c, p]
                rpb = rp.astype(BF16)
                upd = None
                for hh in range(2):
                    qc = qh[hh][r0:r0 + chunk].astype(BF16)
                    kc = kh[hh][r0:r0 + chunk]
                    vc = vh[hh][r0:r0 + chunk].astype(BF16)
                    causal = idx_j <= idx_i
                    dec = jnp.where(causal, jnp.exp(lgs[hh] * jnp.where(causal, (idx_i - idx_j).astype(F32), 0.0)), 0.0)
                    sc = _nt(qc, kc.astype(BF16)) * dec
                    oc = _mm(sc.astype(BF16), vc) + _mm(qc, rpb) * jnp.exp(lgs[hh] * (pos_col + 1.0))
                    pieces[hh].append(oc)
                    kd = kc * jnp.exp(lgs[hh] * (chunk - 1.0 - pos_col))
                    term = _tn(kd.astype(BF16), vc)
                    upd = term if upd is None else upd + term
                rn = rp * row_scale + upd
                if carry:
                    r_ref[p] = rn
                else:
                    rnew_ref[c, p] = rn
            for hh in range(2):
                outs[2 * p + hh] = jnp.concatenate(pieces[hh], axis=0) if n_c > 1 else pieces[hh][0]

    res = []
    for h in range(B_HEADS):
        o = outs[h]
        o = o * lax.rsqrt(jnp.mean(o * o, axis=-1, keepdims=True) + EPS)
        res.append(o * _silu(gate[:, h * B_DV:(h + 1) * B_DV]))
    o_ref[...] = jnp.concatenate(res, axis=1).reshape(n_s, l_r, MIX_W)

    if carry:
        @pl.when(j == pl.num_programs(1) - 1)
        def _():
            rnew_ref[0] = r_ref[...]


def _mixer_ret(proj3, cos_t, sin_t, r_prev, *, n_s, l_r, chunk, carry):
    bsz, t_len, _ = proj3.shape
    t_b = n_s * l_r
    grid = (bsz // n_s, t_len // l_r)
    n_tab = cos_t.shape[0] // t_b
    kern = functools.partial(_ret_kernel, chunk=chunk, carry=carry)
    qw = B_HEADS * B_DK
    return pl.pallas_call(
        kern,
        grid=grid,
        in_specs=[
            pl.BlockSpec((n_s, l_r, qw), lambda b, j: (b, j, COL_BQ // qw)),
            pl.BlockSpec((n_s, l_r, qw), lambda b, j: (b, j, COL_BK // qw)),
            pl.BlockSpec((n_s, l_r, MIX_W), lambda b, j: (b, j, COL_BV // MIX_W)),
            pl.BlockSpec((n_s, l_r, MIX_W), lambda b, j: (b, j, COL_BG // MIX_W)),
            pl.BlockSpec((t_b, LANES), lambda b, j: (j % n_tab, 0)),
            pl.BlockSpec((t_b, LANES), lambda b, j: (j % n_tab, 0)),
            pl.BlockSpec((n_s, B_HEADS // 2, LANES, B_DV), lambda b, j: (b, 0, 0, 0)),
        ],
        out_specs=[
            pl.BlockSpec((n_s, l_r, MIX_W), lambda b, j: (b, j, 0)),
            pl.BlockSpec((n_s, B_HEADS // 2, LANES, B_DV), lambda b, j: (b, 0, 0, 0)),
        ],
        out_shape=[jax.ShapeDtypeStruct((bsz, t_len, MIX_W), F32),
                   jax.ShapeDtypeStruct((bsz, B_HEADS // 2, LANES, B_DV), F32)],
        scratch_shapes=[pltpu.VMEM((B_HEADS // 2, LANES, B_DV), F32)],
        compiler_params=_cparams(("parallel", "arbitrary")),
        name="mixer_ret",
    )(proj3, proj3, proj3, proj3, cos_t, sin_t, r_prev)


def _rope_tables(pos, reps):
    half = B_DK // 2
    inv = ROPE_BASE ** (-jnp.arange(half, dtype=F32) / half)
    ang = pos.astype(F32)[:, None] * inv[None, :]
    cos = jnp.cos(ang)
    sin = jnp.sin(ang)
    cos_t = jnp.concatenate([cos, cos, cos, cos], axis=1)
    sin_t = jnp.concatenate([-sin, sin, -sin, sin], axis=1)
    return jnp.tile(cos_t, (reps, 1)), jnp.tile(sin_t, (reps, 1))


def _pool_kernel(x_ref, prev_ref, cw_ref, cs_ref, o_ref, ext_ref, *, pos0):
    j = pl.program_id(1)
    n_s, l_r, _ = x_ref.shape
    t_b = n_s * l_r
    hdr = POOL_MAX

    @pl.when(j == 0)
    def _():
        ext_ref[:, 0:hdr, :] = jnp.zeros((n_s, hdr, MIX_W), F32)
        ext_ref[:, 1:hdr, :] = prev_ref[...]

    @pl.when(j > 0)
    def _():
        ext_ref[:, 0:hdr, :] = ext_ref[:, l_r:l_r + hdr, :]

    ext_ref[:, hdr:hdr + l_r, :] = x_ref[...]
    pos = pos0 + j * l_r + lax.broadcasted_iota(jnp.int32, (1, l_r, 1), 1)
    outs = []
    for gi, w in enumerate(POOL_WINDOWS):
        c0 = gi * C_GW
        acc = None
        for t in range(w):
            term = ext_ref[:, hdr - t:hdr - t + l_r, c0:c0 + C_GW]
            acc = term if acc is None else acc + term
        cnt = jnp.minimum(pos + 1, w).astype(F32)
        y = acc / cnt - ext_ref[:, hdr:hdr + l_r, c0:c0 + C_GW]
        y = _mm(y.reshape(t_b, C_GW).astype(BF16), cw_ref[gi])
        outs.append(y)
    o_ref[...] = (jnp.concatenate(outs, axis=1) * cs_ref[...]).reshape(n_s, l_r, MIX_W)


def _mixer_pool(proj3, prev, c_w, c_scale, *, n_s, l_r, pos0):
    bsz, t_len, _ = proj3.shape
    grid = (bsz // n_s, t_len // l_r)
    kern = functools.partial(_pool_kernel, pos0=pos0)
    return pl.pallas_call(
        kern,
        grid=grid,
        in_specs=[
            pl.BlockSpec((n_s, l_r, MIX_W), lambda b, j: (b, j, COL_CIN // MIX_W)),
            pl.BlockSpec((n_s, POOL_MAX - 1, MIX_W), lambda b, j: (b, 0, 0)),
            pl.BlockSpec((C_GROUPS, C_GW, C_GW), lambda b, j: (0, 0, 0)),
            pl.BlockSpec((1, MIX_W), lambda b, j: (0, 0)),
        ],
        out_specs=pl.BlockSpec((n_s, l_r, MIX_W), lambda b, j: (b, j, 0)),
        out_shape=jax.ShapeDtypeStruct((bsz, t_len, MIX_W), F32),
        scratch_shapes=[pltpu.VMEM((n_s, POOL_MAX + l_r + SUBLANES, MIX_W), F32)],
        compiler_params=_cparams(("parallel", "arbitrary")),
        name="mixer_pool",
    )(proj3, prev, c_w, c_scale)


def _lookup_kernel(oh_ref, tab_ref, o_ref):
    o_ref[...] = _mm(oh_ref[...], tab_ref[...], precision=HIGHEST)


def _bias_lookup(idx, table):
    n_valid = idx.shape[0]
    tr = 4096
    n_rows = -(-n_valid // tr) * tr
    idx = np.concatenate([idx, np.zeros((n_rows - n_valid,), idx.dtype)])
    onehot = jnp.asarray(idx[:, None] == np.arange(REL_BUCKETS)[None, :], dtype=F32)
    tab = jnp.pad(table, ((0, 0), (0, LANES - table.shape[1])))
    out = pl.pallas_call(
        _lookup_kernel,
        grid=(n_rows // tr,),
        in_specs=[pl.BlockSpec((tr, REL_BUCKETS), lambda i: (i, 0)),
                  pl.BlockSpec((REL_BUCKETS, LANES), lambda i: (0, 0))],
        out_specs=pl.BlockSpec((tr, LANES), lambda i: (i, 0)),
        out_shape=jax.ShapeDtypeStruct((n_rows, LANES), F32),
        compiler_params=_cparams(("parallel",)),
        name="bias_lookup",
    )(onehot, tab)
    return out[:n_valid, :table.shape[1]]


def _t5_bucket_np(dist):
    exact = REL_BUCKETS // 2
    n = np.maximum(dist, 0)
    nf = np.maximum(n, 1).astype(np.float32)
    large = exact + (np.log(nf / np.float32(exact)) / np.float32(math.log(REL_MAX_DIST / exact))
                     * np.float32(REL_BUCKETS - exact)).astype(np.int32)
    large = np.minimum(large, REL_BUCKETS - 1)
    return np.where(n < exact, n, large).astype(np.int32)


SUPER = D_SPAN * D_DILATIONS[-1]
TILES = SUPER // D_SPAN


def _band_fused_kernel(*refs):
    ins = refs[:5 * N_DG]
    bias_ref = refs[5 * N_DG]
    o_ref = refs[5 * N_DG + 1]
    scr = refs[5 * N_DG + 2:]
    kf = scr[0:2 * N_DG:2]
    vf = scr[1:2 * N_DG:2]
    og = scr[2 * N_DG:3 * N_DG]
    lg = scr[3 * N_DG:4 * N_DG]
    has_prev = pl.program_id(2) > 0
    scale = D_HD ** -0.5
    for g, dil in enumerate(D_DILATIONS):
        q_ref, k_ref, v_ref, kp_ref, vp_ref = ins[5 * g:5 * g + 5]
        span = D_SPAN * dil
        kf[g][0:span, :] = kp_ref[...]
        kf[g][span:span + SUPER, :] = k_ref[...]
        vf[g][0:span, :] = vp_ref[...]
        vf[g][span:span + SUPER, :] = v_ref[...]
        sh = int(math.log2(dil))

        def tile(idx, carry, g=g, dil=dil, span=span, sh=sh, q_ref=q_ref):
            qi = lax.shift_right_logical(idx, sh)
            res = idx & (dil - 1)
            start = qi * span + res
            if dil == 1:
                start = pl.multiple_of(start, D_SPAN)
                rows = pl.ds(start, D_SPAN)
                krows = pl.ds(start, 2 * D_SPAN)
            else:
                rows = pl.ds(start, D_SPAN, stride=dil)
                krows = pl.ds(start, 2 * D_SPAN, stride=dil)
            q = q_ref[rows, :].astype(BF16)
            kk = kf[g][krows, :].astype(BF16)
            vv = vf[g][krows, :].astype(BF16)
            which = jnp.where(jnp.logical_or(has_prev, qi > 0), 1, 0)
            s = _nt(q, kk) * scale + bias_ref[g, which]
            m = jnp.max(s, axis=-1, keepdims=True)
            p = jnp.exp(s - m)
            l = jnp.sum(p, axis=-1, keepdims=True)
            og[g][rows, :] = _mm(p.astype(BF16), vv) / l
            lg[g][rows, :] = jnp.broadcast_to(m + jnp.log(l), (D_SPAN, D_HD))
            return carry

        lax.fori_loop(0, TILES, tile, 0, unroll=8)

    def merge(c, carry):
        rows = pl.ds(pl.multiple_of(c * D_SPAN, D_SPAN), D_SPAN)
        l0, l1, l2 = lg[0][rows, :], lg[1][rows, :], lg[2][rows, :]
        m = jnp.maximum(jnp.maximum(l0, l1), l2)
        e0, e1, e2 = jnp.exp(l0 - m), jnp.exp(l1 - m), jnp.exp(l2 - m)
        inv = 1.0 / (e0 + e1 + e2)
        o_ref[rows, :] = (e0 * og[0][rows, :] + e1 * og[1][rows, :] + e2 * og[2][rows, :]) * inv
        return carry

    lax.fori_loop(0, TILES, merge, 0)


def _dilated_band(proj3, band_bias):
    bsz, s_len, _ = proj3.shape
    n_blk = s_len // SUPER
    base = COL_DQKV // D_HD
    in_specs = []
    args = []
    for g, dil in enumerate(D_DILATIONS):
        span = D_SPAN * dil
        per = SUPER // span

        def col(c, g=g):
            return lambda b, h, n: (b, n, base + (c * N_DG + g) * D_HEADS + h)

        def col_prev(c, g=g, per=per):
            return lambda b, h, n: (b, jnp.maximum(n * per - 1, 0), base + (c * N_DG + g) * D_HEADS + h)

        in_specs += [pl.BlockSpec((None, SUPER, D_HD), col(0)),
                     pl.BlockSpec((None, SUPER, D_HD), col(1)),
                     pl.BlockSpec((None, SUPER, D_HD), col(2)),
                     pl.BlockSpec((None, span, D_HD), col_prev(1)),
                     pl.BlockSpec((None, span, D_HD), col_prev(2))]
        args += [proj3] * 5
    in_specs.append(pl.BlockSpec((N_DG, None, 2, D_SPAN, 2 * D_SPAN), lambda b, h, n: (0, h, 0, 0, 0)))
    scratch = []
    for dil in D_DILATIONS:
        scratch += [pltpu.VMEM((D_SPAN * dil + SUPER, D_HD), F32)] * 2
    scratch += [pltpu.VMEM((SUPER, D_HD), F32)] * (2 * N_DG)
    return pl.pallas_call(
        _band_fused_kernel,
        grid=(bsz, D_HEADS, n_blk),
        in_specs=in_specs,
        out_specs=pl.BlockSpec((None, SUPER, D_HD), lambda b, h, n: (b, n, h)),
        out_shape=jax.ShapeDtypeStruct((bsz, s_len, MIX_W), F32),
        scratch_shapes=scratch,
        compiler_params=_cparams(("parallel", "parallel", "arbitrary")),
        name="dilated_band",
    )(*args, band_bias)


KV_ROWS = 2 * D_HEADS


def _step_attn_kernel(q_ref, kn_ref, vn_ref, cache_ref, bias_c_ref, mask_c_ref, bias_n_ref, mask_n_ref,
                      buf_ref, o_ref, lse_ref, new_ref, padk_ref, padv_ref, *, grouped):
    del buf_ref
    t_new = q_ref.shape[0]
    scale = D_HD ** -0.5
    zeros_q = jnp.zeros((SUBLANES, D_HD), F32)
    heads = range(D_HEADS)
    cols = [slice(h * D_HD, (h + 1) * D_HD) for h in heads]

    def pad16(p):
        return jnp.concatenate([p, jnp.zeros_like(p)], axis=0).astype(BF16)

    padk_ref[...] = jnp.zeros(padk_ref.shape, F32)
    padv_ref[...] = jnp.zeros(padv_ref.shape, F32)
    for h in heads:
        new_ref[pl.ds(h, t_new, stride=KV_ROWS), :] = kn_ref[:, cols[h]]
        new_ref[pl.ds(D_HEADS + h, t_new, stride=KV_ROWS), :] = vn_ref[:, cols[h]]
        padk_ref[h, 0:t_new, :] = kn_ref[:, cols[h]]
        padv_ref[h, 0:t_new, :] = vn_ref[:, cols[h]]

    if grouped:
        n_grp, kept = cache_ref.shape[0], cache_ref.shape[1] // KV_ROWS
        kh = [cache_ref[:, pl.ds(h, kept, stride=KV_ROWS), :].reshape(n_grp * kept, D_HD).astype(BF16)
              for h in heads]
        vh = [cache_ref[:, pl.ds(D_HEADS + h, kept, stride=KV_ROWS), :].reshape(n_grp * kept, D_HD).astype(BF16)
              for h in heads]
    else:
        n_pos = cache_ref.shape[0] // KV_ROWS
        kh = [cache_ref[pl.ds(h, n_pos, stride=KV_ROWS), :].astype(BF16) for h in heads]
        vh = [cache_ref[pl.ds(D_HEADS + h, n_pos, stride=KV_ROWS), :].astype(BF16) for h in heads]
    lhs = [jnp.concatenate([q_ref[:, cols[h]], zeros_q], axis=0).astype(BF16) for h in heads]
    s_n = [_nt(lhs[h], padk_ref[h].astype(BF16))[0:t_new] * scale + bias_n_ref[h] + mask_n_ref[...]
           for h in heads]
    s_c = [_nt(lhs[h], kh[h])[0:t_new] * scale + bias_c_ref[h] + mask_c_ref[...] for h in heads]
    m = [jnp.maximum(jnp.max(s_n[h], axis=-1, keepdims=True), jnp.max(s_c[h], axis=-1, keepdims=True))
         for h in heads]
    p_n = [jnp.exp(s_n[h] - m[h]) for h in heads]
    p_c = [jnp.exp(s_c[h] - m[h]) for h in heads]
    l = [jnp.sum(p_n[h], axis=-1, keepdims=True) + jnp.sum(p_c[h], axis=-1, keepdims=True) for h in heads]
    acc = [_mm(pad16(p_n[h]), padv_ref[h].astype(BF16))[0:t_new] + _mm(pad16(p_c[h]), vh[h])[0:t_new]
           for h in heads]
    for h in heads:
        o_ref[:, cols[h]] = acc[h] / l[h]
        lse_ref[:, h:h + 1] = m[h] + jnp.log(l[h])


def _dilated_step(proj3, cache_flat, out_buf, layer, gi, dil, bias_c, mask_c, bias_n, mask_n):
    bsz, t_new, _ = proj3.shape
    flat_rows = cache_flat.shape[1]
    l_buf = flat_rows // KV_ROWS
    shift = t_new * KV_ROWS
    base = COL_DQKV // DBLK
    cq, ck, cv = base + gi, base + N_DG + gi, base + 2 * N_DG + gi
    grouped = t_new < dil
    if grouped:
        cache_arg = cache_flat.reshape(cache_flat.shape[0], l_buf // dil, dil * KV_ROWS, D_HD)
        cache_spec = pl.BlockSpec((None, l_buf // dil, shift, D_HD), lambda b: (layer * bsz + b, 0, 0, 0))
    else:
        cache_arg = cache_flat
        cache_spec = pl.BlockSpec((None, flat_rows, D_HD), lambda b: (layer * bsz + b, 0, 0))
    n_keys = bias_c.shape[-1]
    return pl.pallas_call(
        functools.partial(_step_attn_kernel, grouped=grouped),
        grid=(bsz,),
        in_specs=[
            pl.BlockSpec((None, t_new, DBLK), lambda b: (b, 0, cq)),
            pl.BlockSpec((None, t_new, DBLK), lambda b: (b, 0, ck)),
            pl.BlockSpec((None, t_new, DBLK), lambda b: (b, 0, cv)),
            cache_spec,
            pl.BlockSpec((D_HEADS, t_new, n_keys), lambda b: (0, 0, 0)),
            pl.BlockSpec((t_new, n_keys), lambda b: (0, 0)),
            pl.BlockSpec((D_HEADS, t_new, LANES), lambda b: (0, 0, 0)),
            pl.BlockSpec((t_new, LANES), lambda b: (0, 0)),
            pl.BlockSpec(memory_space=pl.ANY),
        ],
        out_specs=[
            pl.BlockSpec((None, t_new, DBLK), lambda b: (b, 0, 0)),
            pl.BlockSpec((None, t_new, D_HEADS), lambda b: (b, 0, 0)),
            pl.BlockSpec((None, shift, D_HD), lambda b: (layer * bsz + b, flat_rows // shift - 1, 0)),
        ],
        out_shape=[jax.ShapeDtypeStruct((bsz, t_new, DBLK), F32),
                   jax.ShapeDtypeStruct((bsz, t_new, D_HEADS), F32),
                   jax.ShapeDtypeStruct(out_buf.shape, F32)],
        scratch_shapes=[pltpu.VMEM((D_HEADS, LANES, D_HD), F32),
                        pltpu.VMEM((D_HEADS, LANES, D_HD), F32)],
        input_output_aliases={8: 2},
        compiler_params=_cparams(("parallel",)),
        name="dilated_step",
    )(proj3, proj3, proj3, cache_arg, bias_c, mask_c, bias_n, mask_n, out_buf)


def _win_kernel(k_ref, v_ref, buf_ref, out_ref):
    del buf_ref
    n_pos = k_ref.shape[0]
    for h in range(D_HEADS):
        cs = slice(h * D_HD, (h + 1) * D_HD)
        out_ref[pl.ds(h, n_pos, stride=KV_ROWS), :] = k_ref[:, cs]
        out_ref[pl.ds(D_HEADS + h, n_pos, stride=KV_ROWS), :] = v_ref[:, cs]


def _win_extract(proj3, out_buf, layer, gi, keep):
    bsz, t_len, _ = proj3.shape
    n_pos = min(keep, 512)
    first = (t_len - keep) // n_pos
    base = COL_DQKV // DBLK
    ck, cv = base + N_DG + gi, base + 2 * N_DG + gi
    return pl.pallas_call(
        _win_kernel,
        grid=(bsz, keep // n_pos),
        in_specs=[pl.BlockSpec((None, n_pos, DBLK), lambda b, c: (b, first + c, ck)),
                  pl.BlockSpec((None, n_pos, DBLK), lambda b, c: (b, first + c, cv)),
                  pl.BlockSpec(memory_space=pl.ANY)],
        out_specs=pl.BlockSpec((None, n_pos * KV_ROWS, D_HD), lambda b, c: (layer * bsz + b, c, 0)),
        out_shape=jax.ShapeDtypeStruct(out_buf.shape, F32),
        input_output_aliases={2: 0},
        compiler_params=_cparams(("parallel", "arbitrary")),
        name="win_extract",
    )(proj3, proj3, out_buf)


def _merge_kernel(*refs, split_d):
    if split_d:
        (gate_ref, ya_ref, yb_ref, yc_ref, o0_ref, o1_ref, o2_ref, l0_ref, l1_ref, l2_ref,
         wbr_ref, wo_ref, h_ref, out_ref) = refs
        l0, l1, l2 = l0_ref[...], l1_ref[...], l2_ref[...]
        m = jnp.maximum(jnp.maximum(l0, l1), l2)
        e0, e1, e2 = jnp.exp(l0 - m), jnp.exp(l1 - m), jnp.exp(l2 - m)
        inv = 1.0 / (e0 + e1 + e2)
        w0, w1, w2 = e0 * inv, e1 * inv, e2 * inv
        yd = []
        for h in range(D_HEADS):
            cs = slice(h * D_HD, (h + 1) * D_HD)
            yd.append(w0[:, h:h + 1] * o0_ref[:, cs] + w1[:, h:h + 1] * o1_ref[:, cs]
                      + w2[:, h:h + 1] * o2_ref[:, cs])
        y_d = jnp.concatenate(yd, axis=1)
    else:
        gate_ref, ya_ref, yb_ref, yc_ref, yd_ref, wbr_ref, wo_ref, h_ref, out_ref = refs
        y_d = yd_ref[...]
    branches = [ya_ref[...], yb_ref[...], yc_ref[...], y_d]
    merged = None
    for nbr in range(N_BRANCH):
        up = _mm(branches[nbr].astype(BF16), wbr_ref[nbr])
        term = _sigmoid(gate_ref[:, nbr * D_MODEL:(nbr + 1) * D_MODEL]) * up
        merged = term if merged is None else merged + term
    out_ref[...] = h_ref[...] + _mm(merged.astype(BF16), wo_ref[...])


def _merge(proj2, ya, yb, yc, yd, w_br, w_o, layer, h2, tm):
    n = h2.shape[0]
    row = lambda width: pl.BlockSpec((tm, width), lambda i: (i, 0))
    split_d = isinstance(yd, tuple)
    if split_d:
        d_args = list(yd[0]) + list(yd[1])
        d_specs = [row(MIX_W)] * N_DG + [row(D_HEADS)] * N_DG
    else:
        d_args = [yd]
        d_specs = [row(MIX_W)]
    return pl.pallas_call(
        functools.partial(_merge_kernel, split_d=split_d),
        grid=(n // tm,),
        in_specs=[pl.BlockSpec((tm, N_BRANCH * D_MODEL), lambda i: (i, COL_GATE // (N_BRANCH * D_MODEL))),
                  row(MIX_W), row(MIX_W), row(MIX_W)] + d_specs + [
                  pl.BlockSpec((None, N_BRANCH, MIX_W, D_MODEL), lambda i: (layer, 0, 0, 0),
                               pipeline_mode=pl.Buffered(1)),
                  pl.BlockSpec((None, D_MODEL, D_MODEL), lambda i: (layer, 0, 0), pipeline_mode=pl.Buffered(1)),
                  row(D_MODEL)],
        out_specs=row(D_MODEL),
        out_shape=jax.ShapeDtypeStruct((n, D_MODEL), F32),
        compiler_params=_cparams(("parallel",)),
        name="branch_merge",
    )(proj2, ya, yb, yc, *d_args, w_br, w_o, h2)


def _ffn_kernel(h_ref, gffn_ref, wg_ref, wu_ref, wd_ref, p_ref, gple_ref, wpg_ref, wple_ref, gfin_ref,
                out_ref, *, final, n_split):
    h = h_ref[...]
    f = _rms_rows(h, gffn_ref[...]).astype(BF16)
    tf = D_FF // n_split
    acc = None
    for k in range(n_split):
        cs = slice(k * tf, (k + 1) * tf)
        a = _mm(f, wg_ref[:, cs])
        b = _mm(f, wu_ref[:, cs])
        part = _mm((_silu(a) * b).astype(BF16), wd_ref[cs, :])
        acc = part if acc is None else acc + part
    h2 = h + acc
    e = _rms_rows(h2, gple_ref[...]).astype(BF16)
    gate = _sigmoid(_mm(e, wpg_ref[...]))
    h3 = h2 + gate * _mm(p_ref[...].astype(BF16), wple_ref[...])
    if final:
        out_ref[...] = _rms_rows(h3, gfin_ref[...])
    else:
        out_ref[...] = h3


def _ffn(h2, g_ffn, w_gate, w_up, w_down, p3, g_ple, w_pg, w_ple, g_fin, layer, tm, n_split, final):
    n = h2.shape[0]
    vec = lambda: pl.BlockSpec((1, D_MODEL), lambda i: (0, 0))
    res = lambda shape: pl.BlockSpec((None,) + shape, lambda i: (layer, 0, 0), pipeline_mode=pl.Buffered(1))
    return pl.pallas_call(
        functools.partial(_ffn_kernel, final=final, n_split=n_split),
        grid=(n // tm,),
        in_specs=[pl.BlockSpec((tm, D_MODEL), lambda i: (i, 0)),
                  vec(),
                  res((D_MODEL, D_FF)),
                  res((D_MODEL, D_FF)),
                  res((D_FF, D_MODEL)),
                  pl.BlockSpec((None, tm, PLE_DIM), lambda i: (layer, i, 0)),
                  vec(),
                  res((D_MODEL, D_MODEL)),
                  res((PLE_DIM, D_MODEL)),
                  vec()],
        out_specs=pl.BlockSpec((tm, D_MODEL), lambda i: (i, 0)),
        out_shape=jax.ShapeDtypeStruct((n, D_MODEL), F32),
        compiler_params=_cparams(("parallel",)),
        name="ffn_ple",
    )(h2, g_ffn.reshape(1, -1), w_gate, w_up, w_down, p3, g_ple.reshape(1, -1), w_pg, w_ple,
      g_fin.reshape(1, -1))


def _prep_weights(w_in, w_br, w_o, w_gate, w_up, w_down, w_ple_gate, w_ple, a_log, dt_bias):
    o_ab = A_QKV + MIX_W
    o_bq = o_ab + 2 * A_HEADS
    o_d = o_bq + 2 * B_HEADS * B_DK + 3 * MIX_W
    o_g = o_d + 3 * N_DG * MIX_W
    w_main = jnp.concatenate([w_in[:, :, :o_ab], w_in[:, :, o_bq:o_d], w_in[:, :, o_g:],
                              w_in[:, :, o_d:o_g]], axis=-1).astype(BF16)
    w_ab = jnp.pad(w_in[:, :, o_ab:o_bq], ((0, 0), (0, 0), (0, LANES - 2 * A_HEADS))).astype(BF16)
    par = jnp.zeros((a_log.shape[0], SUBLANES, LANES), F32)
    par = par.at[:, 0, A_HEADS:2 * A_HEADS].set(a_log)
    par = par.at[:, 1, A_HEADS:2 * A_HEADS].set(dt_bias)
    return dict(w_main=w_main, w_ab=w_ab, par=par, w_br=w_br.astype(BF16), w_o=w_o.astype(BF16),
                w_gate=w_gate.astype(BF16), w_up=w_up.astype(BF16), w_down=w_down.astype(BF16),
                w_pg=w_ple_gate.astype(BF16), w_ple=w_ple.astype(BF16))


def _band_bias(t5_bias):
    i = np.arange(D_SPAN)[:, None]
    j = np.arange(2 * D_SPAN)[None, :]
    rel = i + D_SPAN - j
    band = (rel >= 0) & (rel <= D_SPAN)
    masks = np.stack([band & (j >= D_SPAN), band])
    out = []
    for gi, dil in enumerate(D_DILATIONS):
        idx = _t5_bucket_np(rel * dil).reshape(-1)
        tab = _bias_lookup(idx, t5_bias[:, gi * D_HEADS:(gi + 1) * D_HEADS])
        tab = tab.reshape(D_SPAN, 2 * D_SPAN, D_HEADS).transpose(2, 0, 1)
        out.append(jnp.where(masks[None], tab[:, None], -jnp.inf))
    return jnp.stack(out)


def _step_bias(t5_bias, l_buf, t_new, gi, dil):
    t = np.arange(t_new)[:, None]
    j = np.arange(l_buf + LANES)[None, :]
    dist = l_buf + t - j
    valid = (dist >= 0) & (dist % dil == 0) & (dist <= D_SPAN * dil) & (j < l_buf + t_new)
    idx = _t5_bucket_np(np.where(valid, dist, 0)).reshape(-1)
    tab = _bias_lookup(idx, t5_bias[:, gi * D_HEADS:(gi + 1) * D_HEADS])
    tab = tab.reshape(t_new, l_buf + LANES, D_HEADS).transpose(2, 0, 1)
    tab = jnp.where(valid[None], tab, 0.0)
    mask = jnp.asarray(np.where(valid, 0.0, -np.inf), dtype=F32)
    pos = np.arange(l_buf)
    keep = pos[(pos % dil) < t_new] if t_new < dil else pos
    return tab[:, :, keep], mask[:, keep], tab[:, :, l_buf:], mask[:, l_buf:]


def _layer_common(h2, p3, bsz, t_len, wts, lw, i, mix_fn, tm, final, g_final):
    n = bsz * t_len
    proj2, ab2 = _norm_matmul(h2, lw['g_mix'][i], wts['w_main'], wts['w_ab'], i, tm, 2560)
    proj3 = proj2.reshape(bsz, t_len, PROJ_W)
    ab3 = ab2.reshape(bsz, t_len, LANES)
    ya, yb, yc, yd, states = mix_fn(proj3, ab3)
    flat = lambda x: x.reshape(n, x.shape[-1])
    if isinstance(yd, tuple):
        yd = ([flat(x) for x in yd[0]], [flat(x) for x in yd[1]])
    else:
        yd = flat(yd)
    tm2 = min(tm, 512)
    h2 = _merge(proj2, flat(ya), flat(yb), flat(yc), yd, wts['w_br'], wts['w_o'], i, h2, tm2)
    tm3 = min(tm, 512)
    h2 = _ffn(h2, lw['g_ffn'][i], wts['w_gate'], wts['w_up'], wts['w_down'], p3,
              lw['g_ple'][i], wts['w_pg'], wts['w_ple'], g_final, i, tm3, 2, final)
    return h2, proj3, states


def _run_prompt(x, p, wts, lw, t5_bias, g_final, depth, host):
    bsz, t_len, _ = x.shape
    n = bsz * t_len
    h2 = x.reshape(n, D_MODEL)
    p3 = p.reshape(depth, n, PLE_DIM)
    pos = jnp.arange(t_len, dtype=jnp.int32)
    cos_t, sin_t = _rope_tables(pos, 1)
    band_bias = _band_bias(t5_bias)
    zero_s = jnp.zeros((1, bsz, A_HEADS, A_DK, A_DV), F32)
    zero_c = jnp.zeros((bsz, A_CONV - 1, A_QKV), F32)
    zero_r = jnp.zeros((bsz, B_HEADS // 2, LANES, B_DV), F32)
    zero_p = jnp.zeros((bsz, POOL_MAX - 1, MIX_W), F32)
    keeps = [min(D_SPAN * dil, t_len) for dil in D_DILATIONS]
    win_bufs = [lax.empty((depth * bsz, keep * KV_ROWS, D_HD), F32) for keep in keeps]
    per_layer = []
    for i in range(depth):
        def mix_fn(proj3, ab3, i=i):
            ya, s_new, *shifted = _mixer_delta(
                proj3, ab3, zero_c, zero_s, 0, lw['conv_w'][i], wts['par'][i], lw['a_gain'][i].reshape(1, A_DV),
                n_s=1, l_r=256, chunk=A_CHUNK, carry=True,
                host=(host[0], host[1], i * host[2], host[2], host[3]))
            host[1][:] = shifted
            yb, r_new = _mixer_ret(proj3, cos_t, sin_t, zero_r, n_s=1, l_r=512, chunk=B_CHUNK, carry=True)
            yc = _mixer_pool(proj3, zero_p, lw['c_w'][i].astype(BF16), lw['c_scale'][i].reshape(1, MIX_W),
                             n_s=1, l_r=1024, pos0=0)
            yd = _dilated_band(proj3, band_bias)
            return ya, yb, yc, yd, (s_new, r_new)

        h2, proj3, (s_new, r_new) = _layer_common(h2, p3, bsz, t_len, wts, lw, i,
                                                  mix_fn, 1024, i == depth - 1, g_final)
        for gi in range(N_DG):
            win_bufs[gi] = _win_extract(proj3, win_bufs[gi], i, gi, keeps[gi])
        per_layer.append((s_new,
                          proj3[:, t_len - (A_CONV - 1):, COL_AQKV:COL_AQKV + A_QKV],
                          r_new.reshape(bsz, B_HEADS, B_DK, B_DV),
                          proj3[:, t_len - (POOL_MAX - 1):, COL_CIN:COL_CIN + MIX_W]))
    stacked = tuple(jnp.stack([ns[j] for ns in per_layer]) for j in range(4))
    wins = tuple(win_bufs[gi].reshape(depth, bsz, keeps[gi], 2, D_HEADS, D_HD) for gi in range(N_DG))
    return h2.reshape(bsz, t_len, D_MODEL), stacked + wins


def _run_sample(x, p, states, wts, lw, t5_bias, g_final, depth, cache_flat, out_bufs):
    s_delta, s_conv, s_ret, s_pool, caches = states
    bsz, t_len, _ = x.shape
    n = bsz * t_len
    n_s = 16
    h2 = x.reshape(n, D_MODEL)
    p3 = p.reshape(depth, n, PLE_DIM)
    pos = PAST_LEN + jnp.arange(t_len, dtype=jnp.int32)
    cos_t, sin_t = _rope_tables(pos, n_s)
    step_bias = [_step_bias(t5_bias, caches[gi].shape[2], t_len, gi, dil) for gi, dil in enumerate(D_DILATIONS)]
    s_ret2 = s_ret.reshape(s_ret.shape[0], bsz, B_HEADS // 2, LANES, B_DV)
    per_layer = []
    for i in range(depth):
        def mix_fn(proj3, ab3, i=i):
            ya, s_new = _mixer_delta(proj3, ab3, s_conv[i], s_delta, i, lw['conv_w'][i], wts['par'][i],
                                     lw['a_gain'][i].reshape(1, A_DV), n_s=n_s, l_r=t_len, chunk=t_len, carry=False)
            yb, r_new = _mixer_ret(proj3, cos_t, sin_t, s_ret2[i], n_s=n_s, l_r=t_len, chunk=t_len, carry=False)
            yc = _mixer_pool(proj3, s_pool[i], lw['c_w'][i].astype(BF16), lw['c_scale'][i].reshape(1, MIX_W),
                             n_s=n_s, l_r=t_len, pos0=PAST_LEN)
            od, lse = [], []
            for gi, dil in enumerate(D_DILATIONS):
                o, l, out_bufs[gi] = _dilated_step(proj3, cache_flat[gi], out_bufs[gi], i, gi, dil, *step_bias[gi])
                od.append(o)
                lse.append(l)
            return ya, yb, yc, (od, lse), (s_new, r_new)

        h2, proj3, (s_new, r_new) = _layer_common(h2, p3, bsz, t_len, wts, lw, i,
                                                  mix_fn, n, i == depth - 1, g_final)
        pool_new = jnp.concatenate([s_pool[i][:, t_len:], proj3[:, :, COL_CIN:COL_CIN + MIX_W]], axis=1)
        per_layer.append((s_new,
                          proj3[:, t_len - (A_CONV - 1):, COL_AQKV:COL_AQKV + A_QKV],
                          r_new.reshape(bsz, B_HEADS, B_DK, B_DV),
                          pool_new))
    stacked = tuple(jnp.stack([ns[j] for ns in per_layer]) for j in range(4))
    wins = tuple(out_bufs[gi].reshape(caches[gi].shape) for gi in range(N_DG))
    return h2.reshape(bsz, t_len, D_MODEL), stacked + wins


def kernel(x_prompt, x_sample, state_delta, state_delta_conv, state_ret, state_pool, cache_win0, cache_win1, cache_win2, p_prompt, p_sample, g_mix, w_in, conv_w, a_log, dt_bias, a_gain, c_w, c_scale, t5_bias, w_br, w_o, g_ffn, w_gate, w_up, w_down, g_ple, w_ple_gate, w_ple, g_final):
    depth = w_in.shape[0]
    wts = _prep_weights(w_in, w_br, w_o, w_gate, w_up, w_down, w_ple_gate, w_ple, a_log, dt_bias)
    lw = dict(g_mix=g_mix, conv_w=conv_w, a_gain=a_gain, c_w=c_w, c_scale=c_scale, g_ffn=g_ffn, g_ple=g_ple)
    caches = (cache_win0, cache_win1, cache_win2)
    bsz_s, t_new = x_sample.shape[0], x_sample.shape[1]
    cache_flat = [c.reshape(c.shape[0] * bsz_s, c.shape[2] * KV_ROWS, D_HD) for c in caches]
    out_bufs = [lax.empty(c.shape, F32) for c in cache_flat]
    y_p, (delta_p, conv_p, ret_p, pool_p, win0_p, win1_p, win2_p) = _run_prompt(
        x_prompt, p_prompt, wts, lw, t5_bias, g_final, depth, (cache_flat, out_bufs, bsz_s, t_new * KV_ROWS))
    y_s, (delta_s, conv_s, ret_s, pool_s, win0_s, win1_s, win2_s) = _run_sample(
        x_sample, p_sample, (state_delta, state_delta_conv, state_ret, state_pool, caches),
        wts, lw, t5_bias, g_final, depth, cache_flat, out_bufs)
    return (y_p, y_s, delta_p, delta_s, conv_p, conv_s, ret_p, ret_s,
            pool_p, pool_s, win0_p, win0_s, win1_p, win1_s, win2_p, win2_s)
```

```python
import functools
import math

import numpy as np
import jax
import jax.numpy as jnp
from jax import lax
from jax.experimental import pallas as pl
from jax.experimental.pallas import tpu as pltpu

F32 = jnp.float32
BF16 = jnp.bfloat16
HIGHEST = lax.Precision.HIGHEST

D_MODEL = 1024
DEPTH = 4
PAST_LEN = 8192
MIX_W = D_MODEL // 2
N_BRANCH = 4
A_HEADS = 4
A_DK = MIX_W // A_HEADS
A_DV = MIX_W // A_HEADS
A_QKV = A_HEADS * (2 * A_DK + A_DV)
A_CONV = 4
A_CHUNK = 64
B_HEADS = 4
B_DV = MIX_W // B_HEADS
B_DK = B_DV // 2
B_CHUNK = 128
ROPE_BASE = 10000.0
C_GROUPS = 4
C_GW = MIX_W // C_GROUPS
POOL_WINDOWS = (2, 4, 8, 16)
POOL_MAX = 16
D_DILATIONS = (1, 4, 16)
N_DG = 3
D_SPAN = 128
D_HEADS = 4
D_HD = MIX_W // D_HEADS
REL_BUCKETS = 32
REL_MAX_DIST = D_SPAN * 16
D_FF = -(-8 * D_MODEL // (3 * 256)) * 256
PLE_DIM = 256
EPS = 1e-6

LANES = 128
SUBLANES = 8
VMEM_LIMIT_BYTES = 56 * 1024 * 1024

PROJ_W = 8704
COL_AQKV = 0
COL_AZ = 1536
COL_BQ = 2048
COL_BK = 2304
COL_BV = 2560
COL_BG = 3072
COL_CIN = 3584
COL_DQKV = 4096
GATE_W = N_BRANCH * D_MODEL
DBLK = D_HEADS * D_HD


def _cparams(sem):
    return pltpu.CompilerParams(dimension_semantics=sem, vmem_limit_bytes=VMEM_LIMIT_BYTES)


def _nt(a, b, precision=None):
    return lax.dot_general(a, b, (((1,), (1,)), ((), ())), precision=precision,
                           preferred_element_type=F32)


def _tn(a, b):
    return lax.dot_general(a, b, (((0,), (0,)), ((), ())), preferred_element_type=F32)


def _mm(a, b, precision=None):
    return jnp.dot(a, b, precision=precision, preferred_element_type=F32)


def _sigmoid(x):
    return 1.0 / (1.0 + jnp.exp(-x))


def _silu(x):
    return x * _sigmoid(x)


def _rms_rows(x, g):
    return x * lax.rsqrt(jnp.mean(x * x, axis=-1, keepdims=True) + EPS) * g


def _norm_matmul_kernel(x_ref, g_ref, w_ref, wab_ref, o_ref, ab_ref, u_ref):
    @pl.when(pl.program_id(1) == 0)
    def _():
        u_ref[...] = _rms_rows(x_ref[...], g_ref[...]).astype(BF16)
        ab_ref[...] = _mm(u_ref[...], wab_ref[...])

    o_ref[...] = _mm(u_ref[...], w_ref[...])


def _norm_matmul(x, g, w, w_ab, layer, tm, tn):
    n, d = x.shape
    wcols = w.shape[2]
    return pl.pallas_call(
        _norm_matmul_kernel,
        grid=(n // tm, wcols // tn),
        in_specs=[pl.BlockSpec((tm, d), lambda i, j: (i, 0)),
                  pl.BlockSpec((1, d), lambda i, j: (0, 0)),
                  pl.BlockSpec((None, d, tn), lambda i, j: (layer, 0, j)),
                  pl.BlockSpec((None, d, LANES), lambda i, j: (layer, 0, 0))],
        out_specs=[pl.BlockSpec((tm, tn), lambda i, j: (i, j)),
                   pl.BlockSpec((tm, LANES), lambda i, j: (i, 0))],
        out_shape=[jax.ShapeDtypeStruct((n, wcols), F32),
                   jax.ShapeDtypeStruct((n, LANES), F32)],
        scratch_shapes=[pltpu.VMEM((tm, d), BF16)],
        compiler_params=_cparams(("parallel", "arbitrary")),
        name="norm_matmul",
    )(x, g.reshape(1, d), w, w_ab)


def _delta_kernel(qkv_ref, z_ref, ab_ref, cprev_ref, sprev_ref, convw_ref, par_ref, gain_ref,
                  *rest, chunk, carry, n_host, host_chunks):
    host_in = rest[:3 * n_host]
    o_ref, snew_ref = rest[3 * n_host:3 * n_host + 2]
    host_out = rest[3 * n_host + 2:4 * n_host + 2]
    ext_ref, s_ref = rest[4 * n_host + 2:]
    j = pl.program_id(1)
    for g in range(n_host):
        src, nxt, dst = host_in[3 * g], host_in[3 * g + 1], host_out[g]
        rows, tail = src.shape[0], nxt.shape[0]
        dst[0:rows - tail, :] = src[tail:rows, :]
        last = (j % host_chunks) == host_chunks - 1
        dst[rows - tail:rows, :] = jnp.where(last, 0.0, nxt[...])
    n_s, l_r, _ = qkv_ref.shape
    t_b = n_s * l_r
    n_c = t_b // chunk
    hdr = SUBLANES
    pre = A_CONV - 1

    @pl.when(j == 0)
    def _():
        ext_ref[:, 0:hdr, :] = jnp.zeros((n_s, hdr, A_QKV), F32)
        ext_ref[:, hdr - pre:hdr, :] = cprev_ref[...]
        if carry:
            s_ref[...] = sprev_ref[0]

    if carry:
        @pl.when(j > 0)
        def _():
            ext_ref[:, 0:hdr, :] = ext_ref[:, l_r:l_r + hdr, :]

    ext_ref[:, hdr:hdr + l_r, :] = qkv_ref[...]
    cw = convw_ref[...]

    def conv_cols(c0):
        acc = None
        for t in range(A_CONV):
            term = (ext_ref[:, hdr - pre + t:hdr - pre + t + l_r, c0:c0 + LANES]
                    * cw[t:t + 1, c0:c0 + LANES])
            acc = term if acc is None else acc + term
        return _silu(acc).reshape(t_b, LANES)

    ab = ab_ref[...].reshape(t_b, LANES)
    par = par_ref[...]
    beta_all = _sigmoid(ab)
    xs = ab + par[1:2, :]
    softplus = jnp.maximum(xs, 0.0) + jnp.log1p(jnp.exp(-jnp.abs(xs)))
    g_all = -jnp.exp(par[0:1, :]) * softplus

    sh = int(math.log2(chunk))
    ri = lax.broadcasted_iota(jnp.int32, (t_b, t_b), 0)
    ci = lax.broadcasted_iota(jnp.int32, (t_b, t_b), 1)
    same = lax.shift_right_logical(ri, sh) == lax.shift_right_logical(ci, sh)
    causal = jnp.logical_and(same, ci <= ri)
    strict = jnp.logical_and(same, ci < ri)
    gc_all = _mm(causal.astype(F32), g_all, precision=HIGHEST)
    sel = (lax.broadcasted_iota(jnp.int32, (SUBLANES, LANES), 0)
           == lax.broadcasted_iota(jnp.int32, (SUBLANES, LANES), 1)).astype(F32)
    gc_rows = _nt(sel, gc_all, precision=HIGHEST)
    glast_all = _mm(same.astype(F32), g_all, precision=HIGHEST)

    gain = gain_ref[...]
    z = z_ref[...].reshape(t_b, MIX_W)
    row_id = lax.broadcasted_iota(jnp.int32, (t_b, 1), 0)
    heads = range(A_HEADS)

    gcc = [gc_all[:, A_HEADS + h:A_HEADS + h + 1] for h in heads]
    gcr = [gc_rows[A_HEADS + h:A_HEADS + h + 1, :] for h in heads]
    glast = [glast_all[:, A_HEADS + h:A_HEADS + h + 1] for h in heads]
    bcol = [beta_all[:, h:h + 1] for h in heads]
    q = [conv_cols(h * A_DK) for h in heads]
    k = [conv_cols(A_HEADS * A_DK + h * A_DK) for h in heads]
    v = [conv_cols(2 * A_HEADS * A_DK + h * A_DV) for h in heads]
    q = [x * lax.rsqrt(jnp.sum(x * x, axis=-1, keepdims=True) + EPS) * (A_DK ** -0.5) for x in q]
    k = [x * lax.rsqrt(jnp.sum(x * x, axis=-1, keepdims=True) + EPS) for x in k]
    kb = [x.astype(BF16) for x in k]
    kk = [_nt(kb[h], kb[h]) for h in heads]
    qkr = [_nt(q[h].astype(BF16), kb[h]) for h in heads]
    gam = [jnp.where(causal, jnp.exp(jnp.where(causal, gcc[h] - gcr[h], 0.0)), 0.0) for h in heads]
    qk = [(qkr[h] * gam[h]).astype(BF16) for h in heads]
    a = [jnp.where(strict, bcol[h] * kk[h] * gam[h], 0.0) for h in heads]
    r = [-x for x in a]
    p = a
    for _ in range(sh - 1):
        pb = [x.astype(BF16) for x in p]
        p = [_mm(x, x) for x in pb]
        pr = [_mm(p[h].astype(BF16), r[h].astype(BF16)) for h in heads]
        r = [r[h] + p[h] + pr[h] for h in heads]
    ecol = [jnp.exp(x) for x in gcc]
    rhs = [jnp.concatenate([v[h] * bcol[h], k[h] * (bcol[h] * ecol[h])], axis=1) for h in heads]
    sol = [rhs[h] + _mm(r[h].astype(BF16), rhs[h].astype(BF16)) for h in heads]
    u = [x[:, :A_DV] for x in sol]
    w = [x[:, A_DV:] for x in sol]
    qe = [q[h] * ecol[h] for h in heads]
    kd = [(k[h] * jnp.exp(glast[h] - gcc[h])).astype(BF16) for h in heads]
    sdec = [jnp.exp(x) for x in glast]
    vnews = [[] for _ in heads]
    qss = [[] for _ in heads]
    if carry:
        s = [s_ref[h] for h in heads]
        for c in range(n_c):
            r0 = c * chunk
            wq = [jnp.concatenate([w[h][r0:r0 + chunk], qe[h][r0:r0 + chunk]], axis=0).astype(BF16)
                  for h in heads]
            t = [_mm(wq[h], s[h].astype(BF16)) for h in heads]
            vnew = [u[h][r0:r0 + chunk] - t[h][:chunk] for h in heads]
            upd = [_tn(kd[h][r0:r0 + chunk], vnew[h].astype(BF16)) for h in heads]
            s = [s[h] * sdec[h][r0:r0 + 1, :] + upd[h] for h in heads]
            for h in heads:
                vnews[h].append(vnew[h])
                qss[h].append(t[h][chunk:])
        vn_all = [jnp.concatenate(x, axis=0) if n_c > 1 else x[0] for x in vnews]
        vnb = [x.astype(BF16) for x in vn_all]
    else:
        olds = [[] for _ in heads]
        for c in range(n_c):
            r0 = c * chunk
            for h in heads:
                s_c = sprev_ref[c, h]
                olds[h].append(s_c)
                wq = jnp.concatenate([w[h][r0:r0 + chunk], qe[h][r0:r0 + chunk]], axis=0).astype(BF16)
                t = _mm(wq, s_c.astype(BF16))
                vnews[h].append(u[h][r0:r0 + chunk] - t[:chunk])
                qss[h].append(t[chunk:])
        vn_all = [jnp.concatenate(x, axis=0) for x in vnews]
        vnb = [x.astype(BF16) for x in vn_all]
        for c in range(n_c):
            r0 = c * chunk
            in_c = jnp.logical_and(row_id >= r0, row_id < r0 + chunk)
            for h in heads:
                upd = _tn(jnp.where(in_c, kd[h], jnp.zeros_like(kd[h])),
                          jnp.where(in_c, vnb[h], jnp.zeros_like(vnb[h])))
                snew_ref[c, h] = olds[h][c] * sdec[h][r0:r0 + 1, :] + upd
    outs = []
    for h in heads:
        qs_all = jnp.concatenate(qss[h], axis=0) if n_c > 1 else qss[h][0]
        o = qs_all + _mm(qk[h], vnb[h])
        o = o * lax.rsqrt(jnp.mean(o * o, axis=-1, keepdims=True) + EPS) * gain
        outs.append(o * _silu(z[:, h * A_DV:(h + 1) * A_DV]))
    o_ref[...] = jnp.concatenate(outs, axis=1).reshape(n_s, l_r, MIX_W)

    if carry:
        for h in heads:
            s_ref[h] = s[h]

        @pl.when(j == pl.num_programs(1) - 1)
        def _():
            for h in heads:
                snew_ref[0, h] = s[h]


def _mixer_delta(proj3, ab3, conv_prev, s_prev, layer, conv_w, par, gain, *, n_s, l_r, chunk, carry,
                 host=None):
    bsz, t_len, _ = proj3.shape
    grid = (bsz // n_s, t_len // l_r)
    host_specs, host_args, host_out_specs, host_out_shapes, aliases = [], [], [], [], {}
    n_host, host_chunks = 0, 1
    if host is not None:
        bufs, dsts, slab0, n_slabs, tail = host
        n_host = len(bufs)
        steps = grid[0] * grid[1]
        host_chunks = steps // n_slabs
        per_b = grid[1] // host_chunks
        assert steps == host_chunks * n_slabs and grid[1] == per_b * host_chunks
        for g, (buf, dst) in enumerate(zip(bufs, dsts)):
            rows = buf.shape[1] // host_chunks
            per, last = rows // tail, buf.shape[1] // tail - 1
            assert rows % tail == 0 and rows * host_chunks == buf.shape[1]
            slab = lambda b, j: slab0 + b * per_b + j // host_chunks
            host_specs += [
                pl.BlockSpec((None, rows, LANES), lambda b, j, slab=slab: (slab(b, j), j % host_chunks, 0)),
                pl.BlockSpec((None, tail, LANES), lambda b, j, slab=slab, per=per, last=last: (
                    slab(b, j), jnp.minimum((j % host_chunks + 1) * per, last), 0)),
                pl.BlockSpec(memory_space=pl.ANY)]
            host_args += [buf, buf, dst]
            host_out_specs.append(
                pl.BlockSpec((None, rows, LANES), lambda b, j, slab=slab: (slab(b, j), j % host_chunks, 0)))
            host_out_shapes.append(jax.ShapeDtypeStruct(dst.shape, F32))
            aliases[8 + 3 * g + 2] = 2 + g
    kern = functools.partial(_delta_kernel, chunk=chunk, carry=carry, n_host=n_host, host_chunks=host_chunks)
    return pl.pallas_call(
        kern,
        grid=grid,
        in_specs=[
            pl.BlockSpec((n_s, l_r, A_QKV), lambda b, j: (b, j, COL_AQKV // A_QKV)),
            pl.BlockSpec((n_s, l_r, MIX_W), lambda b, j: (b, j, COL_AZ // MIX_W)),
            pl.BlockSpec((n_s, l_r, LANES), lambda b, j: (b, j, 0)),
            pl.BlockSpec((n_s, A_CONV - 1, A_QKV), lambda b, j: (b, 0, 0)),
            pl.BlockSpec((None, n_s, A_HEADS, A_DK, A_DV), lambda b, j: (layer, b, 0, 0, 0)),
            pl.BlockSpec((A_CONV, A_QKV), lambda b, j: (0, 0)),
            pl.BlockSpec((SUBLANES, LANES), lambda b, j: (0, 0)),
            pl.BlockSpec((1, A_DV), lambda b, j: (0, 0)),
        ] + host_specs,
        out_specs=[
            pl.BlockSpec((n_s, l_r, MIX_W), lambda b, j: (b, j, 0)),
            pl.BlockSpec((n_s, A_HEADS, A_DK, A_DV), lambda b, j: (b, 0, 0, 0)),
        ] + host_out_specs,
        out_shape=[jax.ShapeDtypeStruct((bsz, t_len, MIX_W), F32),
                   jax.ShapeDtypeStruct((bsz, A_HEADS, A_DK, A_DV), F32)] + host_out_shapes,
        scratch_shapes=[pltpu.VMEM((n_s, SUBLANES + l_r + SUBLANES, A_QKV), F32),
                        pltpu.VMEM((A_HEADS, A_DK, A_DV), F32)],
        input_output_aliases=aliases,
        compiler_params=_cparams(("parallel", "arbitrary")),
        name="mixer_delta",
    )(proj3, proj3, ab3, conv_prev, s_prev, conv_w, par, gain, *host_args)


def _ret_log_gamma(h):
    return math.log1p(-(2.0 ** (-5.0 - h)))


def _ret_kernel(q_ref, k_ref, v_ref, g_ref, cos_ref, sin_ref, rprev_ref,
                o_ref, rnew_ref, r_ref, *, chunk, carry):
    j = pl.program_id(1)
    n_s, l_r, _ = q_ref.shape
    t_b = n_s * l_r
    n_c = t_b // chunk
    half = B_DK // 2

    if carry:
        @pl.when(j == 0)
        def _():
            r_ref[...] = rprev_ref[0]

    cos = cos_ref[...]
    sin = sin_ref[...]
    lane = lax.broadcasted_iota(jnp.int32, (1, LANES), 1)
    first_half = (lane % B_DK) < half
    lane_head = lane // B_DK

    def rope(x):
        swapped = jnp.where(first_half, pltpu.roll(x, LANES - half, axis=1),
                            pltpu.roll(x, half, axis=1))
        return x * cos + swapped * sin

    idx_i = lax.broadcasted_iota(jnp.int32, (chunk, chunk), 0)
    idx_j = lax.broadcasted_iota(jnp.int32, (chunk, chunk), 1)
    pos_col = lax.broadcasted_iota(jnp.int32, (chunk, 1), 0).astype(F32)
    row_head = lax.broadcasted_iota(jnp.int32, (LANES, 1), 0) // B_DK
    gate = g_ref[...].reshape(t_b, MIX_W)
    vv = v_ref[...].reshape(t_b, MIX_W)

    small = chunk < 2 * SUBLANES
    if small:
        sh = int(math.log2(chunk))
        bi = lax.broadcasted_iota(jnp.int32, (t_b, t_b), 0)
        bj = lax.broadcasted_iota(jnp.int32, (t_b, t_b), 1)
        bsame = lax.shift_right_logical(bi, sh) == lax.shift_right_logical(bj, sh)
        bcausal = jnp.logical_and(bsame, bj <= bi)
        bdiff = (bi - bj).astype(F32)
        row_id = lax.broadcasted_iota(jnp.int32, (t_b, 1), 0)
        pos_in = (row_id & (chunk - 1)).astype(F32)

    outs = [None] * B_HEADS
    for p in range(B_HEADS // 2):
        q2 = rope(q_ref[...].reshape(t_b, B_HEADS * B_DK)[:, p * LANES:(p + 1) * LANES])
        k2 = rope(k_ref[...].reshape(t_b, B_HEADS * B_DK)[:, p * LANES:(p + 1) * LANES]) * (B_DK ** -0.5)
        lgs = [_ret_log_gamma(2 * p + hh) for hh in range(2)]
        qh = [jnp.where(lane_head == hh, q2, 0.0) for hh in range(2)]
        kh = [jnp.where(lane_head == hh, k2, 0.0) for hh in range(2)]
        vh = [vv[:, (2 * p + hh) * B_DV:(2 * p + hh + 1) * B_DV] for hh in range(2)]
        row_scale = jnp.where(row_head == 0, math.exp(lgs[0] * chunk), math.exp(lgs[1] * chunk))
        if small:
            intra = []
            for hh in range(2):
                dec = jnp.where(bcausal, jnp.exp(lgs[hh] * jnp.where(bcausal, bdiff, 0.0)), 0.0)
                sc = _nt(qh[hh].astype(BF16), kh[hh].astype(BF16)) * dec
                intra.append(_mm(sc.astype(BF16), vh[hh].astype(BF16)))
            cross = [[], []]
            for c in range(n_c):
                r0 = c * chunk
                rp = r_ref[p] if carry else rprev_ref[c, p]
                qq = jnp.concatenate([qh[0][r0:r0 + chunk], qh[1][r0:r0 + chunk]], axis=0)
                t = _mm(qq.astype(BF16), rp.astype(BF16))
                cross[0].append(t[:chunk])
                cross[1].append(t[chunk:])
                in_c = jnp.logical_and(row_id >= r0, row_id < r0 + chunk)
                upd = None
                for hh in range(2):
                    kd = jnp.where(in_c, kh[hh] * jnp.exp(lgs[hh] * (chunk - 1.0 - pos_in)), 0.0)
                    term = _tn(kd.astype(BF16), vh[hh].astype(BF16))
                    upd = term if upd is None else upd + term
                rn = rp * row_scale + upd
                if carry:
                    r_ref[p] = rn
                else:
                    rnew_ref[c, p] = rn
            for hh in range(2):
                cr = jnp.concatenate(cross[hh], axis=0) * jnp.exp(lgs[hh] * (pos_in + 1.0))
                outs[2 * p + hh] = intra[hh] + cr
        else:
            pieces = [[], []]
            for c in range(n_c):
                r0 = c * chunk
                rp = r_ref[p] if carry else rprev_ref[c, p]
                rpb = rp.astype(BF16)
                upd = None
                for hh in range(2):
                    qc = qh[hh][r0:r0 + chunk].astype(BF16)
                    kc = kh[hh][r0:r0 + chunk]
                    vc = vh[hh][r0:r0 + chunk].astype(BF16)
                    causal = idx_j <= idx_i
                    dec = jnp.where(causal, jnp.exp(lgs[hh] * jnp.where(causal, (idx_i - idx_j).astype(F32), 0.0)), 0.0)
                    sc = _nt(qc, kc.astype(BF16)) * dec
                    oc = _mm(sc.astype(BF16), vc) + _mm(qc, rpb) * jnp.exp(lgs[hh] * (pos_col + 1.0))
                    pieces[hh].append(oc)
                    kd = kc * jnp.exp(lgs[hh] * (chunk - 1.0 - pos_col))
                    term = _tn(kd.astype(BF16), vc)
                    upd = term if upd is None else upd + term
                rn = rp * row_scale + upd
                if carry:
                    r_ref[p] = rn
                else:
                    rnew_ref[c, p] = rn
            for hh in range(2):
                outs[2 * p + hh] = jnp.concatenate(pieces[hh], axis=0) if n_c > 1 else pieces[hh][0]

    res = []
    for h in range(B_HEADS):
        o = outs[h]
        o = o * lax.rsqrt(jnp.mean(o * o, axis=-1, keepdims=True) + EPS)
        res.append(o * _silu(gate[:, h * B_DV:(h + 1) * B_DV]))
    o_ref[...] = jnp.concatenate(res, axis=1).reshape(n_s, l_r, MIX_W)

    if carry:
        @pl.when(j == pl.num_programs(1) - 1)
        def _():
            rnew_ref[0] = r_ref[...]


def _mixer_ret(proj3, cos_t, sin_t, r_prev, *, n_s, l_r, chunk, carry):
    bsz, t_len, _ = proj3.shape
    t_b = n_s * l_r
    grid = (bsz // n_s, t_len // l_r)
    n_tab = cos_t.shape[0] // t_b
    kern = functools.partial(_ret_kernel, chunk=chunk, carry=carry)
    qw = B_HEADS * B_DK
    return pl.pallas_call(
        kern,
        grid=grid,
        in_specs=[
            pl.BlockSpec((n_s, l_r, qw), lambda b, j: (b, j, COL_BQ // qw)),
            pl.BlockSpec((n_s, l_r, qw), lambda b, j: (b, j, COL_BK // qw)),
            pl.BlockSpec((n_s, l_r, MIX_W), lambda b, j: (b, j, COL_BV // MIX_W)),
            pl.BlockSpec((n_s, l_r, MIX_W), lambda b, j: (b, j, COL_BG // MIX_W)),
            pl.BlockSpec((t_b, LANES), lambda b, j: (j % n_tab, 0)),
            pl.BlockSpec((t_b, LANES), lambda b, j: (j % n_tab, 0)),
            pl.BlockSpec((n_s, B_HEADS // 2, LANES, B_DV), lambda b, j: (b, 0, 0, 0)),
        ],
        out_specs=[
            pl.BlockSpec((n_s, l_r, MIX_W), lambda b, j: (b, j, 0)),
            pl.BlockSpec((n_s, B_HEADS // 2, LANES, B_DV), lambda b, j: (b, 0, 0, 0)),
        ],
        out_shape=[jax.ShapeDtypeStruct((bsz, t_len, MIX_W), F32),
                   jax.ShapeDtypeStruct((bsz, B_HEADS // 2, LANES, B_DV), F32)],
        scratch_shapes=[pltpu.VMEM((B_HEADS // 2, LANES, B_DV), F32)],
        compiler_params=_cparams(("parallel", "arbitrary")),
        name="mixer_ret",
    )(proj3, proj3, proj3, proj3, cos_t, sin_t, r_prev)


def _rope_tables(pos, reps):
    half = B_DK // 2
    inv = ROPE_BASE ** (-jnp.arange(half, dtype=F32) / half)
    ang = pos.astype(F32)[:, None] * inv[None, :]
    cos = jnp.cos(ang)
    sin = jnp.sin(ang)
    cos_t = jnp.concatenate([cos, cos, cos, cos], axis=1)
    sin_t = jnp.concatenate([-sin, sin, -sin, sin], axis=1)
    return jnp.tile(cos_t, (reps, 1)), jnp.tile(sin_t, (reps, 1))


def _pool_kernel(x_ref, prev_ref, cw_ref, cs_ref, o_ref, ext_ref, *, pos0):
    j = pl.program_id(1)
    n_s, l_r, _ = x_ref.shape
    t_b = n_s * l_r
    hdr = POOL_MAX

    @pl.when(j == 0)
    def _():
        ext_ref[:, 0:hdr, :] = jnp.zeros((n_s, hdr, MIX_W), F32)
        ext_ref[:, 1:hdr, :] = prev_ref[...]

    @pl.when(j > 0)
    def _():
        ext_ref[:, 0:hdr, :] = ext_ref[:, l_r:l_r + hdr, :]

    ext_ref[:, hdr:hdr + l_r, :] = x_ref[...]
    pos = pos0 + j * l_r + lax.broadcasted_iota(jnp.int32, (1, l_r, 1), 1)
    outs = []
    for gi, w in enumerate(POOL_WINDOWS):
        c0 = gi * C_GW
        acc = None
        for t in range(w):
            term = ext_ref[:, hdr - t:hdr - t + l_r, c0:c0 + C_GW]
            acc = term if acc is None else acc + term
        cnt = jnp.minimum(pos + 1, w).astype(F32)
        y = acc / cnt - ext_ref[:, hdr:hdr + l_r, c0:c0 + C_GW]
        y = _mm(y.reshape(t_b, C_GW).astype(BF16), cw_ref[gi])
        outs.append(y)
    o_ref[...] = (jnp.concatenate(outs, axis=1) * cs_ref[...]).reshape(n_s, l_r, MIX_W)


def _mixer_pool(proj3, prev, c_w, c_scale, *, n_s, l_r, pos0):
    bsz, t_len, _ = proj3.shape
    grid = (bsz // n_s, t_len // l_r)
    kern = functools.partial(_pool_kernel, pos0=pos0)
    return pl.pallas_call(
        kern,
        grid=grid,
        in_specs=[
            pl.BlockSpec((n_s, l_r, MIX_W), lambda b, j: (b, j, COL_CIN // MIX_W)),
            pl.BlockSpec((n_s, POOL_MAX - 1, MIX_W), lambda b, j: (b, 0, 0)),
            pl.BlockSpec((C_GROUPS, C_GW, C_GW), lambda b, j: (0, 0, 0)),
            pl.BlockSpec((1, MIX_W), lambda b, j: (0, 0)),
        ],
        out_specs=pl.BlockSpec((n_s, l_r, MIX_W), lambda b, j: (b, j, 0)),
        out_shape=jax.ShapeDtypeStruct((bsz, t_len, MIX_W), F32),
        scratch_shapes=[pltpu.VMEM((n_s, POOL_MAX + l_r + SUBLANES, MIX_W), F32)],
        compiler_params=_cparams(("parallel", "arbitrary")),
        name="mixer_pool",
    )(proj3, prev, c_w, c_scale)


def _lookup_kernel(oh_ref, tab_ref, o_ref):
    o_ref[...] = _mm(oh_ref[...], tab_ref[...], precision=HIGHEST)


def _bias_lookup(idx, table):
    n_valid = idx.shape[0]
    tr = 4096
    n_rows = -(-n_valid // tr) * tr
    idx = np.concatenate([idx, np.zeros((n_rows - n_valid,), idx.dtype)])
    onehot = jnp.asarray(idx[:, None] == np.arange(REL_BUCKETS)[None, :], dtype=F32)
    tab = jnp.pad(table, ((0, 0), (0, LANES - table.shape[1])))
    out = pl.pallas_call(
        _lookup_kernel,
        grid=(n_rows // tr,),
        in_specs=[pl.BlockSpec((tr, REL_BUCKETS), lambda i: (i, 0)),
                  pl.BlockSpec((REL_BUCKETS, LANES), lambda i: (0, 0))],
        out_specs=pl.BlockSpec((tr, LANES), lambda i: (i, 0)),
        out_shape=jax.ShapeDtypeStruct((n_rows, LANES), F32),
        compiler_params=_cparams(("parallel",)),
        name="bias_lookup",
    )(onehot, tab)
    return out[:n_valid, :table.shape[1]]


def _t5_bucket_np(dist):
    exact = REL_BUCKETS // 2
    n = np.maximum(dist, 0)
    nf = np.maximum(n, 1).astype(np.float32)
    large = exact + (np.log(nf / np.float32(exact)) / np.float32(math.log(REL_MAX_DIST / exact))
                     * np.float32(REL_BUCKETS - exact)).astype(np.int32)
    large = np.minimum(large, REL_BUCKETS - 1)
    return np.where(n < exact, n, large).astype(np.int32)


SUPER = D_SPAN * D_DILATIONS[-1]
TILES = SUPER // D_SPAN


def _band_fused_kernel(*refs):
    ins = refs[:5 * N_DG]
    bias_ref = refs[5 * N_DG]
    o_ref = refs[5 * N_DG + 1]
    scr = refs[5 * N_DG + 2:]
    kf = scr[0:2 * N_DG:2]
    vf = scr[1:2 * N_DG:2]
    og = scr[2 * N_DG:3 * N_DG]
    lg = scr[3 * N_DG:4 * N_DG]
    has_prev = pl.program_id(2) > 0
    scale = D_HD ** -0.5
    for g, dil in enumerate(D_DILATIONS):
        q_ref, k_ref, v_ref, kp_ref, vp_ref = ins[5 * g:5 * g + 5]
        span = D_SPAN * dil
        kf[g][0:span, :] = kp_ref[...]
        kf[g][span:span + SUPER, :] = k_ref[...]
        vf[g][0:span, :] = vp_ref[...]
        vf[g][span:span + SUPER, :] = v_ref[...]
        sh = int(math.log2(dil))

        def tile(idx, carry, g=g, dil=dil, span=span, sh=sh, q_ref=q_ref):
            qi = lax.shift_right_logical(idx, sh)
            res = idx & (dil - 1)
            start = qi * span + res
            if dil == 1:
                start = pl.multiple_of(start, D_SPAN)
                rows = pl.ds(start, D_SPAN)
                krows = pl.ds(start, 2 * D_SPAN)
            else:
                rows = pl.ds(start, D_SPAN, stride=dil)
                krows = pl.ds(start, 2 * D_SPAN, stride=dil)
            q = q_ref[rows, :].astype(BF16)
            kk = kf[g][krows, :].astype(BF16)
            vv = vf[g][krows, :].astype(BF16)
            which = jnp.where(jnp.logical_or(has_prev, qi > 0), 1, 0)
            s = _nt(q, kk) * scale + bias_ref[g, which]
            m = jnp.max(s, axis=-1, keepdims=True)
            p = jnp.exp(s - m)
            l = jnp.sum(p, axis=-1, keepdims=True)
            og[g][rows, :] = _mm(p.astype(BF16), vv) / l
            lg[g][rows, :] = jnp.broadcast_to(m + jnp.log(l), (D_SPAN, D_HD))
            return carry

        lax.fori_loop(0, TILES, tile, 0, unroll=8)

    def merge(c, carry):
        rows = pl.ds(pl.multiple_of(c * D_SPAN, D_SPAN), D_SPAN)
        l0, l1, l2 = lg[0][rows, :], lg[1][rows, :], lg[2][rows, :]
        m = jnp.maximum(jnp.maximum(l0, l1), l2)
        e0, e1, e2 = jnp.exp(l0 - m), jnp.exp(l1 - m), jnp.exp(l2 - m)
        inv = 1.0 / (e0 + e1 + e2)
        o_ref[rows, :] = (e0 * og[0][rows, :] + e1 * og[1][rows, :] + e2 * og[2][rows, :]) * inv
        return carry

    lax.fori_loop(0, TILES, merge, 0)


def _dilated_band(proj3, band_bias):
    bsz, s_len, _ = proj3.shape
    n_blk = s_len // SUPER
    base = COL_DQKV // D_HD
    in_specs = []
    args = []
    for g, dil in enumerate(D_DILATIONS):
        span = D_SPAN * dil
        per = SUPER // span

        def col(c, g=g):
            return lambda b, h, n: (b, n, base + (c * N_DG + g) * D_HEADS + h)

        def col_prev(c, g=g, per=per):
            return lambda b, h, n: (b, jnp.maximum(n * per - 1, 0), base + (c * N_DG + g) * D_HEADS + h)

        in_specs += [pl.BlockSpec((None, SUPER, D_HD), col(0)),
                     pl.BlockSpec((None, SUPER, D_HD), col(1)),
                     pl.BlockSpec((None, SUPER, D_HD), col(2)),
                     pl.BlockSpec((None, span, D_HD), col_prev(1)),
                     pl.BlockSpec((None, span, D_HD), col_prev(2))]
        args += [proj3] * 5
    in_specs.append(pl.BlockSpec((N_DG, None, 2, D_SPAN, 2 * D_SPAN), lambda b, h, n: (0, h, 0, 0, 0)))
    scratch = []
    for dil in D_DILATIONS:
        scratch += [pltpu.VMEM((D_SPAN * dil + SUPER, D_HD), F32)] * 2
    scratch += [pltpu.VMEM((SUPER, D_HD), F32)] * (2 * N_DG)
    return pl.pallas_call(
        _band_fused_kernel,
        grid=(bsz, D_HEADS, n_blk),
        in_specs=in_specs,
        out_specs=pl.BlockSpec((None, SUPER, D_HD), lambda b, h, n: (b, n, h)),
        out_shape=jax.ShapeDtypeStruct((bsz, s_len, MIX_W), F32),
        scratch_shapes=scratch,
        compiler_params=_cparams(("parallel", "parallel", "arbitrary")),
        name="dilated_band",
    )(*args, band_bias)


KV_ROWS = 2 * D_HEADS


def _step_attn_kernel(q_ref, kn_ref, vn_ref, cache_ref, bias_c_ref, mask_c_ref, bias_n_ref, mask_n_ref,
                      buf_ref, o_ref, lse_ref, new_ref, padk_ref, padv_ref, *, grouped):
    del buf_ref
    t_new = q_ref.shape[0]
    scale = D_HD ** -0.5
    zeros_q = jnp.zeros((SUBLANES, D_HD), F32)
    heads = range(D_HEADS)
    cols = [slice(h * D_HD, (h + 1) * D_HD) for h in heads]

    def pad16(p):
        return jnp.concatenate([p, jnp.zeros_like(p)], axis=0).astype(BF16)

    padk_ref[...] = jnp.zeros(padk_ref.shape, F32)
    padv_ref[...] = jnp.zeros(padv_ref.shape, F32)
    for h in heads:
        new_ref[pl.ds(h, t_new, stride=KV_ROWS), :] = kn_ref[:, cols[h]]
        new_ref[pl.ds(D_HEADS + h, t_new, stride=KV_ROWS), :] = vn_ref[:, cols[h]]
        padk_ref[h, 0:t_new, :] = kn_ref[:, cols[h]]
        padv_ref[h, 0:t_new, :] = vn_ref[:, cols[h]]

    if grouped:
        n_grp, kept = cache_ref.shape[0], cache_ref.shape[1] // KV_ROWS
        kh = [cache_ref[:, pl.ds(h, kept, stride=KV_ROWS), :].reshape(n_grp * kept, D_HD).astype(BF16)
              for h in heads]
        vh = [cache_ref[:, pl.ds(D_HEADS + h, kept, stride=KV_ROWS), :].reshape(n_grp * kept, D_HD).astype(BF16)
              for h in heads]
    else:
        n_pos = cache_ref.shape[0] // KV_ROWS
        kh = [cache_ref[pl.ds(h, n_pos, stride=KV_ROWS), :].astype(BF16) for h in heads]
        vh = [cache_ref[pl.ds(D_HEADS + h, n_pos, stride=KV_ROWS), :].astype(BF16) for h in heads]
    lhs = [jnp.concatenate([q_ref[:, cols[h]], zeros_q], axis=0).astype(BF16) for h in heads]
    s_n = [_nt(lhs[h], padk_ref[h].astype(BF16))[0:t_new] * scale + bias_n_ref[h] + mask_n_ref[...]
           for h in heads]
    s_c = [_nt(lhs[h], kh[h])[0:t_new] * scale + bias_c_ref[h] + mask_c_ref[...] for h in heads]
    m = [jnp.maximum(jnp.max(s_n[h], axis=-1, keepdims=True), jnp.max(s_c[h], axis=-1, keepdims=True))
         for h in heads]
    p_n = [jnp.exp(s_n[h] - m[h]) for h in heads]
    p_c = [jnp.exp(s_c[h] - m[h]) for h in heads]
    l = [jnp.sum(p_n[h], axis=-1, keepdims=True) + jnp.sum(p_c[h], axis=-1, keepdims=True) for h in heads]
    acc = [_mm(pad16(p_n[h]), padv_ref[h].astype(BF16))[0:t_new] + _mm(pad16(p_c[h]), vh[h])[0:t_new]
           for h in heads]
    for h in heads:
        o_ref[:, cols[h]] = acc[h] / l[h]
        lse_ref[:, h:h + 1] = m[h] + jnp.log(l[h])


def _dilated_step(proj3, cache_flat, out_buf, layer, gi, dil, bias_c, mask_c, bias_n, mask_n):
    bsz, t_new, _ = proj3.shape
    flat_rows = cache_flat.shape[1]
    l_buf = flat_rows // KV_ROWS
    shift = t_new * KV_ROWS
    base = COL_DQKV // DBLK
    cq, ck, cv = base + gi, base + N_DG + gi, base + 2 * N_DG + gi
    grouped = t_new < dil
    if grouped:
        cache_arg = cache_flat.reshape(cache_flat.shape[0], l_buf // dil, dil * KV_ROWS, D_HD)
        cache_spec = pl.BlockSpec((None, l_buf // dil, shift, D_HD), lambda b: (layer * bsz + b, 0, 0, 0))
    else:
        cache_arg = cache_flat
        cache_spec = pl.BlockSpec((None, flat_rows, D_HD), lambda b: (layer * bsz + b, 0, 0))
    n_keys = bias_c.shape[-1]
    return pl.pallas_call(
        functools.partial(_step_attn_kernel, grouped=grouped),
        grid=(bsz,),
        in_specs=[
            pl.BlockSpec((None, t_new, DBLK), lambda b: (b, 0, cq)),
            pl.BlockSpec((None, t_new, DBLK), lambda b: (b, 0, ck)),
            pl.BlockSpec((None, t_new, DBLK), lambda b: (b, 0, cv)),
            cache_spec,
            pl.BlockSpec((D_HEADS, t_new, n_keys), lambda b: (0, 0, 0)),
            pl.BlockSpec((t_new, n_keys), lambda b: (0, 0)),
            pl.BlockSpec((D_HEADS, t_new, LANES), lambda b: (0, 0, 0)),
            pl.BlockSpec((t_new, LANES), lambda b: (0, 0)),
            pl.BlockSpec(memory_space=pl.ANY),
        ],
        out_specs=[
            pl.BlockSpec((None, t_new, DBLK), lambda b: (b, 0, 0)),
            pl.BlockSpec((None, t_new, D_HEADS), lambda b: (b, 0, 0)),
            pl.BlockSpec((None, shift, D_HD), lambda b: (layer * bsz + b, flat_rows // shift - 1, 0)),
        ],
        out_shape=[jax.ShapeDtypeStruct((bsz, t_new, DBLK), F32),
                   jax.ShapeDtypeStruct((bsz, t_new, D_HEADS), F32),
                   jax.ShapeDtypeStruct(out_buf.shape, F32)],
        scratch_shapes=[pltpu.VMEM((D_HEADS, LANES, D_HD), F32),
                        pltpu.VMEM((D_HEADS, LANES, D_HD), F32)],
        input_output_aliases={8: 2},
        compiler_params=_cparams(("parallel",)),
        name="dilated_step",
    )(proj3, proj3, proj3, cache_arg, bias_c, mask_c, bias_n, mask_n, out_buf)


def _win_kernel(k_ref, v_ref, buf_ref, out_ref):
    del buf_ref
    n_pos = k_ref.shape[0]
    for h in range(D_HEADS):
        cs = slice(h * D_HD, (h + 1) * D_HD)
        out_ref[pl.ds(h, n_pos, stride=KV_ROWS), :] = k_ref[:, cs]
        out_ref[pl.ds(D_HEADS + h, n_pos, stride=KV_ROWS), :] = v_ref[:, cs]


def _win_extract(proj3, out_buf, layer, gi, keep):
    bsz, t_len, _ = proj3.shape
    n_pos = min(keep, 512)
    first = (t_len - keep) // n_pos
    base = COL_DQKV // DBLK
    ck, cv = base + N_DG + gi, base + 2 * N_DG + gi
    return pl.pallas_call(
        _win_kernel,
        grid=(bsz, keep // n_pos),
        in_specs=[pl.BlockSpec((None, n_pos, DBLK), lambda b, c: (b, first + c, ck)),
                  pl.BlockSpec((None, n_pos, DBLK), lambda b, c: (b, first + c, cv)),
                  pl.BlockSpec(memory_space=pl.ANY)],
        out_specs=pl.BlockSpec((None, n_pos * KV_ROWS, D_HD), lambda b, c: (layer * bsz + b, c, 0)),
        out_shape=jax.ShapeDtypeStruct(out_buf.shape, F32),
        input_output_aliases={2: 0},
        compiler_params=_cparams(("parallel", "arbitrary")),
        name="win_extract",
    )(proj3, proj3, out_buf)


def _merge_kernel(*refs, split_d):
    if split_d:
        (gmix_ref, wgin_ref, ya_ref, yb_ref, yc_ref, o0_ref, o1_ref, o2_ref, l0_ref, l1_ref, l2_ref,
         wbr_ref, wo_ref, h_ref, out_ref) = refs
        l0, l1, l2 = l0_ref[...], l1_ref[...], l2_ref[...]
        m = jnp.maximum(jnp.maximum(l0, l1), l2)
        e0, e1, e2 = jnp.exp(l0 - m), jnp.exp(l1 - m), jnp.exp(l2 - m)
        inv = 1.0 / (e0 + e1 + e2)
        w0, w1, w2 = e0 * inv, e1 * inv, e2 * inv
        yd = []
        for h in range(D_HEADS):
            cs = slice(h * D_HD, (h + 1) * D_HD)
            yd.append(w0[:, h:h + 1] * o0_ref[:, cs] + w1[:, h:h + 1] * o1_ref[:, cs]
                      + w2[:, h:h + 1] * o2_ref[:, cs])
        y_d = jnp.concatenate(yd, axis=1)
    else:
        gmix_ref, wgin_ref, ya_ref, yb_ref, yc_ref, yd_ref, wbr_ref, wo_ref, h_ref, out_ref = refs
        y_d = yd_ref[...]
    branches = [ya_ref[...], yb_ref[...], yc_ref[...], y_d]
    h = h_ref[...]
    u = _rms_rows(h, gmix_ref[...]).astype(BF16)
    merged = None
    for nbr in range(N_BRANCH):
        up = _mm(branches[nbr].astype(BF16), wbr_ref[nbr])
        gate = _mm(u, wgin_ref[:, nbr * D_MODEL:(nbr + 1) * D_MODEL])
        term = _sigmoid(gate) * up
        merged = term if merged is None else merged + term
    out_ref[...] = h + _mm(merged.astype(BF16), wo_ref[...])


def _merge(g_mix, w_gin, ya, yb, yc, yd, w_br, w_o, layer, h2, tm):
    n = h2.shape[0]
    row = lambda width: pl.BlockSpec((tm, width), lambda i: (i, 0))
    split_d = isinstance(yd, tuple)
    if split_d:
        d_args = list(yd[0]) + list(yd[1])
        d_specs = [row(MIX_W)] * N_DG + [row(D_HEADS)] * N_DG
    else:
        d_args = [yd]
        d_specs = [row(MIX_W)]
    return pl.pallas_call(
        functools.partial(_merge_kernel, split_d=split_d),
        grid=(n // tm,),
        in_specs=[pl.BlockSpec((1, D_MODEL), lambda i: (0, 0)),
                  pl.BlockSpec((None, D_MODEL, GATE_W), lambda i: (layer, 0, 0), pipeline_mode=pl.Buffered(1)),
                  row(MIX_W), row(MIX_W), row(MIX_W)] + d_specs + [
                  pl.BlockSpec((None, N_BRANCH, MIX_W, D_MODEL), lambda i: (layer, 0, 0, 0),
                               pipeline_mode=pl.Buffered(1)),
                  pl.BlockSpec((None, D_MODEL, D_MODEL), lambda i: (layer, 0, 0), pipeline_mode=pl.Buffered(1)),
                  row(D_MODEL)],
        out_specs=row(D_MODEL),
        out_shape=jax.ShapeDtypeStruct((n, D_MODEL), F32),
        compiler_params=_cparams(("parallel",)),
        name="branch_merge",
    )(g_mix.reshape(1, D_MODEL), w_gin, ya, yb, yc, *d_args, w_br, w_o, h2)


def _ffn_kernel(h_ref, gffn_ref, wg_ref, wu_ref, wd_ref, p_ref, gple_ref, wpg_ref, wple_ref, gfin_ref,
                out_ref, *, final, n_split):
    h = h_ref[...]
    f = _rms_rows(h, gffn_ref[...]).astype(BF16)
    tf = D_FF // n_split
    acc = None
    for k in range(n_split):
        cs = slice(k * tf, (k + 1) * tf)
        a = _mm(f, wg_ref[:, cs])
        b = _mm(f, wu_ref[:, cs])
        part = _mm((_silu(a) * b).astype(BF16), wd_ref[cs, :])
        acc = part if acc is None else acc + part
    h2 = h + acc
    e = _rms_rows(h2, gple_ref[...]).astype(BF16)
    gate = _sigmoid(_mm(e, wpg_ref[...]))
    h3 = h2 + gate * _mm(p_ref[...].astype(BF16), wple_ref[...])
    if final:
        out_ref[...] = _rms_rows(h3, gfin_ref[...])
    else:
        out_ref[...] = h3


def _ffn(h2, g_ffn, w_gate, w_up, w_down, p3, g_ple, w_pg, w_ple, g_fin, layer, tm, n_split, final):
    n = h2.shape[0]
    vec = lambda: pl.BlockSpec((1, D_MODEL), lambda i: (0, 0))
    res = lambda shape: pl.BlockSpec((None,) + shape, lambda i: (layer, 0, 0), pipeline_mode=pl.Buffered(1))
    return pl.pallas_call(
        functools.partial(_ffn_kernel, final=final, n_split=n_split),
        grid=(n // tm,),
        in_specs=[pl.BlockSpec((tm, D_MODEL), lambda i: (i, 0)),
                  vec(),
                  res((D_MODEL, D_FF)),
                  res((D_MODEL, D_FF)),
                  res((D_FF, D_MODEL)),
                  pl.BlockSpec((None, tm, PLE_DIM), lambda i: (layer, i, 0)),
                  vec(),
                  res((D_MODEL, D_MODEL)),
                  res((PLE_DIM, D_MODEL)),
                  vec()],
        out_specs=pl.BlockSpec((tm, D_MODEL), lambda i: (i, 0)),
        out_shape=jax.ShapeDtypeStruct((n, D_MODEL), F32),
        compiler_params=_cparams(("parallel",)),
        name="ffn_ple",
    )(h2, g_ffn.reshape(1, -1), w_gate, w_up, w_down, p3, g_ple.reshape(1, -1), w_pg, w_ple,
      g_fin.reshape(1, -1))


def _prep_weights(w_in, w_br, w_o, w_gate, w_up, w_down, w_ple_gate, w_ple, a_log, dt_bias):
    o_ab = A_QKV + MIX_W
    o_bq = o_ab + 2 * A_HEADS
    o_d = o_bq + 2 * B_HEADS * B_DK + 3 * MIX_W
    o_g = o_d + 3 * N_DG * MIX_W
    w_main = jnp.concatenate([w_in[:, :, :o_ab], w_in[:, :, o_bq:o_g]], axis=-1).astype(BF16)
    w_gin = w_in[:, :, o_g:].astype(BF16)
    w_ab = jnp.pad(w_in[:, :, o_ab:o_bq], ((0, 0), (0, 0), (0, LANES - 2 * A_HEADS))).astype(BF16)
    par = jnp.zeros((a_log.shape[0], SUBLANES, LANES), F32)
    par = par.at[:, 0, A_HEADS:2 * A_HEADS].set(a_log)
    par = par.at[:, 1, A_HEADS:2 * A_HEADS].set(dt_bias)
    return dict(w_main=w_main, w_gin=w_gin, w_ab=w_ab, par=par, w_br=w_br.astype(BF16), w_o=w_o.astype(BF16),
                w_gate=w_gate.astype(BF16), w_up=w_up.astype(BF16), w_down=w_down.astype(BF16),
                w_pg=w_ple_gate.astype(BF16), w_ple=w_ple.astype(BF16))


def _band_bias(t5_bias):
    i = np.arange(D_SPAN)[:, None]
    j = np.arange(2 * D_SPAN)[None, :]
    rel = i + D_SPAN - j
    band = (rel >= 0) & (rel <= D_SPAN)
    masks = np.stack([band & (j >= D_SPAN), band])
    out = []
    for gi, dil in enumerate(D_DILATIONS):
        idx = _t5_bucket_np(rel * dil).reshape(-1)
        tab = _bias_lookup(idx, t5_bias[:, gi * D_HEADS:(gi + 1) * D_HEADS])
        tab = tab.reshape(D_SPAN, 2 * D_SPAN, D_HEADS).transpose(2, 0, 1)
        out.append(jnp.where(masks[None], tab[:, None], -jnp.inf))
    return jnp.stack(out)


def _step_bias(t5_bias, l_buf, t_new, gi, dil):
    t = np.arange(t_new)[:, None]
    j = np.arange(l_buf + LANES)[None, :]
    dist = l_buf + t - j
    valid = (dist >= 0) & (dist % dil == 0) & (dist <= D_SPAN * dil) & (j < l_buf + t_new)
    idx = _t5_bucket_np(np.where(valid, dist, 0)).reshape(-1)
    tab = _bias_lookup(idx, t5_bias[:, gi * D_HEADS:(gi + 1) * D_HEADS])
    tab = tab.reshape(t_new, l_buf + LANES, D_HEADS).transpose(2, 0, 1)
    tab = jnp.where(valid[None], tab, 0.0)
    mask = jnp.asarray(np.where(valid, 0.0, -np.inf), dtype=F32)
    pos = np.arange(l_buf)
    keep = pos[(pos % dil) < t_new] if t_new < dil else pos
    return tab[:, :, keep], mask[:, keep], tab[:, :, l_buf:], mask[:, l_buf:]


def _layer_common(h2, p3, bsz, t_len, wts, lw, i, mix_fn, tm, final, g_final):
    n = bsz * t_len
    proj2, ab2 = _norm_matmul(h2, lw['g_mix'][i], wts['w_main'], wts['w_ab'], i, tm, PROJ_W // 4)
    proj3 = proj2.reshape(bsz, t_len, PROJ_W)
    ab3 = ab2.reshape(bsz, t_len, LANES)
    ya, yb, yc, yd, states = mix_fn(proj3, ab3)
    flat = lambda x: x.reshape(n, x.shape[-1])
    if isinstance(yd, tuple):
        yd = ([flat(x) for x in yd[0]], [flat(x) for x in yd[1]])
    else:
        yd = flat(yd)
    tm2 = min(tm, 512)
    h2 = _merge(lw['g_mix'][i], wts['w_gin'], flat(ya), flat(yb), flat(yc), yd, wts['w_br'], wts['w_o'], i, h2, tm2)
    tm3 = min(tm, 512)
    h2 = _ffn(h2, lw['g_ffn'][i], wts['w_gate'], wts['w_up'], wts['w_down'], p3,
              lw['g_ple'][i], wts['w_pg'], wts['w_ple'], g_final, i, tm3, 2, final)
    return h2, proj3, states


def _run_prompt(x, p, wts, lw, t5_bias, g_final, depth, host):
    bsz, t_len, _ = x.shape
    n = bsz * t_len
    h2 = x.reshape(n, D_MODEL)
    p3 = p.reshape(depth, n, PLE_DIM)
    pos = jnp.arange(t_len, dtype=jnp.int32)
    cos_t, sin_t = _rope_tables(pos, 1)
    band_bias = _band_bias(t5_bias)
    zero_s = jnp.zeros((1, bsz, A_HEADS, A_DK, A_DV), F32)
    zero_c = jnp.zeros((bsz, A_CONV - 1, A_QKV), F32)
    zero_r = jnp.zeros((bsz, B_HEADS // 2, LANES, B_DV), F32)
    zero_p = jnp.zeros((bsz, POOL_MAX - 1, MIX_W), F32)
    keeps = [min(D_SPAN * dil, t_len) for dil in D_DILATIONS]
    win_bufs = [lax.empty((depth * bsz, keep * KV_ROWS, D_HD), F32) for keep in keeps]
    per_layer = []
    for i in range(depth):
        def mix_fn(proj3, ab3, i=i):
            ya, s_new, *shifted = _mixer_delta(
                proj3, ab3, zero_c, zero_s, 0, lw['conv_w'][i], wts['par'][i], lw['a_gain'][i].reshape(1, A_DV),
                n_s=1, l_r=256, chunk=A_CHUNK, carry=True,
                host=(host[0], host[1], i * host[2], host[2], host[3]))
            host[1][:] = shifted
            yb, r_new = _mixer_ret(proj3, cos_t, sin_t, zero_r, n_s=1, l_r=512, chunk=B_CHUNK, carry=True)
            yc = _mixer_pool(proj3, zero_p, lw['c_w'][i].astype(BF16), lw['c_scale'][i].reshape(1, MIX_W),
                             n_s=1, l_r=1024, pos0=0)
            yd = _dilated_band(proj3, band_bias)
            return ya, yb, yc, yd, (s_new, r_new)

        h2, proj3, (s_new, r_new) = _layer_common(h2, p3, bsz, t_len, wts, lw, i,
                                                  mix_fn, 1024, i == depth - 1, g_final)
        for gi in range(N_DG):
            win_bufs[gi] = _win_extract(proj3, win_bufs[gi], i, gi, keeps[gi])
        per_layer.append((s_new,
                          proj3[:, t_len - (A_CONV - 1):, COL_AQKV:COL_AQKV + A_QKV],
                          r_new.reshape(bsz, B_HEADS, B_DK, B_DV),
                          proj3[:, t_len - (POOL_MAX - 1):, COL_CIN:COL_CIN + MIX_W]))
    stacked = tuple(jnp.stack([ns[j] for ns in per_layer]) for j in range(4))
    wins = tuple(win_bufs[gi].reshape(depth, bsz, keeps[gi], 2, D_HEADS, D_HD) for gi in range(N_DG))
    return h2.reshape(bsz, t_len, D_MODEL), stacked + wins


def _run_sample(x, p, states, wts, lw, t5_bias, g_final, depth, cache_flat, out_bufs):
    s_delta, s_conv, s_ret, s_pool, caches = states
    bsz, t_len, _ = x.shape
    n = bsz * t_len
    n_s = 16
    h2 = x.reshape(n, D_MODEL)
    p3 = p.reshape(depth, n, PLE_DIM)
    pos = PAST_LEN + jnp.arange(t_len, dtype=jnp.int32)
    cos_t, sin_t = _rope_tables(pos, n_s)
    step_bias = [_step_bias(t5_bias, caches[gi].shape[2], t_len, gi, dil) for gi, dil in enumerate(D_DILATIONS)]
    s_ret2 = s_ret.reshape(s_ret.shape[0], bsz, B_HEADS // 2, LANES, B_DV)
    per_layer = []
    for i in range(depth):
        def mix_fn(proj3, ab3, i=i):
            ya, s_new = _mixer_delta(proj3, ab3, s_conv[i], s_delta, i, lw['conv_w'][i], wts['par'][i],
                                     lw['a_gain'][i].reshape(1, A_DV), n_s=n_s, l_r=t_len, chunk=t_len, carry=False)
            yb, r_new = _mixer_ret(proj3, cos_t, sin_t, s_ret2[i], n_s=n_s, l_r=t_len, chunk=t_len, carry=False)
            yc = _mixer_pool(proj3, s_pool[i], lw['c_w'][i].astype(BF16), lw['c_scale'][i].reshape(1, MIX_W),
                             n_s=n_s, l_r=t_len, pos0=PAST_LEN)
            od, lse = [], []
            for gi, dil in enumerate(D_DILATIONS):
                o, l, out_bufs[gi] = _dilated_step(proj3, cache_flat[gi], out_bufs[gi], i, gi, dil, *step_bias[gi])
                od.append(o)
                lse.append(l)
            return ya, yb, yc, (od, lse), (s_new, r_new)

        h2, proj3, (s_new, r_new) = _layer_common(h2, p3, bsz, t_len, wts, lw, i,
                                                  mix_fn, n, i == depth - 1, g_final)
        pool_new = jnp.concatenate([s_pool[i][:, t_len:], proj3[:, :, COL_CIN:COL_CIN + MIX_W]], axis=1)
        per_layer.append((s_new,
                          proj3[:, t_len - (A_CONV - 1):, COL_AQKV:COL_AQKV + A_QKV],
                          r_new.reshape(bsz, B_HEADS, B_DK, B_DV),
                          pool_new))
    stacked = tuple(jnp.stack([ns[j] for ns in per_layer]) for j in range(4))
    wins = tuple(out_bufs[gi].reshape(caches[gi].shape) for gi in range(N_DG))
    return h2.reshape(bsz, t_len, D_MODEL), stacked + wins


def kernel(x_prompt, x_sample, state_delta, state_delta_conv, state_ret, state_pool, cache_win0, cache_win1, cache_win2, p_prompt, p_sample, g_mix, w_in, conv_w, a_log, dt_bias, a_gain, c_w, c_scale, t5_bias, w_br, w_o, g_ffn, w_gate, w_up, w_down, g_ple, w_ple_gate, w_ple, g_final):
    depth = w_in.shape[0]
    wts = _prep_weights(w_in, w_br, w_o, w_gate, w_up, w_down, w_ple_gate, w_ple, a_log, dt_bias)
    lw = dict(g_mix=g_mix, conv_w=conv_w, a_gain=a_gain, c_w=c_w, c_scale=c_scale, g_ffn=g_ffn, g_ple=g_ple)
    caches = (cache_win0, cache_win1, cache_win2)
    bsz_s, t_new = x_sample.shape[0], x_sample.shape[1]
    cache_flat = [c.reshape(c.shape[0] * bsz_s, c.shape[2] * KV_ROWS, D_HD) for c in caches]
    out_bufs = [lax.empty(c.shape, F32) for c in cache_flat]
    y_p, (delta_p, conv_p, ret_p, pool_p, win0_p, win1_p, win2_p) = _run_prompt(
        x_prompt, p_prompt, wts, lw, t5_bias, g_final, depth, (cache_flat, out_bufs, bsz_s, t_new * KV_ROWS))
    y_s, (delta_s, conv_s, ret_s, pool_s, win0_s, win1_s, win2_s) = _run_sample(
        x_sample, p_sample, (state_delta, state_delta_conv, state_ret, state_pool, caches),
        wts, lw, t5_bias, g_final, depth, cache_flat, out_bufs)
    return (y_p, y_s, delta_p, delta_s, conv_p, conv_s, ret_p, ret_s,
            pool_p, pool_s, win0_p, win0_s, win1_p, win1_s, win2_p, win2_s)
```

```python
import functools
import math

import numpy as np
import jax
import jax.numpy as jnp
from jax import lax
from jax.experimental import pallas as pl
from jax.experimental.pallas import tpu as pltpu

F32 = jnp.float32
BF16 = jnp.bfloat16
HIGHEST = lax.Precision.HIGHEST

D_MODEL = 1024
DEPTH = 4
PAST_LEN = 8192
MIX_W = D_MODEL // 2
N_BRANCH = 4
A_HEADS = 4
A_DK = MIX_W // A_HEADS
A_DV = MIX_W // A_HEADS
A_QKV = A_HEADS * (2 * A_DK + A_DV)
A_CONV = 4
A_CHUNK = 64
B_HEADS = 4
B_DV = MIX_W // B_HEADS
B_DK = B_DV // 2
B_CHUNK = 128
ROPE_BASE = 10000.0
C_GROUPS = 4
C_GW = MIX_W // C_GROUPS
POOL_WINDOWS = (2, 4, 8, 16)
POOL_MAX = 16
D_DILATIONS = (1, 4, 16)
N_DG = 3
D_SPAN = 128
D_HEADS = 4
D_HD = MIX_W // D_HEADS
REL_BUCKETS = 32
REL_MAX_DIST = D_SPAN * 16
D_FF = -(-8 * D_MODEL // (3 * 256)) * 256
PLE_DIM = 256
EPS = 1e-6
LOG2E = math.log2(math.e)

LANES = 128
SUBLANES = 8
VMEM_LIMIT_BYTES = 56 * 1024 * 1024

PROJ_W = 8704
COL_AQKV = 0
COL_AZ = 1536
COL_BQ = 2048
COL_BK = 2304
COL_BV = 2560
COL_BG = 3072
COL_CIN = 3584
COL_DQKV = 4096
GATE_W = N_BRANCH * D_MODEL
DBLK = D_HEADS * D_HD


def _cparams(sem):
    return pltpu.CompilerParams(dimension_semantics=sem, vmem_limit_bytes=VMEM_LIMIT_BYTES)


def _nt(a, b, precision=None):
    return lax.dot_general(a, b, (((1,), (1,)), ((), ())), precision=precision,
                           preferred_element_type=F32)


def _tn(a, b):
    return lax.dot_general(a, b, (((0,), (0,)), ((), ())), preferred_element_type=F32)


def _mm(a, b, precision=None):
    return jnp.dot(a, b, precision=precision, preferred_element_type=F32)


def _sigmoid(x):
    return 1.0 / (1.0 + jnp.exp(-x))


def _silu(x):
    return x * _sigmoid(x)


def _rms_rows(x, g):
    return x * lax.rsqrt(jnp.mean(x * x, axis=-1, keepdims=True) + EPS) * g


def _norm_matmul_kernel(x_ref, g_ref, w_ref, wab_ref, o_ref, ab_ref, u_ref):
    @pl.when(pl.program_id(1) == 0)
    def _():
        u_ref[...] = _rms_rows(x_ref[...], g_ref[...]).astype(BF16)
        ab_ref[...] = _mm(u_ref[...], wab_ref[...])

    o_ref[...] = _mm(u_ref[...], w_ref[...])


def _norm_matmul(x, g, w, w_ab, layer, tm, tn):
    n, d = x.shape
    wcols = w.shape[2]
    return pl.pallas_call(
        _norm_matmul_kernel,
        grid=(n // tm, wcols // tn),
        in_specs=[pl.BlockSpec((tm, d), lambda i, j: (i, 0)),
                  pl.BlockSpec((1, d), lambda i, j: (0, 0)),
                  pl.BlockSpec((None, d, tn), lambda i, j: (layer, 0, j)),
                  pl.BlockSpec((None, d, LANES), lambda i, j: (layer, 0, 0))],
        out_specs=[pl.BlockSpec((tm, tn), lambda i, j: (i, j)),
                   pl.BlockSpec((tm, LANES), lambda i, j: (i, 0))],
        out_shape=[jax.ShapeDtypeStruct((n, wcols), F32),
                   jax.ShapeDtypeStruct((n, LANES), F32)],
        scratch_shapes=[pltpu.VMEM((tm, d), BF16)],
        compiler_params=_cparams(("parallel", "arbitrary")),
        name="norm_matmul",
    )(x, g.reshape(1, d), w, w_ab)


def _delta_kernel(qkv_ref, z_ref, ab_ref, cprev_ref, sprev_ref, convw_ref, par_ref, gain_ref,
                  *rest, chunk, carry, n_host, host_chunks):
    host_in = rest[:3 * n_host]
    o_ref, snew_ref = rest[3 * n_host:3 * n_host + 2]
    host_out = rest[3 * n_host + 2:4 * n_host + 2]
    ext_ref, s_ref = rest[4 * n_host + 2:]
    j = pl.program_id(1)
    for g in range(n_host):
        src, nxt, dst = host_in[3 * g], host_in[3 * g + 1], host_out[g]
        rows, tail = src.shape[0], nxt.shape[0]
        dst[0:rows - tail, :] = src[tail:rows, :]
        last = (j % host_chunks) == host_chunks - 1
        dst[rows - tail:rows, :] = jnp.where(last, 0.0, nxt[...])
    n_s, l_r, _ = qkv_ref.shape
    t_b = n_s * l_r
    n_c = t_b // chunk
    hdr = SUBLANES
    pre = A_CONV - 1

    @pl.when(j == 0)
    def _():
        ext_ref[:, 0:hdr, :] = jnp.zeros((n_s, hdr, A_QKV), F32)
        ext_ref[:, hdr - pre:hdr, :] = cprev_ref[...]
        if carry:
            s_ref[...] = sprev_ref[0]

    if carry:
        @pl.when(j > 0)
        def _():
            ext_ref[:, 0:hdr, :] = ext_ref[:, l_r:l_r + hdr, :]

    ext_ref[:, hdr:hdr + l_r, :] = qkv_ref[...]
    cw = convw_ref[...]

    def conv_cols(c0):
        acc = None
        for t in range(A_CONV):
            term = (ext_ref[:, hdr - pre + t:hdr - pre + t + l_r, c0:c0 + LANES]
                    * cw[t:t + 1, c0:c0 + LANES])
            acc = term if acc is None else acc + term
        return _silu(acc).reshape(t_b, LANES)

    ab = ab_ref[...].reshape(t_b, LANES)
    par = par_ref[...]
    beta_all = _sigmoid(ab)
    xs = ab + par[1:2, :]
    softplus = jnp.maximum(xs, 0.0) + jnp.log1p(jnp.exp(-jnp.abs(xs)))
    g_all = -jnp.exp(par[0:1, :]) * softplus

    sh = int(math.log2(chunk))
    ri = lax.broadcasted_iota(jnp.int32, (t_b, t_b), 0)
    ci = lax.broadcasted_iota(jnp.int32, (t_b, t_b), 1)
    same = lax.shift_right_logical(ri, sh) == lax.shift_right_logical(ci, sh)
    causal = jnp.logical_and(same, ci <= ri)
    strict = jnp.logical_and(same, ci < ri)
    gc_all = _mm(causal.astype(F32), g_all, precision=HIGHEST)
    sel = (lax.broadcasted_iota(jnp.int32, (SUBLANES, LANES), 0)
           == lax.broadcasted_iota(jnp.int32, (SUBLANES, LANES), 1)).astype(F32)
    gc_rows = _nt(sel, gc_all, precision=HIGHEST)
    glast_all = _mm(same.astype(F32), g_all, precision=HIGHEST)

    gain = gain_ref[...]
    z = z_ref[...].reshape(t_b, MIX_W)
    row_id = lax.broadcasted_iota(jnp.int32, (t_b, 1), 0)
    heads = range(A_HEADS)

    gcc = [gc_all[:, A_HEADS + h:A_HEADS + h + 1] for h in heads]
    gcr = [gc_rows[A_HEADS + h:A_HEADS + h + 1, :] for h in heads]
    glast = [glast_all[:, A_HEADS + h:A_HEADS + h + 1] for h in heads]
    bcol = [beta_all[:, h:h + 1] for h in heads]
    q = [conv_cols(h * A_DK) for h in heads]
    k = [conv_cols(A_HEADS * A_DK + h * A_DK) for h in heads]
    v = [conv_cols(2 * A_HEADS * A_DK + h * A_DV) for h in heads]
    q = [x * lax.rsqrt(jnp.sum(x * x, axis=-1, keepdims=True) + EPS) * (A_DK ** -0.5) for x in q]
    k = [x * lax.rsqrt(jnp.sum(x * x, axis=-1, keepdims=True) + EPS) for x in k]
    kb = [x.astype(BF16) for x in k]
    kk = [_nt(kb[h], kb[h]) for h in heads]
    qkr = [_nt(q[h].astype(BF16), kb[h]) for h in heads]
    gam = [jnp.where(causal, jnp.exp(jnp.where(causal, gcc[h] - gcr[h], 0.0)), 0.0) for h in heads]
    qk = [(qkr[h] * gam[h]).astype(BF16) for h in heads]
    a = [jnp.where(strict, bcol[h] * kk[h] * gam[h], 0.0) for h in heads]
    r = [-x for x in a]
    p = a
    for _ in range(sh - 1):
        pb = [x.astype(BF16) for x in p]
        p = [_mm(x, x) for x in pb]
        pr = [_mm(p[h].astype(BF16), r[h].astype(BF16)) for h in heads]
        r = [r[h] + p[h] + pr[h] for h in heads]
    ecol = [jnp.exp(x) for x in gcc]
    rhs = [jnp.concatenate([v[h] * bcol[h], k[h] * (bcol[h] * ecol[h])], axis=1) for h in heads]
    sol = [rhs[h] + _mm(r[h].astype(BF16), rhs[h].astype(BF16)) for h in heads]
    u = [x[:, :A_DV] for x in sol]
    w = [x[:, A_DV:] for x in sol]
    qe = [q[h] * ecol[h] for h in heads]
    kd = [(k[h] * jnp.exp(glast[h] - gcc[h])).astype(BF16) for h in heads]
    sdec = [jnp.exp(x) for x in glast]
    vnews = [[] for _ in heads]
    qss = [[] for _ in heads]
    if carry:
        s = [s_ref[h] for h in heads]
        for c in range(n_c):
            r0 = c * chunk
            wq = [jnp.concatenate([w[h][r0:r0 + chunk], qe[h][r0:r0 + chunk]], axis=0).astype(BF16)
                  for h in heads]
            t = [_mm(wq[h], s[h].astype(BF16)) for h in heads]
            vnew = [u[h][r0:r0 + chunk] - t[h][:chunk] for h in heads]
            upd = [_tn(kd[h][r0:r0 + chunk], vnew[h].astype(BF16)) for h in heads]
            s = [s[h] * sdec[h][r0:r0 + 1, :] + upd[h] for h in heads]
            for h in heads:
                vnews[h].append(vnew[h])
                qss[h].append(t[h][chunk:])
        vn_all = [jnp.concatenate(x, axis=0) if n_c > 1 else x[0] for x in vnews]
        vnb = [x.astype(BF16) for x in vn_all]
    else:
        olds = [[] for _ in heads]
        for c in range(n_c):
            r0 = c * chunk
            for h in heads:
                s_c = sprev_ref[c, h]
                olds[h].append(s_c)
                wq = jnp.concatenate([w[h][r0:r0 + chunk], qe[h][r0:r0 + chunk]], axis=0).astype(BF16)
                t = _mm(wq, s_c.astype(BF16))
                vnews[h].append(u[h][r0:r0 + chunk] - t[:chunk])
                qss[h].append(t[chunk:])
        vn_all = [jnp.concatenate(x, axis=0) for x in vnews]
        vnb = [x.astype(BF16) for x in vn_all]
        for c in range(n_c):
            r0 = c * chunk
            in_c = jnp.logical_and(row_id >= r0, row_id < r0 + chunk)
            for h in heads:
                upd = _tn(jnp.where(in_c, kd[h], jnp.zeros_like(kd[h])),
                          jnp.where(in_c, vnb[h], jnp.zeros_like(vnb[h])))
                snew_ref[c, h] = olds[h][c] * sdec[h][r0:r0 + 1, :] + upd
    outs = []
    for h in heads:
        qs_all = jnp.concatenate(qss[h], axis=0) if n_c > 1 else qss[h][0]
        o = qs_all + _mm(qk[h], vnb[h])
        o = o * lax.rsqrt(jnp.mean(o * o, axis=-1, keepdims=True) + EPS) * gain
        outs.append(o * _silu(z[:, h * A_DV:(h + 1) * A_DV]))
    o_ref[...] = jnp.concatenate(outs, axis=1).reshape(n_s, l_r, MIX_W)

    if carry:
        for h in heads:
            s_ref[h] = s[h]

        @pl.when(j == pl.num_programs(1) - 1)
        def _():
            for h in heads:
                snew_ref[0, h] = s[h]


def _mixer_delta(proj3, ab3, conv_prev, s_prev, layer, conv_w, par, gain, *, n_s, l_r, chunk, carry,
                 host=None):
    bsz, t_len, _ = proj3.shape
    grid = (bsz // n_s, t_len // l_r)
    host_specs, host_args, host_out_specs, host_out_shapes, aliases = [], [], [], [], {}
    n_host, host_chunks = 0, 1
    if host is not None:
        bufs, dsts, slab0, n_slabs, tail = host
        n_host = len(bufs)
        steps = grid[0] * grid[1]
        host_chunks = steps // n_slabs
        per_b = grid[1] // host_chunks
        assert steps == host_chunks * n_slabs and grid[1] == per_b * host_chunks
        for g, (buf, dst) in enumerate(zip(bufs, dsts)):
            rows = buf.shape[1] // host_chunks
            per, last = rows // tail, buf.shape[1] // tail - 1
            assert rows % tail == 0 and rows * host_chunks == buf.shape[1]
            slab = lambda b, j: slab0 + b * per_b + j // host_chunks
            host_specs += [
                pl.BlockSpec((None, rows, LANES), lambda b, j, slab=slab: (slab(b, j), j % host_chunks, 0)),
                pl.BlockSpec((None, tail, LANES), lambda b, j, slab=slab, per=per, last=last: (
                    slab(b, j), jnp.minimum((j % host_chunks + 1) * per, last), 0)),
                pl.BlockSpec(memory_space=pl.ANY)]
            host_args += [buf, buf, dst]
            host_out_specs.append(
                pl.BlockSpec((None, rows, LANES), lambda b, j, slab=slab: (slab(b, j), j % host_chunks, 0)))
            host_out_shapes.append(jax.ShapeDtypeStruct(dst.shape, F32))
            aliases[8 + 3 * g + 2] = 2 + g
    kern = functools.partial(_delta_kernel, chunk=chunk, carry=carry, n_host=n_host, host_chunks=host_chunks)
    return pl.pallas_call(
        kern,
        grid=grid,
        in_specs=[
            pl.BlockSpec((n_s, l_r, A_QKV), lambda b, j: (b, j, COL_AQKV // A_QKV)),
            pl.BlockSpec((n_s, l_r, MIX_W), lambda b, j: (b, j, COL_AZ // MIX_W)),
            pl.BlockSpec((n_s, l_r, LANES), lambda b, j: (b, j, 0)),
            pl.BlockSpec((n_s, A_CONV - 1, A_QKV), lambda b, j: (b, 0, 0)),
            pl.BlockSpec((None, n_s, A_HEADS, A_DK, A_DV), lambda b, j: (layer, b, 0, 0, 0)),
            pl.BlockSpec((A_CONV, A_QKV), lambda b, j: (0, 0)),
            pl.BlockSpec((SUBLANES, LANES), lambda b, j: (0, 0)),
            pl.BlockSpec((1, A_DV), lambda b, j: (0, 0)),
        ] + host_specs,
        out_specs=[
            pl.BlockSpec((n_s, l_r, MIX_W), lambda b, j: (b, j, 0)),
            pl.BlockSpec((n_s, A_HEADS, A_DK, A_DV), lambda b, j: (b, 0, 0, 0)),
        ] + host_out_specs,
        out_shape=[jax.ShapeDtypeStruct((bsz, t_len, MIX_W), F32),
                   jax.ShapeDtypeStruct((bsz, A_HEADS, A_DK, A_DV), F32)] + host_out_shapes,
        scratch_shapes=[pltpu.VMEM((n_s, SUBLANES + l_r + SUBLANES, A_QKV), F32),
                        pltpu.VMEM((A_HEADS, A_DK, A_DV), F32)],
        input_output_aliases=aliases,
        compiler_params=_cparams(("parallel", "arbitrary")),
        name="mixer_delta",
    )(proj3, proj3, ab3, conv_prev, s_prev, conv_w, par, gain, *host_args)


def _ret_log_gamma(h):
    return math.log1p(-(2.0 ** (-5.0 - h)))


def _ret_kernel(q_ref, k_ref, v_ref, g_ref, cos_ref, sin_ref, rprev_ref,
                o_ref, rnew_ref, r_ref, *, chunk, carry):
    j = pl.program_id(1)
    n_s, l_r, _ = q_ref.shape
    t_b = n_s * l_r
    n_c = t_b // chunk
    half = B_DK // 2

    if carry:
        @pl.when(j == 0)
        def _():
            r_ref[...] = rprev_ref[0]

    cos = cos_ref[...]
    sin = sin_ref[...]
    lane = lax.broadcasted_iota(jnp.int32, (1, LANES), 1)
    first_half = (lane % B_DK) < half
    lane_head = lane // B_DK

    def rope(x):
        swapped = jnp.where(first_half, pltpu.roll(x, LANES - half, axis=1),
                            pltpu.roll(x, half, axis=1))
        return x * cos + swapped * sin

    idx_i = lax.broadcasted_iota(jnp.int32, (chunk, chunk), 0)
    idx_j = lax.broadcasted_iota(jnp.int32, (chunk, chunk), 1)
    pos_col = lax.broadcasted_iota(jnp.int32, (chunk, 1), 0).astype(F32)
    row_head = lax.broadcasted_iota(jnp.int32, (LANES, 1), 0) // B_DK
    gate = g_ref[...].reshape(t_b, MIX_W)
    vv = v_ref[...].reshape(t_b, MIX_W)

    small = chunk < 2 * SUBLANES
    if small:
        sh = int(math.log2(chunk))
        bi = lax.broadcasted_iota(jnp.int32, (t_b, t_b), 0)
        bj = lax.broadcasted_iota(jnp.int32, (t_b, t_b), 1)
        bsame = lax.shift_right_logical(bi, sh) == lax.shift_right_logical(bj, sh)
        bcausal = jnp.logical_and(bsame, bj <= bi)
        bdiff = (bi - bj).astype(F32)
        row_id = lax.broadcasted_iota(jnp.int32, (t_b, 1), 0)
        pos_in = (row_id & (chunk - 1)).astype(F32)

    outs = [None] * B_HEADS
    for p in range(B_HEADS // 2):
        q2 = rope(q_ref[...].reshape(t_b, B_HEADS * B_DK)[:, p * LANES:(p + 1) * LANES])
        k2 = rope(k_ref[...].reshape(t_b, B_HEADS * B_DK)[:, p * LANES:(p + 1) * LANES]) * (B_DK ** -0.5)
        lgs = [_ret_log_gamma(2 * p + hh) for hh in range(2)]
        qh = [jnp.where(lane_head == hh, q2, 0.0) for hh in range(2)]
        kh = [jnp.where(lane_head == hh, k2, 0.0) for hh in range(2)]
        vh = [vv[:, (2 * p + hh) * B_DV:(2 * p + hh + 1) * B_DV] for hh in range(2)]
        row_scale = jnp.where(row_head == 0, math.exp(lgs[0] * chunk), math.exp(lgs[1] * chunk))
        if small:
            intra = []
            for hh in range(2):
                dec = jnp.where(bcausal, jnp.exp(lgs[hh] * jnp.where(bcausal, bdiff, 0.0)), 0.0)
                sc = _nt(qh[hh].astype(BF16), kh[hh].astype(BF16)) * dec
                intra.append(_mm(sc.astype(BF16), vh[hh].astype(BF16)))
            cross = [[], []]
            for c in range(n_c):
                r0 = c * chunk
                rp = r_ref[p] if carry else rprev_ref[c, p]
                qq = jnp.concatenate([qh[0][r0:r0 + chunk], qh[1][r0:r0 + chunk]], axis=0)
                t = _mm(qq.astype(BF16), rp.astype(BF16))
                cross[0].append(t[:chunk])
                cross[1].append(t[chunk:])
                in_c = jnp.logical_and(row_id >= r0, row_id < r0 + chunk)
                upd = None
                for hh in range(2):
                    kd = jnp.where(in_c, kh[hh] * jnp.exp(lgs[hh] * (chunk - 1.0 - pos_in)), 0.0)
                    term = _tn(kd.astype(BF16), vh[hh].astype(BF16))
                    upd = term if upd is None else upd + term
                rn = rp * row_scale + upd
                if carry:
                    r_ref[p] = rn
                else:
                    rnew_ref[c, p] = rn
            for hh in range(2):
                cr = jnp.concatenate(cross[hh], axis=0) * jnp.exp(lgs[hh] * (pos_in + 1.0))
                outs[2 * p + hh] = intra[hh] + cr
        else:
            pieces = [[], []]
            for c in range(n_c):
                r0 = c * chunk
                rp = r_ref[p] if carry else rprev_ref[c, p]
                rpb = rp.astype(BF16)
                upd = None
                for hh in range(2):
                    qc = qh[hh][r0:r0 + chunk].astype(BF16)
                    kc = kh[hh][r0:r0 + chunk]
                    vc = vh[hh][r0:r0 + chunk].astype(BF16)
                    causal = idx_j <= idx_i
                    dec = jnp.where(causal, jnp.exp(lgs[hh] * jnp.where(causal, (idx_i - idx_j).astype(F32), 0.0)), 0.0)
                    sc = _nt(qc, kc.astype(BF16)) * dec
                    oc = _mm(sc.astype(BF16), vc) + _mm(qc, rpb) * jnp.exp(lgs[hh] * (pos_col + 1.0))
                    pieces[hh].append(oc)
                    kd = kc * jnp.exp(lgs[hh] * (chunk - 1.0 - pos_col))
                    term = _tn(kd.astype(BF16), vc)
                    upd = term if upd is None else upd + term
                rn = rp * row_scale + upd
                if carry:
                    r_ref[p] = rn
                else:
                    rnew_ref[c, p] = rn
            for hh in range(2):
                outs[2 * p + hh] = jnp.concatenate(pieces[hh], axis=0) if n_c > 1 else pieces[hh][0]

    res = []
    for h in range(B_HEADS):
        o = outs[h]
        o = o * lax.rsqrt(jnp.mean(o * o, axis=-1, keepdims=True) + EPS)
        res.append(o * _silu(gate[:, h * B_DV:(h + 1) * B_DV]))
    o_ref[...] = jnp.concatenate(res, axis=1).reshape(n_s, l_r, MIX_W)

    if carry:
        @pl.when(j == pl.num_programs(1) - 1)
        def _():
            rnew_ref[0] = r_ref[...]


def _mixer_ret(proj3, cos_t, sin_t, r_prev, *, n_s, l_r, chunk, carry):
    bsz, t_len, _ = proj3.shape
    t_b = n_s * l_r
    grid = (bsz // n_s, t_len // l_r)
    n_tab = cos_t.shape[0] // t_b
    kern = functools.partial(_ret_kernel, chunk=chunk, carry=carry)
    qw = B_HEADS * B_DK
    return pl.pallas_call(
        kern,
        grid=grid,
        in_specs=[
            pl.BlockSpec((n_s, l_r, qw), lambda b, j: (b, j, COL_BQ // qw)),
            pl.BlockSpec((n_s, l_r, qw), lambda b, j: (b, j, COL_BK // qw)),
            pl.BlockSpec((n_s, l_r, MIX_W), lambda b, j: (b, j, COL_BV // MIX_W)),
            pl.BlockSpec((n_s, l_r, MIX_W), lambda b, j: (b, j, COL_BG // MIX_W)),
            pl.BlockSpec((t_b, LANES), lambda b, j: (j % n_tab, 0)),
            pl.BlockSpec((t_b, LANES), lambda b, j: (j % n_tab, 0)),
            pl.BlockSpec((n_s, B_HEADS // 2, LANES, B_DV), lambda b, j: (b, 0, 0, 0)),
        ],
        out_specs=[
            pl.BlockSpec((n_s, l_r, MIX_W), lambda b, j: (b, j, 0)),
            pl.BlockSpec((n_s, B_HEADS // 2, LANES, B_DV), lambda b, j: (b, 0, 0, 0)),
        ],
        out_shape=[jax.ShapeDtypeStruct((bsz, t_len, MIX_W), F32),
                   jax.ShapeDtypeStruct((bsz, B_HEADS // 2, LANES, B_DV), F32)],
        scratch_shapes=[pltpu.VMEM((B_HEADS // 2, LANES, B_DV), F32)],
        compiler_params=_cparams(("parallel", "arbitrary")),
        name="mixer_ret",
    )(proj3, proj3, proj3, proj3, cos_t, sin_t, r_prev)


def _rope_tables(pos, reps):
    half = B_DK // 2
    inv = ROPE_BASE ** (-jnp.arange(half, dtype=F32) / half)
    ang = pos.astype(F32)[:, None] * inv[None, :]
    cos = jnp.cos(ang)
    sin = jnp.sin(ang)
    cos_t = jnp.concatenate([cos, cos, cos, cos], axis=1)
    sin_t = jnp.concatenate([-sin, sin, -sin, sin], axis=1)
    return jnp.tile(cos_t, (reps, 1)), jnp.tile(sin_t, (reps, 1))


def _pool_kernel(x_ref, prev_ref, cw_ref, cs_ref, o_ref, ext_ref, *, pos0):
    j = pl.program_id(1)
    n_s, l_r, _ = x_ref.shape
    t_b = n_s * l_r
    hdr = POOL_MAX

    @pl.when(j == 0)
    def _():
        ext_ref[:, 0:hdr, :] = jnp.zeros((n_s, hdr, MIX_W), F32)
        ext_ref[:, 1:hdr, :] = prev_ref[...]

    @pl.when(j > 0)
    def _():
        ext_ref[:, 0:hdr, :] = ext_ref[:, l_r:l_r + hdr, :]

    ext_ref[:, hdr:hdr + l_r, :] = x_ref[...]
    pos = pos0 + j * l_r + lax.broadcasted_iota(jnp.int32, (1, l_r, 1), 1)
    outs = []
    for gi, w in enumerate(POOL_WINDOWS):
        c0 = gi * C_GW
        acc = None
        for t in range(w):
            term = ext_ref[:, hdr - t:hdr - t + l_r, c0:c0 + C_GW]
            acc = term if acc is None else acc + term
        cnt = jnp.minimum(pos + 1, w).astype(F32)
        y = acc / cnt - ext_ref[:, hdr:hdr + l_r, c0:c0 + C_GW]
        y = _mm(y.reshape(t_b, C_GW).astype(BF16), cw_ref[gi])
        outs.append(y)
    o_ref[...] = (jnp.concatenate(outs, axis=1) * cs_ref[...]).reshape(n_s, l_r, MIX_W)


def _mixer_pool(proj3, prev, c_w, c_scale, *, n_s, l_r, pos0):
    bsz, t_len, _ = proj3.shape
    grid = (bsz // n_s, t_len // l_r)
    kern = functools.partial(_pool_kernel, pos0=pos0)
    return pl.pallas_call(
        kern,
        grid=grid,
        in_specs=[
            pl.BlockSpec((n_s, l_r, MIX_W), lambda b, j: (b, j, COL_CIN // MIX_W)),
            pl.BlockSpec((n_s, POOL_MAX - 1, MIX_W), lambda b, j: (b, 0, 0)),
            pl.BlockSpec((C_GROUPS, C_GW, C_GW), lambda b, j: (0, 0, 0)),
            pl.BlockSpec((1, MIX_W), lambda b, j: (0, 0)),
        ],
        out_specs=pl.BlockSpec((n_s, l_r, MIX_W), lambda b, j: (b, j, 0)),
        out_shape=jax.ShapeDtypeStruct((bsz, t_len, MIX_W), F32),
        scratch_shapes=[pltpu.VMEM((n_s, POOL_MAX + l_r + SUBLANES, MIX_W), F32)],
        compiler_params=_cparams(("parallel", "arbitrary")),
        name="mixer_pool",
    )(proj3, prev, c_w, c_scale)


def _lookup_kernel(tab_ref, oh_ref, o_ref):
    o_ref[...] = _mm(tab_ref[...], oh_ref[...], precision=HIGHEST)


def _bias_lookup(idx, table):
    n_valid = idx.shape[0]
    tr = 4096
    n_cols = -(-n_valid // tr) * tr
    idx = np.concatenate([idx, np.zeros((n_cols - n_valid,), idx.dtype)])
    onehot = jnp.asarray(np.arange(REL_BUCKETS)[:, None] == idx[None, :], dtype=F32)
    tab = jnp.pad(table.T, ((0, SUBLANES - table.shape[1]), (0, 0)))
    out = pl.pallas_call(
        _lookup_kernel,
        grid=(n_cols // tr,),
        in_specs=[pl.BlockSpec((SUBLANES, REL_BUCKETS), lambda i: (0, 0)),
                  pl.BlockSpec((REL_BUCKETS, tr), lambda i: (0, i))],
        out_specs=pl.BlockSpec((SUBLANES, tr), lambda i: (0, i)),
        out_shape=jax.ShapeDtypeStruct((SUBLANES, n_cols), F32),
        compiler_params=_cparams(("parallel",)),
        name="bias_lookup",
    )(tab, onehot)
    return out[:table.shape[1], :n_valid]


def _t5_bucket_np(dist):
    exact = REL_BUCKETS // 2
    n = np.maximum(dist, 0)
    nf = np.maximum(n, 1).astype(np.float32)
    large = exact + (np.log(nf / np.float32(exact)) / np.float32(math.log(REL_MAX_DIST / exact))
                     * np.float32(REL_BUCKETS - exact)).astype(np.int32)
    large = np.minimum(large, REL_BUCKETS - 1)
    return np.where(n < exact, n, large).astype(np.int32)


SUPER = D_SPAN * D_DILATIONS[-1]
TILES = SUPER // D_SPAN


def _band_fused_kernel(*refs):
    ins = refs[:5 * N_DG]
    bias_ref = refs[5 * N_DG]
    o_ref = refs[5 * N_DG + 1]
    scr = refs[5 * N_DG + 2:]
    kf = scr[0:2 * N_DG:2]
    vf = scr[1:2 * N_DG:2]
    og = scr[2 * N_DG:3 * N_DG]
    lg = scr[3 * N_DG:4 * N_DG]
    has_prev = pl.program_id(2) > 0
    scale2 = (D_HD ** -0.5) * LOG2E
    for g, dil in enumerate(D_DILATIONS):
        q_ref, k_ref, v_ref, kp_ref, vp_ref = ins[5 * g:5 * g + 5]
        span = D_SPAN * dil
        kf[g][0:span, :] = kp_ref[...]
        kf[g][span:span + SUPER, :] = k_ref[...]
        vf[g][0:span, :] = vp_ref[...]
        vf[g][span:span + SUPER, :] = v_ref[...]
        sh = int(math.log2(dil))

        def tile(idx, carry, g=g, dil=dil, span=span, sh=sh, q_ref=q_ref):
            qi = lax.shift_right_logical(idx, sh)
            res = idx & (dil - 1)
            start = qi * span + res
            if dil == 1:
                start = pl.multiple_of(start, D_SPAN)
                rows = pl.ds(start, D_SPAN)
                krows = pl.ds(start, 2 * D_SPAN)
            else:
                rows = pl.ds(start, D_SPAN, stride=dil)
                krows = pl.ds(start, 2 * D_SPAN, stride=dil)
            q = q_ref[rows, :].astype(BF16)
            kk = kf[g][krows, :].astype(BF16)
            vv = vf[g][krows, :].astype(BF16)
            which = jnp.where(jnp.logical_or(has_prev, qi > 0), 1, 0)
            s = _nt(q, kk) * scale2 + bias_ref[g, which]
            m = jnp.max(s, axis=-1, keepdims=True)
            p = jnp.exp2(s - m)
            l = jnp.sum(p, axis=-1, keepdims=True)
            og[g][rows, :] = _mm(p.astype(BF16), vv) / l
            lg[g][rows, :] = jnp.broadcast_to(m + jnp.log2(l), (D_SPAN, D_HD))
            return carry

        lax.fori_loop(0, TILES, tile, 0, unroll=8)

    def merge(c, carry):
        rows = pl.ds(pl.multiple_of(c * D_SPAN, D_SPAN), D_SPAN)
        l0, l1, l2 = lg[0][rows, :], lg[1][rows, :], lg[2][rows, :]
        m = jnp.maximum(jnp.maximum(l0, l1), l2)
        e0, e1, e2 = jnp.exp2(l0 - m), jnp.exp2(l1 - m), jnp.exp2(l2 - m)
        inv = 1.0 / (e0 + e1 + e2)
        o_ref[rows, :] = (e0 * og[0][rows, :] + e1 * og[1][rows, :] + e2 * og[2][rows, :]) * inv
        return carry

    lax.fori_loop(0, TILES, merge, 0)


def _dilated_band(proj3, band_bias):
    bsz, s_len, _ = proj3.shape
    n_blk = s_len // SUPER
    base = COL_DQKV // D_HD
    in_specs = []
    args = []
    for g, dil in enumerate(D_DILATIONS):
        span = D_SPAN * dil
        per = SUPER // span

        def col(c, g=g):
            return lambda b, h, n: (b, n, base + (c * N_DG + g) * D_HEADS + h)

        def col_prev(c, g=g, per=per):
            return lambda b, h, n: (b, jnp.maximum(n * per - 1, 0), base + (c * N_DG + g) * D_HEADS + h)

        in_specs += [pl.BlockSpec((None, SUPER, D_HD), col(0)),
                     pl.BlockSpec((None, SUPER, D_HD), col(1)),
                     pl.BlockSpec((None, SUPER, D_HD), col(2)),
                     pl.BlockSpec((None, span, D_HD), col_prev(1)),
                     pl.BlockSpec((None, span, D_HD), col_prev(2))]
        args += [proj3] * 5
    in_specs.append(pl.BlockSpec((N_DG, None, 2, D_SPAN, 2 * D_SPAN), lambda b, h, n: (0, h, 0, 0, 0)))
    scratch = []
    for dil in D_DILATIONS:
        scratch += [pltpu.VMEM((D_SPAN * dil + SUPER, D_HD), F32)] * 2
    scratch += [pltpu.VMEM((SUPER, D_HD), F32)] * (2 * N_DG)
    return pl.pallas_call(
        _band_fused_kernel,
        grid=(bsz, D_HEADS, n_blk),
        in_specs=in_specs,
        out_specs=pl.BlockSpec((None, SUPER, D_HD), lambda b, h, n: (b, n, h)),
        out_shape=jax.ShapeDtypeStruct((bsz, s_len, MIX_W), F32),
        scratch_shapes=scratch,
        compiler_params=_cparams(("parallel", "parallel", "arbitrary")),
        name="dilated_band",
    )(*args, band_bias)


KV_ROWS = 2 * D_HEADS


def _step_attn_kernel(q_ref, kn_ref, vn_ref, cache_ref, bias_c_ref, mask_c_ref, bias_n_ref, mask_n_ref,
                      buf_ref, o_ref, lse_ref, new_ref, padk_ref, padv_ref, *, grouped):
    del buf_ref
    t_new = q_ref.shape[0]
    scale = D_HD ** -0.5
    zeros_q = jnp.zeros((SUBLANES, D_HD), F32)
    heads = range(D_HEADS)
    cols = [slice(h * D_HD, (h + 1) * D_HD) for h in heads]

    def pad16(p):
        return jnp.concatenate([p, jnp.zeros_like(p)], axis=0).astype(BF16)

    padk_ref[...] = jnp.zeros(padk_ref.shape, F32)
    padv_ref[...] = jnp.zeros(padv_ref.shape, F32)
    for h in heads:
        new_ref[pl.ds(h, t_new, stride=KV_ROWS), :] = kn_ref[:, cols[h]]
        new_ref[pl.ds(D_HEADS + h, t_new, stride=KV_ROWS), :] = vn_ref[:, cols[h]]
        padk_ref[h, 0:t_new, :] = kn_ref[:, cols[h]]
        padv_ref[h, 0:t_new, :] = vn_ref[:, cols[h]]

    if grouped:
        n_grp, kept = cache_ref.shape[0], cache_ref.shape[1] // KV_ROWS
        kh = [cache_ref[:, pl.ds(h, kept, stride=KV_ROWS), :].reshape(n_grp * kept, D_HD).astype(BF16)
              for h in heads]
        vh = [cache_ref[:, pl.ds(D_HEADS + h, kept, stride=KV_ROWS), :].reshape(n_grp * kept, D_HD).astype(BF16)
              for h in heads]
    else:
        n_pos = cache_ref.shape[0] // KV_ROWS
        kh = [cache_ref[pl.ds(h, n_pos, stride=KV_ROWS), :].astype(BF16) for h in heads]
        vh = [cache_ref[pl.ds(D_HEADS + h, n_pos, stride=KV_ROWS), :].astype(BF16) for h in heads]
    lhs = [jnp.concatenate([q_ref[:, cols[h]], zeros_q], axis=0).astype(BF16) for h in heads]
    s_n = [_nt(lhs[h], padk_ref[h].astype(BF16))[0:t_new] * scale + bias_n_ref[h] + mask_n_ref[...]
           for h in heads]
    s_c = [_nt(lhs[h], kh[h])[0:t_new] * scale + bias_c_ref[h] + mask_c_ref[...] for h in heads]
    m = [jnp.maximum(jnp.max(s_n[h], axis=-1, keepdims=True), jnp.max(s_c[h], axis=-1, keepdims=True))
         for h in heads]
    p_n = [jnp.exp(s_n[h] - m[h]) for h in heads]
    p_c = [jnp.exp(s_c[h] - m[h]) for h in heads]
    l = [jnp.sum(p_n[h], axis=-1, keepdims=True) + jnp.sum(p_c[h], axis=-1, keepdims=True) for h in heads]
    acc = [_mm(pad16(p_n[h]), padv_ref[h].astype(BF16))[0:t_new] + _mm(pad16(p_c[h]), vh[h])[0:t_new]
           for h in heads]
    for h in heads:
        o_ref[:, cols[h]] = acc[h] / l[h]
        lse_ref[:, h:h + 1] = m[h] + jnp.log(l[h])


def _dilated_step(proj3, cache_flat, out_buf, layer, gi, dil, bias_c, mask_c, bias_n, mask_n):
    bsz, t_new, _ = proj3.shape
    flat_rows = cache_flat.shape[1]
    l_buf = flat_rows // KV_ROWS
    shift = t_new * KV_ROWS
    base = COL_DQKV // DBLK
    cq, ck, cv = base + gi, base + N_DG + gi, base + 2 * N_DG + gi
    grouped = t_new < dil
    if grouped:
        cache_arg = cache_flat.reshape(cache_flat.shape[0], l_buf // dil, dil * KV_ROWS, D_HD)
        cache_spec = pl.BlockSpec((None, l_buf // dil, shift, D_HD), lambda b: (layer * bsz + b, 0, 0, 0))
    else:
        cache_arg = cache_flat
        cache_spec = pl.BlockSpec((None, flat_rows, D_HD), lambda b: (layer * bsz + b, 0, 0))
    n_keys = bias_c.shape[-1]
    return pl.pallas_call(
        functools.partial(_step_attn_kernel, grouped=grouped),
        grid=(bsz,),
        in_specs=[
            pl.BlockSpec((None, t_new, DBLK), lambda b: (b, 0, cq)),
            pl.BlockSpec((None, t_new, DBLK), lambda b: (b, 0, ck)),
            pl.BlockSpec((None, t_new, DBLK), lambda b: (b, 0, cv)),
            cache_spec,
            pl.BlockSpec((D_HEADS, t_new, n_keys), lambda b: (0, 0, 0)),
            pl.BlockSpec((t_new, n_keys), lambda b: (0, 0)),
            pl.BlockSpec((D_HEADS, t_new, LANES), lambda b: (0, 0, 0)),
            pl.BlockSpec((t_new, LANES), lambda b: (0, 0)),
            pl.BlockSpec(memory_space=pl.ANY),
        ],
        out_specs=[
            pl.BlockSpec((None, t_new, DBLK), lambda b: (b, 0, 0)),
            pl.BlockSpec((None, t_new, D_HEADS), lambda b: (b, 0, 0)),
            pl.BlockSpec((None, shift, D_HD), lambda b: (layer * bsz + b, flat_rows // shift - 1, 0)),
        ],
        out_shape=[jax.ShapeDtypeStruct((bsz, t_new, DBLK), F32),
                   jax.ShapeDtypeStruct((bsz, t_new, D_HEADS), F32),
                   jax.ShapeDtypeStruct(out_buf.shape, F32)],
        scratch_shapes=[pltpu.VMEM((D_HEADS, LANES, D_HD), F32),
                        pltpu.VMEM((D_HEADS, LANES, D_HD), F32)],
        input_output_aliases={8: 2},
        compiler_params=_cparams(("parallel",)),
        name="dilated_step",
    )(proj3, proj3, proj3, cache_arg, bias_c, mask_c, bias_n, mask_n, out_buf)


def _win_kernel(k_ref, v_ref, buf_ref, out_ref):
    del buf_ref
    n_pos = k_ref.shape[0]
    for h in range(D_HEADS):
        cs = slice(h * D_HD, (h + 1) * D_HD)
        out_ref[pl.ds(h, n_pos, stride=KV_ROWS), :] = k_ref[:, cs]
        out_ref[pl.ds(D_HEADS + h, n_pos, stride=KV_ROWS), :] = v_ref[:, cs]


def _win_extract(proj3, out_buf, layer, gi, keep):
    bsz, t_len, _ = proj3.shape
    n_pos = min(keep, 512)
    first = (t_len - keep) // n_pos
    base = COL_DQKV // DBLK
    ck, cv = base + N_DG + gi, base + 2 * N_DG + gi
    return pl.pallas_call(
        _win_kernel,
        grid=(bsz, keep // n_pos),
        in_specs=[pl.BlockSpec((None, n_pos, DBLK), lambda b, c: (b, first + c, ck)),
                  pl.BlockSpec((None, n_pos, DBLK), lambda b, c: (b, first + c, cv)),
                  pl.BlockSpec(memory_space=pl.ANY)],
        out_specs=pl.BlockSpec((None, n_pos * KV_ROWS, D_HD), lambda b, c: (layer * bsz + b, c, 0)),
        out_shape=jax.ShapeDtypeStruct(out_buf.shape, F32),
        input_output_aliases={2: 0},
        compiler_params=_cparams(("parallel", "arbitrary")),
        name="win_extract",
    )(proj3, proj3, out_buf)


def _merge_kernel(*refs, split_d):
    if split_d:
        (gmix_ref, wgin_ref, ya_ref, yb_ref, yc_ref, o0_ref, o1_ref, o2_ref, l0_ref, l1_ref, l2_ref,
         wbr_ref, wo_ref, h_ref, out_ref) = refs
        l0, l1, l2 = l0_ref[...], l1_ref[...], l2_ref[...]
        m = jnp.maximum(jnp.maximum(l0, l1), l2)
        e0, e1, e2 = jnp.exp(l0 - m), jnp.exp(l1 - m), jnp.exp(l2 - m)
        inv = 1.0 / (e0 + e1 + e2)
        w0, w1, w2 = e0 * inv, e1 * inv, e2 * inv
        yd = []
        for h in range(D_HEADS):
            cs = slice(h * D_HD, (h + 1) * D_HD)
            yd.append(w0[:, h:h + 1] * o0_ref[:, cs] + w1[:, h:h + 1] * o1_ref[:, cs]
                      + w2[:, h:h + 1] * o2_ref[:, cs])
        y_d = jnp.concatenate(yd, axis=1)
    else:
        gmix_ref, wgin_ref, ya_ref, yb_ref, yc_ref, yd_ref, wbr_ref, wo_ref, h_ref, out_ref = refs
        y_d = yd_ref[...]
    branches = [ya_ref[...], yb_ref[...], yc_ref[...], y_d]
    h = h_ref[...]
    u = _rms_rows(h, gmix_ref[...]).astype(BF16)
    merged = None
    for nbr in range(N_BRANCH):
        up = _mm(branches[nbr].astype(BF16), wbr_ref[nbr])
        gate = _mm(u, wgin_ref[:, nbr * D_MODEL:(nbr + 1) * D_MODEL])
        term = _sigmoid(gate) * up
        merged = term if merged is None else merged + term
    out_ref[...] = h + _mm(merged.astype(BF16), wo_ref[...])


def _merge(g_mix, w_gin, ya, yb, yc, yd, w_br, w_o, layer, h2, tm):
    n = h2.shape[0]
    row = lambda width: pl.BlockSpec((tm, width), lambda i: (i, 0))
    split_d = isinstance(yd, tuple)
    if split_d:
        d_args = list(yd[0]) + list(yd[1])
        d_specs = [row(MIX_W)] * N_DG + [row(D_HEADS)] * N_DG
    else:
        d_args = [yd]
        d_specs = [row(MIX_W)]
    return pl.pallas_call(
        functools.partial(_merge_kernel, split_d=split_d),
        grid=(n // tm,),
        in_specs=[pl.BlockSpec((1, D_MODEL), lambda i: (0, 0)),
                  pl.BlockSpec((None, D_MODEL, GATE_W), lambda i: (layer, 0, 0), pipeline_mode=pl.Buffered(1)),
                  row(MIX_W), row(MIX_W), row(MIX_W)] + d_specs + [
                  pl.BlockSpec((None, N_BRANCH, MIX_W, D_MODEL), lambda i: (layer, 0, 0, 0),
                               pipeline_mode=pl.Buffered(1)),
                  pl.BlockSpec((None, D_MODEL, D_MODEL), lambda i: (layer, 0, 0), pipeline_mode=pl.Buffered(1)),
                  row(D_MODEL)],
        out_specs=row(D_MODEL),
        out_shape=jax.ShapeDtypeStruct((n, D_MODEL), F32),
        compiler_params=_cparams(("parallel",)),
        name="branch_merge",
    )(g_mix.reshape(1, D_MODEL), w_gin, ya, yb, yc, *d_args, w_br, w_o, h2)


def _ffn_kernel(h_ref, gffn_ref, wg_ref, wu_ref, wd_ref, p_ref, gple_ref, wpg_ref, wple_ref, gfin_ref,
                out_ref, *, final, n_split):
    h = h_ref[...]
    f = _rms_rows(h, gffn_ref[...]).astype(BF16)
    tf = D_FF // n_split
    acc = None
    for k in range(n_split):
        cs = slice(k * tf, (k + 1) * tf)
        a = _mm(f, wg_ref[:, cs])
        b = _mm(f, wu_ref[:, cs])
        part = _mm((_silu(a) * b).astype(BF16), wd_ref[cs, :])
        acc = part if acc is None else acc + part
    h2 = h + acc
    e = _rms_rows(h2, gple_ref[...]).astype(BF16)
    gate = _sigmoid(_mm(e, wpg_ref[...]))
    h3 = h2 + gate * _mm(p_ref[...].astype(BF16), wple_ref[...])
    if final:
        out_ref[...] = _rms_rows(h3, gfin_ref[...])
    else:
        out_ref[...] = h3


def _ffn(h2, g_ffn, w_gate, w_up, w_down, p3, g_ple, w_pg, w_ple, g_fin, layer, tm, n_split, final):
    n = h2.shape[0]
    vec = lambda: pl.BlockSpec((1, D_MODEL), lambda i: (0, 0))
    res = lambda shape: pl.BlockSpec((None,) + shape, lambda i: (layer, 0, 0), pipeline_mode=pl.Buffered(1))
    return pl.pallas_call(
        functools.partial(_ffn_kernel, final=final, n_split=n_split),
        grid=(n // tm,),
        in_specs=[pl.BlockSpec((tm, D_MODEL), lambda i: (i, 0)),
                  vec(),
                  res((D_MODEL, D_FF)),
                  res((D_MODEL, D_FF)),
                  res((D_FF, D_MODEL)),
                  pl.BlockSpec((None, tm, PLE_DIM), lambda i: (layer, i, 0)),
                  vec(),
                  res((D_MODEL, D_MODEL)),
                  res((PLE_DIM, D_MODEL)),
                  vec()],
        out_specs=pl.BlockSpec((tm, D_MODEL), lambda i: (i, 0)),
        out_shape=jax.ShapeDtypeStruct((n, D_MODEL), F32),
        compiler_params=_cparams(("parallel",)),
        name="ffn_ple",
    )(h2, g_ffn.reshape(1, -1), w_gate, w_up, w_down, p3, g_ple.reshape(1, -1), w_pg, w_ple,
      g_fin.reshape(1, -1))


def _prep_weights(w_in, w_br, w_o, w_gate, w_up, w_down, w_ple_gate, w_ple, a_log, dt_bias):
    o_ab = A_QKV + MIX_W
    o_bq = o_ab + 2 * A_HEADS
    o_d = o_bq + 2 * B_HEADS * B_DK + 3 * MIX_W
    o_g = o_d + 3 * N_DG * MIX_W
    w_main = jnp.concatenate([w_in[:, :, :o_ab], w_in[:, :, o_bq:o_g]], axis=-1).astype(BF16)
    w_gin = w_in[:, :, o_g:].astype(BF16)
    w_ab = jnp.pad(w_in[:, :, o_ab:o_bq], ((0, 0), (0, 0), (0, LANES - 2 * A_HEADS))).astype(BF16)
    par = jnp.zeros((a_log.shape[0], SUBLANES, LANES), F32)
    par = par.at[:, 0, A_HEADS:2 * A_HEADS].set(a_log)
    par = par.at[:, 1, A_HEADS:2 * A_HEADS].set(dt_bias)
    return dict(w_main=w_main, w_gin=w_gin, w_ab=w_ab, par=par, w_br=w_br.astype(BF16), w_o=w_o.astype(BF16),
                w_gate=w_gate.astype(BF16), w_up=w_up.astype(BF16), w_down=w_down.astype(BF16),
                w_pg=w_ple_gate.astype(BF16), w_ple=w_ple.astype(BF16))


def _band_bias(t5_bias):
    i = np.arange(D_SPAN)[:, None]
    j = np.arange(2 * D_SPAN)[None, :]
    rel = i + D_SPAN - j
    band = (rel >= 0) & (rel <= D_SPAN)
    masks = np.stack([band & (j >= D_SPAN), band])
    out = []
    for gi, dil in enumerate(D_DILATIONS):
        idx = _t5_bucket_np(rel * dil).reshape(-1)
        tab = _bias_lookup(idx, t5_bias[:, gi * D_HEADS:(gi + 1) * D_HEADS])
        tab = tab.reshape(D_HEADS, D_SPAN, 2 * D_SPAN)
        out.append(jnp.where(masks[None], tab[:, None] * LOG2E, -jnp.inf))
    return jnp.stack(out)


def _step_bias(t5_bias, l_buf, t_new, gi, dil):
    t = np.arange(t_new)[:, None]
    j = np.arange(l_buf + LANES)[None, :]
    dist = l_buf + t - j
    valid = (dist >= 0) & (dist % dil == 0) & (dist <= D_SPAN * dil) & (j < l_buf + t_new)
    idx = _t5_bucket_np(np.where(valid, dist, 0)).reshape(-1)
    tab = _bias_lookup(idx, t5_bias[:, gi * D_HEADS:(gi + 1) * D_HEADS])
    tab = tab.reshape(D_HEADS, t_new, l_buf + LANES)
    tab = jnp.where(valid[None], tab, 0.0)
    mask = jnp.asarray(np.where(valid, 0.0, -np.inf), dtype=F32)
    pos = np.arange(l_buf)
    keep = pos[(pos % dil) < t_new] if t_new < dil else pos
    return tab[:, :, keep], mask[:, keep], tab[:, :, l_buf:], mask[:, l_buf:]


def _layer_common(h2, p3, bsz, t_len, wts, lw, i, mix_fn, tm, final, g_final):
    n = bsz * t_len
    proj2, ab2 = _norm_matmul(h2, lw['g_mix'][i], wts['w_main'], wts['w_ab'], i, tm, PROJ_W // 4)
    proj3 = proj2.reshape(bsz, t_len, PROJ_W)
    ab3 = ab2.reshape(bsz, t_len, LANES)
    ya, yb, yc, yd, states = mix_fn(proj3, ab3)
    flat = lambda x: x.reshape(n, x.shape[-1])
    if isinstance(yd, tuple):
        yd = ([flat(x) for x in yd[0]], [flat(x) for x in yd[1]])
    else:
        yd = flat(yd)
    tm2 = min(tm, 512)
    h2 = _merge(lw['g_mix'][i], wts['w_gin'], flat(ya), flat(yb), flat(yc), yd, wts['w_br'], wts['w_o'], i, h2, tm2)
    tm3 = min(tm, 512)
    h2 = _ffn(h2, lw['g_ffn'][i], wts['w_gate'], wts['w_up'], wts['w_down'], p3,
              lw['g_ple'][i], wts['w_pg'], wts['w_ple'], g_final, i, tm3, 2, final)
    return h2, proj3, states


def _run_prompt(x, p, wts, lw, t5_bias, g_final, depth, host):
    bsz, t_len, _ = x.shape
    n = bsz * t_len
    h2 = x.reshape(n, D_MODEL)
    p3 = p.reshape(depth, n, PLE_DIM)
    pos = jnp.arange(t_len, dtype=jnp.int32)
    cos_t, sin_t = _rope_tables(pos, 1)
    band_bias = _band_bias(t5_bias)
    zero_s = jnp.zeros((1, bsz, A_HEADS, A_DK, A_DV), F32)
    zero_c = jnp.zeros((bsz, A_CONV - 1, A_QKV), F32)
    zero_r = jnp.zeros((bsz, B_HEADS // 2, LANES, B_DV), F32)
    zero_p = jnp.zeros((bsz, POOL_MAX - 1, MIX_W), F32)
    keeps = [min(D_SPAN * dil, t_len) for dil in D_DILATIONS]
    win_bufs = [lax.empty((depth * bsz, keep * KV_ROWS, D_HD), F32) for keep in keeps]
    per_layer = []
    for i in range(depth):
        def mix_fn(proj3, ab3, i=i):
            ya, s_new, *shifted = _mixer_delta(
                proj3, ab3, zero_c, zero_s, 0, lw['conv_w'][i], wts['par'][i], lw['a_gain'][i].reshape(1, A_DV),
                n_s=1, l_r=256, chunk=A_CHUNK, carry=True,
                host=(host[0], host[1], i * host[2], host[2], host[3]))
            host[1][:] = shifted
            yb, r_new = _mixer_ret(proj3, cos_t, sin_t, zero_r, n_s=1, l_r=512, chunk=B_CHUNK, carry=True)
            yc = _mixer_pool(proj3, zero_p, lw['c_w'][i].astype(BF16), lw['c_scale'][i].reshape(1, MIX_W),
                             n_s=1, l_r=1024, pos0=0)
            yd = _dilated_band(proj3, band_bias)
            return ya, yb, yc, yd, (s_new, r_new)

        h2, proj3, (s_new, r_new) = _layer_common(h2, p3, bsz, t_len, wts, lw, i,
                                                  mix_fn, 1024, i == depth - 1, g_final)
        for gi in range(N_DG):
            win_bufs[gi] = _win_extract(proj3, win_bufs[gi], i, gi, keeps[gi])
        per_layer.append((s_new,
                          proj3[:, t_len - (A_CONV - 1):, COL_AQKV:COL_AQKV + A_QKV],
                          r_new.reshape(bsz, B_HEADS, B_DK, B_DV),
                          proj3[:, t_len - (POOL_MAX - 1):, COL_CIN:COL_CIN + MIX_W]))
    stacked = tuple(jnp.stack([ns[j] for ns in per_layer]) for j in range(4))
    wins = tuple(win_bufs[gi].reshape(depth, bsz, keeps[gi], 2, D_HEADS, D_HD) for gi in range(N_DG))
    return h2.reshape(bsz, t_len, D_MODEL), stacked + wins


def _run_sample(x, p, states, wts, lw, t5_bias, g_final, depth, cache_flat, out_bufs):
    s_delta, s_conv, s_ret, s_pool, caches = states
    bsz, t_len, _ = x.shape
    n = bsz * t_len
    n_s = 16
    h2 = x.reshape(n, D_MODEL)
    p3 = p.reshape(depth, n, PLE_DIM)
    pos = PAST_LEN + jnp.arange(t_len, dtype=jnp.int32)
    cos_t, sin_t = _rope_tables(pos, n_s)
    step_bias = [_step_bias(t5_bias, caches[gi].shape[2], t_len, gi, dil) for gi, dil in enumerate(D_DILATIONS)]
    s_ret2 = s_ret.reshape(s_ret.shape[0], bsz, B_HEADS // 2, LANES, B_DV)
    per_layer = []
    for i in range(depth):
        def mix_fn(proj3, ab3, i=i):
            ya, s_new = _mixer_delta(proj3, ab3, s_conv[i], s_delta, i, lw['conv_w'][i], wts['par'][i],
                                     lw['a_gain'][i].reshape(1, A_DV), n_s=n_s, l_r=t_len, chunk=t_len, carry=False)
            yb, r_new = _mixer_ret(proj3, cos_t, sin_t, s_ret2[i], n_s=n_s, l_r=t_len, chunk=t_len, carry=False)
            yc = _mixer_pool(proj3, s_pool[i], lw['c_w'][i].astype(BF16), lw['c_scale'][i].reshape(1, MIX_W),
                             n_s=n_s, l_r=t_len, pos0=PAST_LEN)
            od, lse = [], []
            for gi, dil in enumerate(D_DILATIONS):
                o, l, out_bufs[gi] = _dilated_step(proj3, cache_flat[gi], out_bufs[gi], i, gi, dil, *step_bias[gi])
                od.append(o)
                lse.append(l)
            return ya, yb, yc, (od, lse), (s_new, r_new)

        h2, proj3, (s_new, r_new) = _layer_common(h2, p3, bsz, t_len, wts, lw, i,
                                                  mix_fn, n, i == depth - 1, g_final)
        pool_new = jnp.concatenate([s_pool[i][:, t_len:], proj3[:, :, COL_CIN:COL_CIN + MIX_W]], axis=1)
        per_layer.append((s_new,
                          proj3[:, t_len - (A_CONV - 1):, COL_AQKV:COL_AQKV + A_QKV],
                          r_new.reshape(bsz, B_HEADS, B_DK, B_DV),
                          pool_new))
    stacked = tuple(jnp.stack([ns[j] for ns in per_layer]) for j in range(4))
    wins = tuple(out_bufs[gi].reshape(caches[gi].shape) for gi in range(N_DG))
    return h2.reshape(bsz, t_len, D_MODEL), stacked + wins


def kernel(x_prompt, x_sample, state_delta, state_delta_conv, state_ret, state_pool, cache_win0, cache_win1, cache_win2, p_prompt, p_sample, g_mix, w_in, conv_w, a_log, dt_bias, a_gain, c_w, c_scale, t5_bias, w_br, w_o, g_ffn, w_gate, w_up, w_down, g_ple, w_ple_gate, w_ple, g_final):
    depth = w_in.shape[0]
    wts = _prep_weights(w_in, w_br, w_o, w_gate, w_up, w_down, w_ple_gate, w_ple, a_log, dt_bias)
    lw = dict(g_mix=g_mix, conv_w=conv_w, a_gain=a_gain, c_w=c_w, c_scale=c_scale, g_ffn=g_ffn, g_ple=g_ple)
    caches = (cache_win0, cache_win1, cache_win2)
    bsz_s, t_new = x_sample.shape[0], x_sample.shape[1]
    cache_flat = [c.reshape(c.shape[0] * bsz_s, c.shape[2] * KV_ROWS, D_HD) for c in caches]
    out_bufs = [lax.empty(c.shape, F32) for c in cache_flat]
    y_p, (delta_p, conv_p, ret_p, pool_p, win0_p, win1_p, win2_p) = _run_prompt(
        x_prompt, p_prompt, wts, lw, t5_bias, g_final, depth, (cache_flat, out_bufs, bsz_s, t_new * KV_ROWS))
    y_s, (delta_s, conv_s, ret_s, pool_s, win0_s, win1_s, win2_s) = _run_sample(
        x_sample, p_sample, (state_delta, state_delta_conv, state_ret, state_pool, caches),
        wts, lw, t5_bias, g_final, depth, cache_flat, out_bufs)
    return (y_p, y_s, delta_p, delta_s, conv_p, conv_s, ret_p, ret_s,
            pool_p, pool_s, win0_p, win0_s, win1_p, win1_s, win2_p, win2_s)
```
